```python
import math
import jax, jax.numpy as jnp
from jax import lax
import numpy as np

D_MODEL = 2048
BATCH = 2
SEQ = 8192
DEPTH = 1
DEC_BATCH = 16
DEC_SEQ = 64
PAST_LEN = 2048

CHUNK = 64
HEAD_DIM = 64
CONV_WIDTH = D_MODEL // 2
CONV_K = 3
N_HEADS = (D_MODEL - CONV_WIDTH) // HEAD_DIM
N_KV_HEADS = 4
GROUP = N_HEADS // N_KV_HEADS
ATTN_WIDTH = N_HEADS * HEAD_DIM
KV_WIDTH = N_KV_HEADS * HEAD_DIM
MIX_WIDTH = CONV_WIDTH + ATTN_WIDTH
IN_COLS = 3 * CONV_WIDTH + ATTN_WIDTH + 2 * KV_WIDTH
WINDOW = 128
WINDOW_CHUNKS = WINDOW // CHUNK
BAND = (WINDOW_CHUNKS + 1) * CHUNK
NUM_BUCKETS = 32
MAX_DISTANCE = 128
D_FF = -(-8 * D_MODEL // (3 * 256)) * 256
EPS = 1e-6
NEG = -1e30

kernel_name = "hybrid_conv_swa_streaming_step"


def rmsnorm(x, g):
    xf = x.astype(jnp.float32)
    y = xf * lax.rsqrt(jnp.mean(xf * xf, axis=-1, keepdims=True) + EPS) * g.astype(jnp.float32)
    return y.astype(x.dtype)


def t5_bucket(rel):
    half = NUM_BUCKETS // 2
    ret = jnp.where(rel > 0, half, 0)
    n = jnp.abs(rel)
    max_exact = half // 2
    nf = jnp.maximum(n, 1).astype(jnp.float32)
    large = max_exact + (jnp.log(nf / max_exact) / math.log(MAX_DISTANCE / max_exact)
                         * (half - max_exact)).astype(jnp.int32)
    large = jnp.minimum(large, half - 1)
    return ret + jnp.where(n < max_exact, n, large)


def rel_bias(table, qpos, kpos):
    b = t5_bucket(kpos[None, :] - qpos[:, None])
    bias = jnp.transpose(table[b].astype(jnp.float32), (2, 0, 1))
    return bias.reshape(N_KV_HEADS, GROUP, qpos.shape[0], kpos.shape[0])


def split_proj(z):
    offs = np.cumsum([CONV_WIDTH, CONV_WIDTH, CONV_WIDTH, ATTN_WIDTH, KV_WIDTH]).tolist()
    return jnp.split(z, offs, axis=-1)


def causal_conv(gp, conv_w):
    L = gp.shape[1] - (CONV_K - 1)
    out = conv_w[0] * gp[:, 0:L]
    for i in range(1, CONV_K):
        out = out + conv_w[i] * gp[:, i:i + L]
    return out


def sink_attention(q, k, v, bias, key_valid, sinks):
    s = jnp.einsum('bnqkgd,bnskd->bnkgqs', q, k).astype(jnp.float32) * (HEAD_DIM ** -0.5)
    s = s + bias[None, None]
    s = jnp.where(key_valid[None, :, None, None, None, :], s, NEG)
    sink = sinks.astype(jnp.float32).reshape(N_KV_HEADS, GROUP)[None, None, :, :, None, None]
    m = jnp.maximum(jnp.max(s, axis=-1, keepdims=True), sink)
    p = jnp.exp(s - m)
    p = p / (jnp.sum(p, axis=-1, keepdims=True) + jnp.exp(sink - m))
    return jnp.einsum('bnkgqs,bnskd->bnqkgd', p.astype(v.dtype), v)


def mix_prompt(h, w_in, conv_w, sinks, rel_table):
    b, L, _ = h.shape
    cb, cc, cu, q, k, v = split_proj(h @ w_in)
    g = cc * cu
    gp = jnp.pad(g, ((0, 0), (CONV_K - 1, 0), (0, 0)))
    conv_out = cb * causal_conv(gp, conv_w)
    nc = L // CHUNK
    pad = WINDOW_CHUNKS * CHUNK
    q = q.reshape(b, nc, CHUNK, N_KV_HEADS, GROUP, HEAD_DIM)
    k = k.reshape(b, L, N_KV_HEADS, HEAD_DIM)
    v = v.reshape(b, L, N_KV_HEADS, HEAD_DIM)
    kc = jnp.pad(k, ((0, 0), (pad, 0), (0, 0), (0, 0))).reshape(b, nc + WINDOW_CHUNKS, CHUNK, N_KV_HEADS, HEAD_DIM)
    vc = jnp.pad(v, ((0, 0), (pad, 0), (0, 0), (0, 0))).reshape(b, nc + WINDOW_CHUNKS, CHUNK, N_KV_HEADS, HEAD_DIM)
    kb = jnp.concatenate([kc[:, i:i + nc] for i in range(WINDOW_CHUNKS + 1)], axis=2)
    vb = jnp.concatenate([vc[:, i:i + nc] for i in range(WINDOW_CHUNKS + 1)], axis=2)
    qpos = jnp.arange(CHUNK)
    kpos = jnp.arange(BAND) - pad
    bias = rel_bias(rel_table, qpos, kpos)
    valid = (jnp.arange(nc)[:, None] * CHUNK + kpos[None, :]) >= 0
    o = sink_attention(q, kb, vb, bias, valid, sinks)
    attn_out = o.reshape(b, L, ATTN_WIDTH)
    mixed = jnp.concatenate([conv_out, attn_out], axis=-1)
    return mixed, g[:, L - (CONV_K - 1):], k[:, L - WINDOW:], v[:, L - WINDOW:]


def mix_sample(h, conv_state, k_cache, v_cache, w_in, conv_w, sinks, rel_table):
    b, L, _ = h.shape
    cb, cc, cu, q, k, v = split_proj(h @ w_in)
    g = cc * cu
    gp = jnp.concatenate([conv_state.astype(g.dtype), g], axis=1)
    conv_out = cb * causal_conv(gp, conv_w)
    lc = k_cache.shape[1]
    k = k.reshape(b, L, N_KV_HEADS, HEAD_DIM)
    v = v.reshape(b, L, N_KV_HEADS, HEAD_DIM)
    kf = jnp.concatenate([k_cache.astype(k.dtype), k], axis=1)
    vf = jnp.concatenate([v_cache.astype(v.dtype), v], axis=1)
    qpos = PAST_LEN + jnp.arange(L)
    kpos = PAST_LEN - lc + jnp.arange(lc + L)
    bias = rel_bias(rel_table, qpos, kpos)
    valid = (kpos >= 0)[None, :]
    o = sink_attention(q.reshape(b, 1, L, N_KV_HEADS, GROUP, HEAD_DIM), kf[:, None], vf[:, None],
                       bias, valid, sinks)
    attn_out = o.reshape(b, L, ATTN_WIDTH)
    mixed = jnp.concatenate([conv_out, attn_out], axis=-1)
    n_all = lc + L
    return mixed, gp[:, L:], kf[:, n_all - lc:], vf[:, n_all - lc:]


def ffn(x, w_gate, w_up, w_down):
    return (jax.nn.silu(x @ w_gate) * (x @ w_up)) @ w_down


def setup_inputs(seed: int = 0) -> dict:
    key = jax.random.key(seed)
    ks = jax.random.split(key, 20)
    f32 = jnp.float32
    lc = min(WINDOW, PAST_LEN)

    def nrm(k, shape, scale):
        return jax.random.normal(k, shape, f32) * scale

    def gain(k):
        return 1.0 + 0.05 * jax.random.normal(k, (DEPTH, D_MODEL), f32)

    return {
        "x_prompt": nrm(ks[0], (BATCH, SEQ, D_MODEL), 1.0),
        "x_sample": nrm(ks[1], (DEC_BATCH, DEC_SEQ, D_MODEL), 1.0),
        "state_conv": nrm(ks[2], (DEPTH, DEC_BATCH, CONV_K - 1, CONV_WIDTH), 1.0),
        "cache_k": nrm(ks[3], (DEPTH, DEC_BATCH, lc, N_KV_HEADS, HEAD_DIM), 1.0),
        "cache_v": nrm(ks[4], (DEPTH, DEC_BATCH, lc, N_KV_HEADS, HEAD_DIM), 1.0),
        "rel_table": nrm(ks[5], (NUM_BUCKETS, N_HEADS), 0.5),
        "g_pre_mix": gain(ks[6]),
        "w_in": nrm(ks[7], (DEPTH, D_MODEL, IN_COLS), D_MODEL ** -0.5),
        "conv_w": nrm(ks[8], (DEPTH, CONV_K, CONV_WIDTH), CONV_K ** -0.5),
        "attn_sinks": nrm(ks[9], (DEPTH, N_HEADS), 0.5),
        "w_out": nrm(ks[10], (DEPTH, MIX_WIDTH, D_MODEL), MIX_WIDTH ** -0.5),
        "g_post_mix": gain(ks[11]),
        "g_pre_ffn": gain(ks[12]),
        "w_gate": nrm(ks[13], (DEPTH, D_MODEL, D_FF), D_MODEL ** -0.5),
        "w_up": nrm(ks[14], (DEPTH, D_MODEL, D_FF), D_MODEL ** -0.5),
        "w_down": nrm(ks[15], (DEPTH, D_FF, D_MODEL), D_FF ** -0.5),
        "g_post_ffn": gain(ks[16]),
    }


def reference(x_prompt, x_sample, state_conv, cache_k, cache_v, rel_table, g_pre_mix, w_in,
              conv_w, attn_sinks, w_out, g_post_mix, g_pre_ffn, w_gate, w_up, w_down, g_post_ffn):
    yp, ys = x_prompt, x_sample
    pc, pk, pv, sc, sk, sv = [], [], [], [], [], []
    for l in range(DEPTH):
        m, c_new, k_new, v_new = mix_prompt(rmsnorm(yp, g_pre_mix[l]), w_in[l], conv_w[l],
                                            attn_sinks[l], rel_table)
        yp = yp + rmsnorm(m @ w_out[l], g_post_mix[l])
        yp = yp + rmsnorm(ffn(rmsnorm(yp, g_pre_ffn[l]), w_gate[l], w_up[l], w_down[l]), g_post_ffn[l])
        pc.append(c_new); pk.append(k_new); pv.append(v_new)
        m, c_new, k_new, v_new = mix_sample(rmsnorm(ys, g_pre_mix[l]), state_conv[l], cache_k[l],
                                            cache_v[l], w_in[l], conv_w[l], attn_sinks[l], rel_table)
        ys = ys + rmsnorm(m @ w_out[l], g_post_mix[l])
        ys = ys + rmsnorm(ffn(rmsnorm(ys, g_pre_ffn[l]), w_gate[l], w_up[l], w_down[l]), g_post_ffn[l])
        sc.append(c_new); sk.append(k_new); sv.append(v_new)
    new_conv_prompt = jnp.stack(pc)
    new_k_prompt = jnp.stack(pk)
    new_v_prompt = jnp.stack(pv)
    new_conv_sample = jnp.stack(sc)
    new_k_sample = jnp.stack(sk)
    new_v_sample = jnp.stack(sv)
    return (yp, ys, new_conv_prompt, new_k_prompt, new_v_prompt, new_conv_sample, new_k_sample, new_v_sample)
```

```python
import functools
import math

import jax
import jax.numpy as jnp
import numpy as np
from jax import lax
from jax.experimental import pallas as pl
from jax.experimental.pallas import tpu as pltpu

D_MODEL = 2048
CHUNK = 64
HEAD_DIM = 64
CONV_WIDTH = D_MODEL // 2
CONV_K = 3
N_HEADS = (D_MODEL - CONV_WIDTH) // HEAD_DIM
N_KV_HEADS = 4
GROUP = N_HEADS // N_KV_HEADS
ATTN_WIDTH = N_HEADS * HEAD_DIM
KV_WIDTH = N_KV_HEADS * HEAD_DIM
IN_COLS = 3 * CONV_WIDTH + ATTN_WIDTH + 2 * KV_WIDTH
WINDOW = 128
BAND = WINDOW + CHUNK
NUM_BUCKETS = 32
MAX_DISTANCE = 128
D_FF = -(-8 * D_MODEL // (3 * 256)) * 256
EPS = 1e-6
NEG = -1e30

OFF_CB, OFF_CC, OFF_CU = 0, CONV_WIDTH, 2 * CONV_WIDTH
OFF_Q = 3 * CONV_WIDTH
OFF_KV = OFF_Q + ATTN_WIDTH

SUBLANES = 8
V7X_VMEM_LIMIT_BYTES = 56 << 20
MIX_ROWS = 512
FFN_ROWS = 512
FFN_TILE = 512
CONV_COLS = 512
NORM_ROWS = 16
CONV_ROWS = 32
G_PAD = SUBLANES

F32 = jnp.float32
BF16 = jnp.bfloat16


def _t5_bucket(rel):
  half = NUM_BUCKETS // 2
  ret = jnp.where(rel > 0, half, 0)
  n = jnp.abs(rel)
  max_exact = half // 2
  nf = jnp.maximum(n, 1).astype(jnp.float32)
  large = max_exact + (jnp.log(nf / max_exact) / math.log(MAX_DISTANCE / max_exact)
                       * (half - max_exact)).astype(jnp.int32)
  large = jnp.minimum(large, half - 1)
  return ret + jnp.where(n < max_exact, n, large)


def _rms_scale(y, gain):
  return y * lax.rsqrt(jnp.mean(y * y, axis=-1, keepdims=True) + EPS) * gain


def _mixer_kernel(tbl_ref, sink_ref, bkt_ref, x_ref, gpre_ref, win_ref, convw_ref, *rest,
                  n_seg, seg_len, has_state):
  if has_state:
    (state_ref, ck_ref, cv_ref, mixed_ref, nconv_ref, nk_ref, nv_ref,
     bias_ref, h_ref, zc_ref, q_ref, kvf_ref, kb_ref, vb_ref, gs_ref, ao_ref) = rest
  else:
    (mixed_ref, nconv_ref, nk_ref, nv_ref,
     bias_ref, h_ref, zc_ref, q_ref, kvf_ref, kb_ref, vb_ref, gs_ref, ao_ref) = rest
  rows = n_seg * seg_len
  g_stride = G_PAD + seg_len
  b_stride = WINDOW + seg_len
  first_call = (pl.program_id(0) == 0) & (pl.program_id(1) == 0)
  seq_start = pl.program_id(1) == 0

  @pl.when(first_call)
  def _():
    bkt = bkt_ref[...]
    for h in range(N_HEADS):
      acc = jnp.zeros((CHUNK, BAND), F32)
      for j in range(NUM_BUCKETS):
        acc = jnp.where(bkt == j, tbl_ref[j, h], acc)
      g = h % GROUP
      bias_ref[h // GROUP, g * CHUNK:(g + 1) * CHUNK, :] = acc

  if has_state:
    for s in range(n_seg):
      gs_ref[s * g_stride + G_PAD - (CONV_K - 1):s * g_stride + G_PAD, :] = state_ref[s]
      for kh in range(N_KV_HEADS):
        kb_ref[kh, s * b_stride:s * b_stride + WINDOW, :] = (
            ck_ref[s, :, kh * HEAD_DIM:(kh + 1) * HEAD_DIM].astype(BF16))
        vb_ref[kh, s * b_stride:s * b_stride + WINDOW, :] = (
            cv_ref[s, :, kh * HEAD_DIM:(kh + 1) * HEAD_DIM].astype(BF16))
  else:
    @pl.when(seq_start)
    def _():
      gs_ref[0:G_PAD, :] = jnp.zeros((G_PAD, CONV_WIDTH), F32)
      kb_ref[:, 0:WINDOW, :] = jnp.zeros((N_KV_HEADS, WINDOW, HEAD_DIM), BF16)
      vb_ref[:, 0:WINDOW, :] = jnp.zeros((N_KV_HEADS, WINDOW, HEAD_DIM), BF16)

    @pl.when(jnp.logical_not(seq_start))
    def _():
      gs_ref[0:G_PAD, :] = gs_ref[seg_len:seg_len + G_PAD, :]
      kb_ref[:, 0:WINDOW, :] = kb_ref[:, seg_len:seg_len + WINDOW, :]
      vb_ref[:, 0:WINDOW, :] = vb_ref[:, seg_len:seg_len + WINDOW, :]

  gpre = gpre_ref[...]

  def norm_body(i, carry):
    r = pl.multiple_of(i * NORM_ROWS, NORM_ROWS)
    h_ref[pl.ds(r, NORM_ROWS), :] = _rms_scale(x_ref[pl.ds(r, NORM_ROWS), :], gpre).astype(BF16)
    return carry

  lax.fori_loop(0, rows // NORM_ROWS, norm_body, 0)

  for c0 in range(0, CONV_WIDTH, CONV_COLS):
    for j, off in enumerate((OFF_CB, OFF_CC, OFF_CU)):
      zc_ref[:, j * CONV_COLS:(j + 1) * CONV_COLS] = jnp.dot(
          h_ref[...], win_ref[:, off + c0:off + c0 + CONV_COLS], preferred_element_type=F32)
    for s in range(n_seg):
      for r in range(0, seg_len, CONV_ROWS):
        zr = s * seg_len + r
        gr = s * g_stride + G_PAD + r
        gs_ref[gr:gr + CONV_ROWS, c0:c0 + CONV_COLS] = (
            zc_ref[zr:zr + CONV_ROWS, CONV_COLS:2 * CONV_COLS]
            * zc_ref[zr:zr + CONV_ROWS, 2 * CONV_COLS:3 * CONV_COLS])
    for s in range(n_seg):
      for r in range(0, seg_len, CONV_ROWS):
        zr = s * seg_len + r
        gr = s * g_stride + G_PAD + r
        conv = convw_ref[0:1, c0:c0 + CONV_COLS] * gs_ref[gr - 2:gr - 2 + CONV_ROWS, c0:c0 + CONV_COLS]
        conv = conv + convw_ref[1:2, c0:c0 + CONV_COLS] * gs_ref[gr - 1:gr - 1 + CONV_ROWS, c0:c0 + CONV_COLS]
        conv = conv + convw_ref[2:3, c0:c0 + CONV_COLS] * gs_ref[gr:gr + CONV_ROWS, c0:c0 + CONV_COLS]
        mixed_ref[zr:zr + CONV_ROWS, c0:c0 + CONV_COLS] = (
            zc_ref[zr:zr + CONV_ROWS, 0:CONV_COLS] * conv).astype(BF16)

  for c0 in range(0, ATTN_WIDTH, CONV_COLS):
    zc_ref[:, 0:CONV_COLS] = jnp.dot(
        h_ref[...], win_ref[:, OFF_Q + c0:OFF_Q + c0 + CONV_COLS], preferred_element_type=F32)
    for hh in range(CONV_COLS // HEAD_DIM):
      q_ref[c0 // HEAD_DIM + hh] = (
          zc_ref[:, hh * HEAD_DIM:(hh + 1) * HEAD_DIM] * (HEAD_DIM ** -0.5)).astype(BF16)

  kvf_ref[...] = jnp.dot(h_ref[...], win_ref[:, OFF_KV:OFF_KV + 2 * KV_WIDTH], preferred_element_type=F32)
  for s in range(n_seg):
    for kh in range(N_KV_HEADS):
      kb_ref[kh, s * b_stride + WINDOW:(s + 1) * b_stride, :] = (
          kvf_ref[s * seg_len:(s + 1) * seg_len, kh * HEAD_DIM:(kh + 1) * HEAD_DIM].astype(BF16))
      vb_ref[kh, s * b_stride + WINDOW:(s + 1) * b_stride, :] = (
          kvf_ref[s * seg_len:(s + 1) * seg_len,
                  KV_WIDTH + kh * HEAD_DIM:KV_WIDTH + (kh + 1) * HEAD_DIM].astype(BF16))

  for s in range(n_seg):
    g_end = s * g_stride + G_PAD + seg_len
    nconv_ref[s] = gs_ref[g_end - (CONV_K - 1):g_end, :]
    if seg_len >= WINDOW:
      nk_ref[s] = kvf_ref[(s + 1) * seg_len - WINDOW:(s + 1) * seg_len, 0:KV_WIDTH]
      nv_ref[s] = kvf_ref[(s + 1) * seg_len - WINDOW:(s + 1) * seg_len, KV_WIDTH:2 * KV_WIDTH]
    else:
      nk_ref[s, 0:WINDOW - seg_len, :] = ck_ref[s, seg_len:WINDOW, :]
      nv_ref[s, 0:WINDOW - seg_len, :] = cv_ref[s, seg_len:WINDOW, :]
      nk_ref[s, WINDOW - seg_len:WINDOW, :] = kvf_ref[s * seg_len:(s + 1) * seg_len, 0:KV_WIDTH]
      nv_ref[s, WINDOW - seg_len:WINDOW, :] = kvf_ref[s * seg_len:(s + 1) * seg_len, KV_WIDTH:2 * KV_WIDTH]

  row_id = lax.broadcasted_iota(jnp.int32, (GROUP * CHUNK, 1), 0)
  sink_cols = []
  for kh in range(N_KV_HEADS):
    col = jnp.full((GROUP * CHUNK, 1), sink_ref[kh * GROUP + GROUP - 1], F32)
    for g in range(GROUP - 2, -1, -1):
      col = jnp.where(row_id < (g + 1) * CHUNK, sink_ref[kh * GROUP + g], col)
    sink_cols.append(col)
  key_id = lax.broadcasted_iota(jnp.int32, (GROUP * CHUNK, BAND), 1)
  chunks_per_seg = seg_len // CHUNK
  band_step = CHUNK if n_seg == 1 else b_stride

  def attn_body(it, carry):
    q0 = pl.multiple_of(it * CHUNK, CHUNK)
    b0 = pl.multiple_of(it * band_step, CHUNK)
    if not has_state:
      first_valid = jnp.where(seq_start, jnp.maximum(WINDOW - it * CHUNK, 0), 0)
      valid = key_id >= first_valid
    for kh in range(N_KV_HEADS):
      qs = jnp.concatenate(
          [q_ref[kh * GROUP + g, pl.ds(q0, CHUNK), :] for g in range(GROUP)], axis=0)
      kband = kb_ref[kh, pl.ds(b0, BAND), :]
      vband = vb_ref[kh, pl.ds(b0, BAND), :]
      s = lax.dot_general(qs, kband, (((1,), (1,)), ((), ())), preferred_element_type=F32)
      s = s + bias_ref[kh]
      if not has_state:
        s = jnp.where(valid, s, NEG)
      sink = sink_cols[kh]
      m = jnp.maximum(jnp.max(s, axis=-1, keepdims=True), sink)
      p = jnp.exp(s - m)
      p = p / (jnp.sum(p, axis=-1, keepdims=True) + jnp.exp(sink - m))
      o = jnp.dot(p.astype(BF16), vband, preferred_element_type=F32)
      for g in range(GROUP):
        ao_ref[kh * GROUP + g, pl.ds(q0, CHUNK), :] = o[g * CHUNK:(g + 1) * CHUNK, :].astype(BF16)
    return carry

  assert n_seg == 1 or chunks_per_seg == 1
  lax.fori_loop(0, n_seg * chunks_per_seg, attn_body, 0)

  for h in range(N_HEADS):
    mixed_ref[:, CONV_WIDTH + h * HEAD_DIM:CONV_WIDTH + (h + 1) * HEAD_DIM] = ao_ref[h]


def _mixer_call(x, state, cache_k, cache_v, bkt, rel_table, sinks, g_pre, w_in, conv_w, *, has_state):
  n_seq, seq_len, _ = x.shape
  if has_state:
    n_seg, seg_len = MIX_ROWS // seq_len, seq_len
    grid = (n_seq // n_seg, 1)
    x = x.reshape(n_seq // n_seg, MIX_ROWS, D_MODEL)
    x_map = lambda b, s: (b, 0, 0)
    seq_map = lambda b, s: (b, 0, 0)
  else:
    n_seg, seg_len = 1, MIX_ROWS
    grid = (n_seq, seq_len // MIX_ROWS)
    x_map = lambda b, s: (b, s, 0)
    seq_map = lambda b, s: (b, 0, 0)
  const2 = lambda b, s: (0, 0)
  once = pl.Buffered(1)

  smem = pl.BlockSpec(memory_space=pltpu.SMEM)
  in_specs = [
      smem, smem,
      pl.BlockSpec((CHUNK, BAND), const2, pipeline_mode=once),
      pl.BlockSpec((None, MIX_ROWS, D_MODEL), x_map),
      pl.BlockSpec((1, D_MODEL), const2, pipeline_mode=once),
      pl.BlockSpec((D_MODEL, IN_COLS), const2, pipeline_mode=once),
      pl.BlockSpec((CONV_K, CONV_WIDTH), const2, pipeline_mode=once),
  ]
  args = [rel_table, sinks, bkt, x, g_pre, w_in, conv_w]
  if has_state:
    in_specs += [
        pl.BlockSpec((n_seg, CONV_K - 1, CONV_WIDTH), seq_map),
        pl.BlockSpec((n_seg, WINDOW, KV_WIDTH), seq_map),
        pl.BlockSpec((n_seg, WINDOW, KV_WIDTH), seq_map),
    ]
    args += [state, cache_k, cache_v]
  out_shape = (
      jax.ShapeDtypeStruct(x.shape, BF16),
      jax.ShapeDtypeStruct((n_seq, CONV_K - 1, CONV_WIDTH), F32),
      jax.ShapeDtypeStruct((n_seq, WINDOW, KV_WIDTH), F32),
      jax.ShapeDtypeStruct((n_seq, WINDOW, KV_WIDTH), F32),
  )
  out_specs = (
      pl.BlockSpec((None, MIX_ROWS, D_MODEL), x_map),
      pl.BlockSpec((n_seg, CONV_K - 1, CONV_WIDTH), seq_map),
      pl.BlockSpec((n_seg, WINDOW, KV_WIDTH), seq_map),
      pl.BlockSpec((n_seg, WINDOW, KV_WIDTH), seq_map),
  )
  band_rows = n_seg * (WINDOW + seg_len)
  scratch = [
      pltpu.VMEM((N_KV_HEADS, GROUP * CHUNK, BAND), F32),
      pltpu.VMEM((MIX_ROWS, D_MODEL), BF16),
      pltpu.VMEM((MIX_ROWS, 3 * CONV_COLS), F32),
      pltpu.VMEM((N_HEADS, MIX_ROWS, HEAD_DIM), BF16),
      pltpu.VMEM((MIX_ROWS, 2 * KV_WIDTH), F32),
      pltpu.VMEM((N_KV_HEADS, band_rows, HEAD_DIM), BF16),
      pltpu.VMEM((N_KV_HEADS, band_rows, HEAD_DIM), BF16),
      pltpu.VMEM((n_seg * (G_PAD + seg_len), CONV_WIDTH), F32),
      pltpu.VMEM((N_HEADS, MIX_ROWS, HEAD_DIM), BF16),
  ]
  kernel = functools.partial(_mixer_kernel, n_seg=n_seg, seg_len=seg_len, has_state=has_state)
  return pl.pallas_call(
      kernel,
      grid=grid,
      in_specs=in_specs,
      out_specs=out_specs,
      out_shape=out_shape,
      scratch_shapes=scratch,
      compiler_params=pltpu.CompilerParams(
          dimension_semantics=("arbitrary", "arbitrary"),
          vmem_limit_bytes=V7X_VMEM_LIMIT_BYTES),
      name="mixer_state" if has_state else "mixer_stream",
  )(*args)


def _ffn_kernel(mixed_ref, x_ref, wout_ref, gpm_ref, gpf_ref, gqf_ref, wg_ref, wu_ref, wd_ref,
                y_ref, hn_ref, acc_ref):
  j = pl.program_id(1)
  rows = mixed_ref.shape[0]

  @pl.when(j == 0)
  def _():
    acc_ref[...] = jnp.dot(mixed_ref[...], wout_ref[...], preferred_element_type=F32)
    gpm = gpm_ref[...]
    gpf = gpf_ref[...]

    def body(i, carry):
      r = pl.multiple_of(i * NORM_ROWS, NORM_ROWS)
      x1 = x_ref[pl.ds(r, NORM_ROWS), :] + _rms_scale(acc_ref[pl.ds(r, NORM_ROWS), :], gpm)
      y_ref[pl.ds(r, NORM_ROWS), :] = x1
      hn_ref[pl.ds(r, NORM_ROWS), :] = _rms_scale(x1, gpf).astype(BF16)
      acc_ref[pl.ds(r, NORM_ROWS), :] = jnp.zeros((NORM_ROWS, D_MODEL), F32)
      return carry

    lax.fori_loop(0, rows // NORM_ROWS, body, 0)

  hn = hn_ref[...]
  gate = jnp.dot(hn, wg_ref[...], preferred_element_type=F32)
  up = jnp.dot(hn, wu_ref[...], preferred_element_type=F32)
  mid = (gate * (1.0 / (1.0 + jnp.exp(-gate))) * up).astype(BF16)
  acc_ref[...] += jnp.dot(mid, wd_ref[...], preferred_element_type=F32)

  @pl.when(j == pl.num_programs(1) - 1)
  def _():
    gqf = gqf_ref[...]

    def body(i, carry):
      r = pl.multiple_of(i * NORM_ROWS, NORM_ROWS)
      y_ref[pl.ds(r, NORM_ROWS), :] = (
          y_ref[pl.ds(r, NORM_ROWS), :] + _rms_scale(acc_ref[pl.ds(r, NORM_ROWS), :], gqf))
      return carry

    lax.fori_loop(0, rows // NORM_ROWS, body, 0)


def _ffn_call(mixed, x, w_out, g_post_mix, g_pre_ffn, g_post_ffn, w_gate, w_up, w_down):
  rows = x.shape[0]
  row_map = lambda i, j: (i, 0)
  const2 = lambda i, j: (0, 0)
  once = pl.Buffered(1)
  return pl.pallas_call(
      _ffn_kernel,
      grid=(rows // FFN_ROWS, D_FF // FFN_TILE),
      in_specs=[
          pl.BlockSpec((FFN_ROWS, D_MODEL), row_map),
          pl.BlockSpec((FFN_ROWS, D_MODEL), row_map),
          pl.BlockSpec((D_MODEL, D_MODEL), const2, pipeline_mode=once),
          pl.BlockSpec((1, D_MODEL), const2, pipeline_mode=once),
          pl.BlockSpec((1, D_MODEL), const2, pipeline_mode=once),
          pl.BlockSpec((1, D_MODEL), const2, pipeline_mode=once),
          pl.BlockSpec((D_MODEL, FFN_TILE), lambda i, j: (0, j)),
          pl.BlockSpec((D_MODEL, FFN_TILE), lambda i, j: (0, j)),
          pl.BlockSpec((FFN_TILE, D_MODEL), lambda i, j: (j, 0)),
      ],
      out_specs=pl.BlockSpec((FFN_ROWS, D_MODEL), row_map),
      out_shape=jax.ShapeDtypeStruct((rows, D_MODEL), F32),
      scratch_shapes=[
          pltpu.VMEM((FFN_ROWS, D_MODEL), BF16),
          pltpu.VMEM((FFN_ROWS, D_MODEL), F32),
      ],
      compiler_params=pltpu.CompilerParams(
          dimension_semantics=("arbitrary", "arbitrary"),
          vmem_limit_bytes=V7X_VMEM_LIMIT_BYTES),
      name="ffn",
  )(mixed, x, w_out, g_post_mix, g_pre_ffn, g_post_ffn, w_gate, w_up, w_down)


def kernel(x_prompt, x_sample, state_conv, cache_k, cache_v, rel_table, g_pre_mix, w_in, conv_w,
           attn_sinks, w_out, g_post_mix, g_pre_ffn, w_gate, w_up, w_down, g_post_ffn):
  depth, n_dec, cache_len = cache_k.shape[:3]
  batch, seq_len, _ = x_prompt.shape
  assert depth == 1 and cache_len == WINDOW and x_sample.shape[1] == CHUNK
  assert seq_len % MIX_ROWS == 0 and MIX_ROWS % CHUNK == 0

  rel = (jnp.arange(BAND) - WINDOW)[None, :] - jnp.arange(CHUNK)[:, None]
  bkt = _t5_bucket(rel).astype(jnp.int32)

  row = lambda g: g[0].reshape(1, D_MODEL)
  win = w_in[0].astype(BF16)
  mix_args = (bkt, rel_table, attn_sinks[0], row(g_pre_mix), win, conv_w[0])
  mixed_p, conv_p, k_p, v_p = _mixer_call(x_prompt, None, None, None, *mix_args, has_state=False)
  mixed_s, conv_s, k_s, v_s = _mixer_call(
      x_sample, state_conv[0], cache_k[0].reshape(n_dec, WINDOW, KV_WIDTH),
      cache_v[0].reshape(n_dec, WINDOW, KV_WIDTH), *mix_args, has_state=True)

  ffn_args = (w_out[0].astype(BF16), row(g_post_mix), row(g_pre_ffn), row(g_post_ffn),
              w_gate[0].astype(BF16), w_up[0].astype(BF16), w_down[0].astype(BF16))
  y_p = _ffn_call(mixed_p.reshape(-1, D_MODEL), x_prompt.reshape(-1, D_MODEL), *ffn_args)
  y_s = _ffn_call(mixed_s.reshape(-1, D_MODEL), x_sample.reshape(-1, D_MODEL), *ffn_args)

  heads = lambda a: a.reshape(1, a.shape[0], WINDOW, N_KV_HEADS, HEAD_DIM)
  return (y_p.reshape(x_prompt.shape), y_s.reshape(x_sample.shape),
          conv_p[None], heads(k_p), heads(v_p), conv_s[None], heads(k_s), heads(v_s))
```

```python
import functools
import math

import jax
import jax.numpy as jnp
import numpy as np
from jax import lax
from jax.experimental import pallas as pl
from jax.experimental.pallas import tpu as pltpu

D_MODEL = 2048
CHUNK = 64
HEAD_DIM = 64
CONV_WIDTH = D_MODEL // 2
CONV_K = 3
N_HEADS = (D_MODEL - CONV_WIDTH) // HEAD_DIM
N_KV_HEADS = 4
GROUP = N_HEADS // N_KV_HEADS
ATTN_WIDTH = N_HEADS * HEAD_DIM
KV_WIDTH = N_KV_HEADS * HEAD_DIM
IN_COLS = 3 * CONV_WIDTH + ATTN_WIDTH + 2 * KV_WIDTH
WINDOW = 128
BAND = WINDOW + CHUNK
NUM_BUCKETS = 32
MAX_DISTANCE = 128
D_FF = -(-8 * D_MODEL // (3 * 256)) * 256
EPS = 1e-6
NEG = -1e30

OFF_CB, OFF_CC, OFF_CU = 0, CONV_WIDTH, 2 * CONV_WIDTH
OFF_Q = 3 * CONV_WIDTH
OFF_KV = OFF_Q + ATTN_WIDTH

SUBLANES = 8
V7X_VMEM_LIMIT_BYTES = 56 << 20
MIX_ROWS = 512
FFN_ROWS = 512
FFN_TILE = 512
CONV_COLS = 512
NORM_ROWS = 16
CONV_ROWS = 32
G_PAD = SUBLANES

F32 = jnp.float32
BF16 = jnp.bfloat16


def _t5_bucket(rel):
  half = NUM_BUCKETS // 2
  ret = jnp.where(rel > 0, half, 0)
  n = jnp.abs(rel)
  max_exact = half // 2
  nf = jnp.maximum(n, 1).astype(jnp.float32)
  large = max_exact + (jnp.log(nf / max_exact) / math.log(MAX_DISTANCE / max_exact)
                       * (half - max_exact)).astype(jnp.int32)
  large = jnp.minimum(large, half - 1)
  return ret + jnp.where(n < max_exact, n, large)


def _rms_scale(y, gain):
  return y * lax.rsqrt(jnp.mean(y * y, axis=-1, keepdims=True) + EPS) * gain


def _mixer_kernel(tbl_ref, sink_ref, bkt_ref, x_ref, gpre_ref, win_ref, convw_ref, *rest,
                  n_seg, seg_len, has_state):
  if has_state:
    (state_ref, ck_ref, cv_ref, mixed_ref, nconv_ref, nk_ref, nv_ref,
     bias_ref, h_ref, zc_ref, q_ref, kvf_ref, kb_ref, vb_ref, gs_ref, ao_ref) = rest
  else:
    (mixed_ref, nconv_ref, nk_ref, nv_ref,
     bias_ref, h_ref, zc_ref, q_ref, kvf_ref, kb_ref, vb_ref, gs_ref, ao_ref) = rest
  rows = n_seg * seg_len
  g_stride = G_PAD + seg_len
  b_stride = WINDOW + seg_len
  first_call = (pl.program_id(0) == 0) & (pl.program_id(1) == 0)
  seq_start = pl.program_id(1) == 0

  @pl.when(first_call)
  def _():
    bkt = bkt_ref[...]
    for h in range(N_HEADS):
      acc = jnp.zeros((CHUNK, BAND), F32)
      for j in range(NUM_BUCKETS):
        acc = jnp.where(bkt == j, tbl_ref[j, h], acc)
      g = h % GROUP
      bias_ref[h // GROUP, g * CHUNK:(g + 1) * CHUNK, :] = acc

  if has_state:
    for s in range(n_seg):
      gs_ref[s * g_stride + G_PAD - (CONV_K - 1):s * g_stride + G_PAD, :] = state_ref[s]
      for kh in range(N_KV_HEADS):
        kb_ref[kh, s * b_stride:s * b_stride + WINDOW, :] = (
            ck_ref[s, :, kh * HEAD_DIM:(kh + 1) * HEAD_DIM].astype(BF16))
        vb_ref[kh, s * b_stride:s * b_stride + WINDOW, :] = (
            cv_ref[s, :, kh * HEAD_DIM:(kh + 1) * HEAD_DIM].astype(BF16))
  else:
    @pl.when(seq_start)
    def _():
      gs_ref[0:G_PAD, :] = jnp.zeros((G_PAD, CONV_WIDTH), F32)
      kb_ref[:, 0:WINDOW, :] = jnp.zeros((N_KV_HEADS, WINDOW, HEAD_DIM), BF16)
      vb_ref[:, 0:WINDOW, :] = jnp.zeros((N_KV_HEADS, WINDOW, HEAD_DIM), BF16)

    @pl.when(jnp.logical_not(seq_start))
    def _():
      gs_ref[0:G_PAD, :] = gs_ref[seg_len:seg_len + G_PAD, :]
      kb_ref[:, 0:WINDOW, :] = kb_ref[:, seg_len:seg_len + WINDOW, :]
      vb_ref[:, 0:WINDOW, :] = vb_ref[:, seg_len:seg_len + WINDOW, :]

  gpre = gpre_ref[...]

  def norm_body(i, carry):
    r = pl.multiple_of(i * NORM_ROWS, NORM_ROWS)
    h_ref[pl.ds(r, NORM_ROWS), :] = _rms_scale(x_ref[pl.ds(r, NORM_ROWS), :], gpre).astype(BF16)
    return carry

  lax.fori_loop(0, rows // NORM_ROWS, norm_body, 0)

  for c0 in range(0, CONV_WIDTH, CONV_COLS):
    for j, off in enumerate((OFF_CB, OFF_CC, OFF_CU)):
      zc_ref[:, j * CONV_COLS:(j + 1) * CONV_COLS] = jnp.dot(
          h_ref[...], win_ref[:, off + c0:off + c0 + CONV_COLS], preferred_element_type=F32)
    for s in range(n_seg):
      for r in range(0, seg_len, CONV_ROWS):
        zr = s * seg_len + r
        gr = s * g_stride + G_PAD + r
        gs_ref[gr:gr + CONV_ROWS, c0:c0 + CONV_COLS] = (
            zc_ref[zr:zr + CONV_ROWS, CONV_COLS:2 * CONV_COLS]
            * zc_ref[zr:zr + CONV_ROWS, 2 * CONV_COLS:3 * CONV_COLS])
    for s in range(n_seg):
      for r in range(0, seg_len, CONV_ROWS):
        zr = s * seg_len + r
        gr = s * g_stride + G_PAD + r
        conv = convw_ref[0:1, c0:c0 + CONV_COLS] * gs_ref[gr - 2:gr - 2 + CONV_ROWS, c0:c0 + CONV_COLS]
        conv = conv + convw_ref[1:2, c0:c0 + CONV_COLS] * gs_ref[gr - 1:gr - 1 + CONV_ROWS, c0:c0 + CONV_COLS]
        conv = conv + convw_ref[2:3, c0:c0 + CONV_COLS] * gs_ref[gr:gr + CONV_ROWS, c0:c0 + CONV_COLS]
        mixed_ref[zr:zr + CONV_ROWS, c0:c0 + CONV_COLS] = (
            zc_ref[zr:zr + CONV_ROWS, 0:CONV_COLS] * conv).astype(BF16)

  for c0 in range(0, ATTN_WIDTH, CONV_COLS):
    zc_ref[:, 0:CONV_COLS] = jnp.dot(
        h_ref[...], win_ref[:, OFF_Q + c0:OFF_Q + c0 + CONV_COLS], preferred_element_type=F32)
    for hh in range(CONV_COLS // HEAD_DIM):
      q_ref[c0 // HEAD_DIM + hh] = (
          zc_ref[:, hh * HEAD_DIM:(hh + 1) * HEAD_DIM] * (HEAD_DIM ** -0.5)).astype(BF16)

  kvf_ref[...] = jnp.dot(h_ref[...], win_ref[:, OFF_KV:OFF_KV + 2 * KV_WIDTH], preferred_element_type=F32)
  for s in range(n_seg):
    for kh in range(N_KV_HEADS):
      kb_ref[kh, s * b_stride + WINDOW:(s + 1) * b_stride, :] = (
          kvf_ref[s * seg_len:(s + 1) * seg_len, kh * HEAD_DIM:(kh + 1) * HEAD_DIM].astype(BF16))
      vb_ref[kh, s * b_stride + WINDOW:(s + 1) * b_stride, :] = (
          kvf_ref[s * seg_len:(s + 1) * seg_len,
                  KV_WIDTH + kh * HEAD_DIM:KV_WIDTH + (kh + 1) * HEAD_DIM].astype(BF16))

  for s in range(n_seg):
    g_end = s * g_stride + G_PAD + seg_len
    nconv_ref[s] = gs_ref[g_end - (CONV_K - 1):g_end, :]
    if seg_len >= WINDOW:
      nk_ref[s] = kvf_ref[(s + 1) * seg_len - WINDOW:(s + 1) * seg_len, 0:KV_WIDTH]
      nv_ref[s] = kvf_ref[(s + 1) * seg_len - WINDOW:(s + 1) * seg_len, KV_WIDTH:2 * KV_WIDTH]
    else:
      nk_ref[s, 0:WINDOW - seg_len, :] = ck_ref[s, seg_len:WINDOW, :]
      nv_ref[s, 0:WINDOW - seg_len, :] = cv_ref[s, seg_len:WINDOW, :]
      nk_ref[s, WINDOW - seg_len:WINDOW, :] = kvf_ref[s * seg_len:(s + 1) * seg_len, 0:KV_WIDTH]
      nv_ref[s, WINDOW - seg_len:WINDOW, :] = kvf_ref[s * seg_len:(s + 1) * seg_len, KV_WIDTH:2 * KV_WIDTH]

  row_id = lax.broadcasted_iota(jnp.int32, (GROUP * CHUNK, 1), 0)
  sink_cols = []
  for kh in range(N_KV_HEADS):
    col = jnp.full((GROUP * CHUNK, 1), sink_ref[kh * GROUP + GROUP - 1], F32)
    for g in range(GROUP - 2, -1, -1):
      col = jnp.where(row_id < (g + 1) * CHUNK, sink_ref[kh * GROUP + g], col)
    sink_cols.append(col)
  key_id = lax.broadcasted_iota(jnp.int32, (GROUP * CHUNK, BAND), 1)
  chunks_per_seg = seg_len // CHUNK
  band_step = CHUNK if n_seg == 1 else b_stride

  def attn_body(it, carry):
    q0 = pl.multiple_of(it * CHUNK, CHUNK)
    b0 = pl.multiple_of(it * band_step, CHUNK)
    if not has_state:
      first_valid = jnp.where(seq_start, jnp.maximum(WINDOW - it * CHUNK, 0), 0)
      valid = key_id >= first_valid
    for kh in range(N_KV_HEADS):
      qs = jnp.concatenate(
          [q_ref[kh * GROUP + g, pl.ds(q0, CHUNK), :] for g in range(GROUP)], axis=0)
      kband = kb_ref[kh, pl.ds(b0, BAND), :]
      vband = vb_ref[kh, pl.ds(b0, BAND), :]
      s = lax.dot_general(qs, kband, (((1,), (1,)), ((), ())), preferred_element_type=F32)
      s = s + bias_ref[kh]
      if not has_state:
        s = jnp.where(valid, s, NEG)
      sink = sink_cols[kh]
      m = jnp.maximum(jnp.max(s, axis=-1, keepdims=True), sink)
      p = jnp.exp(s - m)
      p = p / (jnp.sum(p, axis=-1, keepdims=True) + jnp.exp(sink - m))
      o = jnp.dot(p.astype(BF16), vband, preferred_element_type=F32)
      for g in range(GROUP):
        ao_ref[kh * GROUP + g, pl.ds(q0, CHUNK), :] = o[g * CHUNK:(g + 1) * CHUNK, :].astype(BF16)
    return carry

  assert n_seg == 1 or chunks_per_seg == 1
  lax.fori_loop(0, n_seg * chunks_per_seg, attn_body, 0)

  for h in range(N_HEADS):
    mixed_ref[:, CONV_WIDTH + h * HEAD_DIM:CONV_WIDTH + (h + 1) * HEAD_DIM] = ao_ref[h]


def _mixer_call(x, state, cache_k, cache_v, bkt, rel_table, sinks, g_pre, w_in, conv_w, *, has_state):
  n_seq, seq_len, _ = x.shape
  if has_state:
    n_seg, seg_len = MIX_ROWS // seq_len, seq_len
    grid = (n_seq // n_seg, 1)
    x = x.reshape(n_seq // n_seg, MIX_ROWS, D_MODEL)
    x_map = lambda b, s: (b, 0, 0)
    seq_map = lambda b, s: (b, 0, 0)
  else:
    n_seg, seg_len = 1, MIX_ROWS
    grid = (n_seq, seq_len // MIX_ROWS)
    x_map = lambda b, s: (b, s, 0)
    seq_map = lambda b, s: (b, 0, 0)
  const2 = lambda b, s: (0, 0)
  once = pl.Buffered(1)

  smem = pl.BlockSpec(memory_space=pltpu.SMEM)
  in_specs = [
      smem, smem,
      pl.BlockSpec((CHUNK, BAND), const2, pipeline_mode=once),
      pl.BlockSpec((None, MIX_ROWS, D_MODEL), x_map),
      pl.BlockSpec((1, D_MODEL), const2, pipeline_mode=once),
      pl.BlockSpec((D_MODEL, IN_COLS), const2, pipeline_mode=once),
      pl.BlockSpec((CONV_K, CONV_WIDTH), const2, pipeline_mode=once),
  ]
  args = [rel_table, sinks, bkt, x, g_pre, w_in, conv_w]
  if has_state:
    in_specs += [
        pl.BlockSpec((n_seg, CONV_K - 1, CONV_WIDTH), seq_map),
        pl.BlockSpec((n_seg, WINDOW, KV_WIDTH), seq_map),
        pl.BlockSpec((n_seg, WINDOW, KV_WIDTH), seq_map),
    ]
    args += [state, cache_k, cache_v]
  out_shape = (
      jax.ShapeDtypeStruct(x.shape, BF16),
      jax.ShapeDtypeStruct((n_seq, CONV_K - 1, CONV_WIDTH), F32),
      jax.ShapeDtypeStruct((n_seq, WINDOW, KV_WIDTH), F32),
      jax.ShapeDtypeStruct((n_seq, WINDOW, KV_WIDTH), F32),
  )
  out_specs = (
      pl.BlockSpec((None, MIX_ROWS, D_MODEL), x_map),
      pl.BlockSpec((n_seg, CONV_K - 1, CONV_WIDTH), seq_map),
      pl.BlockSpec((n_seg, WINDOW, KV_WIDTH), seq_map),
      pl.BlockSpec((n_seg, WINDOW, KV_WIDTH), seq_map),
  )
  band_rows = n_seg * (WINDOW + seg_len)
  scratch = [
      pltpu.VMEM((N_KV_HEADS, GROUP * CHUNK, BAND), F32),
      pltpu.VMEM((MIX_ROWS, D_MODEL), BF16),
      pltpu.VMEM((MIX_ROWS, 3 * CONV_COLS), F32),
      pltpu.VMEM((N_HEADS, MIX_ROWS, HEAD_DIM), BF16),
      pltpu.VMEM((MIX_ROWS, 2 * KV_WIDTH), F32),
      pltpu.VMEM((N_KV_HEADS, band_rows, HEAD_DIM), BF16),
      pltpu.VMEM((N_KV_HEADS, band_rows, HEAD_DIM), BF16),
      pltpu.VMEM((n_seg * (G_PAD + seg_len), CONV_WIDTH), F32),
      pltpu.VMEM((N_HEADS, MIX_ROWS, HEAD_DIM), BF16),
  ]
  kernel = functools.partial(_mixer_kernel, n_seg=n_seg, seg_len=seg_len, has_state=has_state)
  return pl.pallas_call(
      kernel,
      grid=grid,
      in_specs=in_specs,
      out_specs=out_specs,
      out_shape=out_shape,
      scratch_shapes=scratch,
      compiler_params=pltpu.CompilerParams(
          dimension_semantics=("arbitrary", "arbitrary"),
          vmem_limit_bytes=V7X_VMEM_LIMIT_BYTES),
      name="mixer_state" if has_state else "mixer_stream",
  )(*args)


def _ffn_kernel(mixed_ref, x_ref, wout_ref, gpm_ref, gpf_ref, gqf_ref, wg_ref, wu_ref, wd_ref,
                y_ref, hn_ref, acc_ref):
  j = pl.program_id(1)
  rows = mixed_ref.shape[0]

  @pl.when(j == 0)
  def _():
    acc_ref[...] = jnp.dot(mixed_ref[...], wout_ref[...], preferred_element_type=F32)
    gpm = gpm_ref[...]
    gpf = gpf_ref[...]

    def body(i, carry):
      r = pl.multiple_of(i * NORM_ROWS, NORM_ROWS)
      x1 = x_ref[pl.ds(r, NORM_ROWS), :] + _rms_scale(acc_ref[pl.ds(r, NORM_ROWS), :], gpm)
      y_ref[pl.ds(r, NORM_ROWS), :] = x1
      hn_ref[pl.ds(r, NORM_ROWS), :] = _rms_scale(x1, gpf).astype(BF16)
      acc_ref[pl.ds(r, NORM_ROWS), :] = jnp.zeros((NORM_ROWS, D_MODEL), F32)
      return carry

    lax.fori_loop(0, rows // NORM_ROWS, body, 0)

  hn = hn_ref[...]
  gate = jnp.dot(hn, wg_ref[...], preferred_element_type=F32)
  up = jnp.dot(hn, wu_ref[...], preferred_element_type=F32)
  mid = (gate * (1.0 / (1.0 + jnp.exp(-gate))) * up).astype(BF16)
  acc_ref[...] += jnp.dot(mid, wd_ref[...], preferred_element_type=F32)

  @pl.when(j == pl.num_programs(1) - 1)
  def _():
    gqf = gqf_ref[...]

    def body(i, carry):
      r = pl.multiple_of(i * NORM_ROWS, NORM_ROWS)
      y_ref[pl.ds(r, NORM_ROWS), :] = (
          y_ref[pl.ds(r, NORM_ROWS), :] + _rms_scale(acc_ref[pl.ds(r, NORM_ROWS), :], gqf))
      return carry

    lax.fori_loop(0, rows // NORM_ROWS, body, 0)


def _ffn_call(mixed, x, w_out, g_post_mix, g_pre_ffn, g_post_ffn, w_gate, w_up, w_down):
  rows = x.shape[0]
  row_map = lambda i, j: (i, 0)
  const2 = lambda i, j: (0, 0)
  once = pl.Buffered(1)
  return pl.pallas_call(
      _ffn_kernel,
      grid=(rows // FFN_ROWS, D_FF // FFN_TILE),
      in_specs=[
          pl.BlockSpec((FFN_ROWS, D_MODEL), row_map),
          pl.BlockSpec((FFN_ROWS, D_MODEL), row_map),
          pl.BlockSpec((D_MODEL, D_MODEL), const2, pipeline_mode=once),
          pl.BlockSpec((1, D_MODEL), const2, pipeline_mode=once),
          pl.BlockSpec((1, D_MODEL), const2, pipeline_mode=once),
          pl.BlockSpec((1, D_MODEL), const2, pipeline_mode=once),
          pl.BlockSpec((None, D_MODEL, FFN_TILE), lambda i, j: (j, 0, 0)),
          pl.BlockSpec((None, D_MODEL, FFN_TILE), lambda i, j: (j, 0, 0)),
          pl.BlockSpec((FFN_TILE, D_MODEL), lambda i, j: (j, 0)),
      ],
      out_specs=pl.BlockSpec((FFN_ROWS, D_MODEL), row_map),
      out_shape=jax.ShapeDtypeStruct((rows, D_MODEL), F32),
      scratch_shapes=[
          pltpu.VMEM((FFN_ROWS, D_MODEL), BF16),
          pltpu.VMEM((FFN_ROWS, D_MODEL), F32),
      ],
      compiler_params=pltpu.CompilerParams(
          dimension_semantics=("arbitrary", "arbitrary"),
          vmem_limit_bytes=V7X_VMEM_LIMIT_BYTES),
      name="ffn",
  )(mixed, x, w_out, g_post_mix, g_pre_ffn, g_post_ffn, w_gate, w_up, w_down)


def kernel(x_prompt, x_sample, state_conv, cache_k, cache_v, rel_table, g_pre_mix, w_in, conv_w,
           attn_sinks, w_out, g_post_mix, g_pre_ffn, w_gate, w_up, w_down, g_post_ffn):
  depth, n_dec, cache_len = cache_k.shape[:3]
  batch, seq_len, _ = x_prompt.shape
  assert depth == 1 and cache_len == WINDOW and x_sample.shape[1] == CHUNK
  assert seq_len % MIX_ROWS == 0 and MIX_ROWS % CHUNK == 0

  rel = (jnp.arange(BAND) - WINDOW)[None, :] - jnp.arange(CHUNK)[:, None]
  bkt = _t5_bucket(rel).astype(jnp.int32)

  row = lambda g: g[0].reshape(1, D_MODEL)
  win = w_in[0].astype(BF16)
  mix_args = (bkt, rel_table, attn_sinks[0], row(g_pre_mix), win, conv_w[0])
  mixed_p, conv_p, k_p, v_p = _mixer_call(x_prompt, None, None, None, *mix_args, has_state=False)
  mixed_s, conv_s, k_s, v_s = _mixer_call(
      x_sample, state_conv[0], cache_k[0].reshape(n_dec, WINDOW, KV_WIDTH),
      cache_v[0].reshape(n_dec, WINDOW, KV_WIDTH), *mix_args, has_state=True)

  tiles = lambda w: w[0].astype(BF16).reshape(D_MODEL, D_FF // FFN_TILE, FFN_TILE).transpose(1, 0, 2)
  ffn_args = (w_out[0].astype(BF16), row(g_post_mix), row(g_pre_ffn), row(g_post_ffn),
              tiles(w_gate), tiles(w_up), w_down[0].astype(BF16))
  y_p = _ffn_call(mixed_p.reshape(-1, D_MODEL), x_prompt.reshape(-1, D_MODEL), *ffn_args)
  y_s = _ffn_call(mixed_s.reshape(-1, D_MODEL), x_sample.reshape(-1, D_MODEL), *ffn_args)

  heads = lambda a: a.reshape(1, a.shape[0], WINDOW, N_KV_HEADS, HEAD_DIM)
  return (y_p.reshape(x_prompt.shape), y_s.reshape(x_sample.shape),
          conv_p[None], heads(k_p), heads(v_p), conv_s[None], heads(k_s), heads(v_s))
```

```python
import functools
import math

import jax
import jax.numpy as jnp
import numpy as np
from jax import lax
from jax.experimental import pallas as pl
from jax.experimental.pallas import tpu as pltpu

D_MODEL = 2048
CHUNK = 64
HEAD_DIM = 64
CONV_WIDTH = D_MODEL // 2
CONV_K = 3
N_HEADS = (D_MODEL - CONV_WIDTH) // HEAD_DIM
N_KV_HEADS = 4
GROUP = N_HEADS // N_KV_HEADS
ATTN_WIDTH = N_HEADS * HEAD_DIM
KV_WIDTH = N_KV_HEADS * HEAD_DIM
IN_COLS = 3 * CONV_WIDTH + ATTN_WIDTH + 2 * KV_WIDTH
WINDOW = 128
BAND = WINDOW + CHUNK
NUM_BUCKETS = 32
MAX_DISTANCE = 128
D_FF = -(-8 * D_MODEL // (3 * 256)) * 256
EPS = 1e-6
NEG = -1e30

OFF_CB, OFF_CC, OFF_CU = 0, CONV_WIDTH, 2 * CONV_WIDTH
OFF_Q = 3 * CONV_WIDTH
OFF_KV = OFF_Q + ATTN_WIDTH

SUBLANES = 8
V7X_VMEM_LIMIT_BYTES = 56 << 20
MIX_ROWS = 512
FFN_ROWS = 512
FFN_TILE = 512
CONV_COLS = 512
NORM_ROWS = 16
NORM_UNROLL = 8
CONV_ROWS = 32
G_PAD = SUBLANES

F32 = jnp.float32
BF16 = jnp.bfloat16


def _t5_bucket(rel):
  half = NUM_BUCKETS // 2
  ret = jnp.where(rel > 0, half, 0)
  n = jnp.abs(rel)
  max_exact = half // 2
  nf = jnp.maximum(n, 1).astype(jnp.float32)
  large = max_exact + (jnp.log(nf / max_exact) / math.log(MAX_DISTANCE / max_exact)
                       * (half - max_exact)).astype(jnp.int32)
  large = jnp.minimum(large, half - 1)
  return ret + jnp.where(n < max_exact, n, large)


def _rms_scale(y, gain):
  return y * lax.rsqrt(jnp.mean(y * y, axis=-1, keepdims=True) + EPS) * gain


def _mixer_kernel(tbl_ref, sink_ref, bkt_ref, x_ref, gpre_ref, win_ref, convw_ref, *rest,
                  n_seg, seg_len, has_state):
  if has_state:
    (state_ref, ck_ref, cv_ref, mixed_ref, nconv_ref, nk_ref, nv_ref,
     bias_ref, h_ref, zc_ref, q_ref, kvf_ref, kb_ref, vb_ref, gs_ref, ao_ref) = rest
  else:
    (mixed_ref, nconv_ref, nk_ref, nv_ref,
     bias_ref, h_ref, zc_ref, q_ref, kvf_ref, kb_ref, vb_ref, gs_ref, ao_ref) = rest
  rows = n_seg * seg_len
  g_stride = G_PAD + seg_len
  b_stride = WINDOW + seg_len
  first_call = (pl.program_id(0) == 0) & (pl.program_id(1) == 0)
  seq_start = pl.program_id(1) == 0

  @pl.when(first_call)
  def _():
    bkt = bkt_ref[...]
    for h in range(N_HEADS):
      acc = jnp.zeros((CHUNK, BAND), F32)
      for j in range(NUM_BUCKETS):
        acc = jnp.where(bkt == j, tbl_ref[j, h], acc)
      g = h % GROUP
      bias_ref[h // GROUP, g * CHUNK:(g + 1) * CHUNK, :] = acc

  if has_state:
    for s in range(n_seg):
      gs_ref[s * g_stride + G_PAD - (CONV_K - 1):s * g_stride + G_PAD, :] = state_ref[s]
      for kh in range(N_KV_HEADS):
        kb_ref[kh, s * b_stride:s * b_stride + WINDOW, :] = (
            ck_ref[s, :, kh * HEAD_DIM:(kh + 1) * HEAD_DIM].astype(BF16))
        vb_ref[kh, s * b_stride:s * b_stride + WINDOW, :] = (
            cv_ref[s, :, kh * HEAD_DIM:(kh + 1) * HEAD_DIM].astype(BF16))
  else:
    @pl.when(seq_start)
    def _():
      gs_ref[0:G_PAD, :] = jnp.zeros((G_PAD, CONV_WIDTH), F32)
      kb_ref[:, 0:WINDOW, :] = jnp.zeros((N_KV_HEADS, WINDOW, HEAD_DIM), BF16)
      vb_ref[:, 0:WINDOW, :] = jnp.zeros((N_KV_HEADS, WINDOW, HEAD_DIM), BF16)

    @pl.when(jnp.logical_not(seq_start))
    def _():
      gs_ref[0:G_PAD, :] = gs_ref[seg_len:seg_len + G_PAD, :]
      kb_ref[:, 0:WINDOW, :] = kb_ref[:, seg_len:seg_len + WINDOW, :]
      vb_ref[:, 0:WINDOW, :] = vb_ref[:, seg_len:seg_len + WINDOW, :]

  gpre = gpre_ref[...]

  def norm_body(i, carry):
    r = pl.multiple_of(i * NORM_ROWS, NORM_ROWS)
    h_ref[pl.ds(r, NORM_ROWS), :] = _rms_scale(x_ref[pl.ds(r, NORM_ROWS), :], gpre).astype(BF16)
    return carry

  lax.fori_loop(0, rows // NORM_ROWS, norm_body, 0, unroll=NORM_UNROLL)

  for c0 in range(0, CONV_WIDTH, CONV_COLS):
    for j, off in enumerate((OFF_CB, OFF_CC, OFF_CU)):
      zc_ref[:, j * CONV_COLS:(j + 1) * CONV_COLS] = jnp.dot(
          h_ref[...], win_ref[:, off + c0:off + c0 + CONV_COLS], preferred_element_type=F32)
    for s in range(n_seg):
      for r in range(0, seg_len, CONV_ROWS):
        zr = s * seg_len + r
        gr = s * g_stride + G_PAD + r
        gs_ref[gr:gr + CONV_ROWS, c0:c0 + CONV_COLS] = (
            zc_ref[zr:zr + CONV_ROWS, CONV_COLS:2 * CONV_COLS]
            * zc_ref[zr:zr + CONV_ROWS, 2 * CONV_COLS:3 * CONV_COLS])
    for s in range(n_seg):
      for r in range(0, seg_len, CONV_ROWS):
        zr = s * seg_len + r
        gr = s * g_stride + G_PAD + r
        conv = convw_ref[0:1, c0:c0 + CONV_COLS] * gs_ref[gr - 2:gr - 2 + CONV_ROWS, c0:c0 + CONV_COLS]
        conv = conv + convw_ref[1:2, c0:c0 + CONV_COLS] * gs_ref[gr - 1:gr - 1 + CONV_ROWS, c0:c0 + CONV_COLS]
        conv = conv + convw_ref[2:3, c0:c0 + CONV_COLS] * gs_ref[gr:gr + CONV_ROWS, c0:c0 + CONV_COLS]
        mixed_ref[zr:zr + CONV_ROWS, c0:c0 + CONV_COLS] = (
            zc_ref[zr:zr + CONV_ROWS, 0:CONV_COLS] * conv).astype(BF16)

  for c0 in range(0, ATTN_WIDTH, CONV_COLS):
    zc_ref[:, 0:CONV_COLS] = jnp.dot(
        h_ref[...], win_ref[:, OFF_Q + c0:OFF_Q + c0 + CONV_COLS], preferred_element_type=F32)
    for hh in range(CONV_COLS // HEAD_DIM):
      q_ref[c0 // HEAD_DIM + hh] = (
          zc_ref[:, hh * HEAD_DIM:(hh + 1) * HEAD_DIM] * (HEAD_DIM ** -0.5)).astype(BF16)

  kvf_ref[...] = jnp.dot(h_ref[...], win_ref[:, OFF_KV:OFF_KV + 2 * KV_WIDTH], preferred_element_type=F32)
  for s in range(n_seg):
    for kh in range(N_KV_HEADS):
      kb_ref[kh, s * b_stride + WINDOW:(s + 1) * b_stride, :] = (
          kvf_ref[s * seg_len:(s + 1) * seg_len, kh * HEAD_DIM:(kh + 1) * HEAD_DIM].astype(BF16))
      vb_ref[kh, s * b_stride + WINDOW:(s + 1) * b_stride, :] = (
          kvf_ref[s * seg_len:(s + 1) * seg_len,
                  KV_WIDTH + kh * HEAD_DIM:KV_WIDTH + (kh + 1) * HEAD_DIM].astype(BF16))

  for s in range(n_seg):
    g_end = s * g_stride + G_PAD + seg_len
    nconv_ref[s] = gs_ref[g_end - (CONV_K - 1):g_end, :]
    if seg_len >= WINDOW:
      nk_ref[s] = kvf_ref[(s + 1) * seg_len - WINDOW:(s + 1) * seg_len, 0:KV_WIDTH]
      nv_ref[s] = kvf_ref[(s + 1) * seg_len - WINDOW:(s + 1) * seg_len, KV_WIDTH:2 * KV_WIDTH]
    else:
      nk_ref[s, 0:WINDOW - seg_len, :] = ck_ref[s, seg_len:WINDOW, :]
      nv_ref[s, 0:WINDOW - seg_len, :] = cv_ref[s, seg_len:WINDOW, :]
      nk_ref[s, WINDOW - seg_len:WINDOW, :] = kvf_ref[s * seg_len:(s + 1) * seg_len, 0:KV_WIDTH]
      nv_ref[s, WINDOW - seg_len:WINDOW, :] = kvf_ref[s * seg_len:(s + 1) * seg_len, KV_WIDTH:2 * KV_WIDTH]

  row_id = lax.broadcasted_iota(jnp.int32, (GROUP * CHUNK, 1), 0)
  sink_cols = []
  for kh in range(N_KV_HEADS):
    col = jnp.full((GROUP * CHUNK, 1), sink_ref[kh * GROUP + GROUP - 1], F32)
    for g in range(GROUP - 2, -1, -1):
      col = jnp.where(row_id < (g + 1) * CHUNK, sink_ref[kh * GROUP + g], col)
    sink_cols.append(col)
  key_id = lax.broadcasted_iota(jnp.int32, (GROUP * CHUNK, BAND), 1)
  chunks_per_seg = seg_len // CHUNK
  band_step = CHUNK if n_seg == 1 else b_stride

  def attn_body(it, carry):
    q0 = pl.multiple_of(it * CHUNK, CHUNK)
    b0 = pl.multiple_of(it * band_step, CHUNK)
    if not has_state:
      first_valid = jnp.where(seq_start, jnp.maximum(WINDOW - it * CHUNK, 0), 0)
      valid = key_id >= first_valid
    for kh in range(N_KV_HEADS):
      qs = jnp.concatenate(
          [q_ref[kh * GROUP + g, pl.ds(q0, CHUNK), :] for g in range(GROUP)], axis=0)
      kband = kb_ref[kh, pl.ds(b0, BAND), :]
      vband = vb_ref[kh, pl.ds(b0, BAND), :]
      s = lax.dot_general(qs, kband, (((1,), (1,)), ((), ())), preferred_element_type=F32)
      s = s + bias_ref[kh]
      if not has_state:
        s = jnp.where(valid, s, NEG)
      sink = sink_cols[kh]
      m = jnp.maximum(jnp.max(s, axis=-1, keepdims=True), sink)
      p = jnp.exp(s - m)
      p = p / (jnp.sum(p, axis=-1, keepdims=True) + jnp.exp(sink - m))
      o = jnp.dot(p.astype(BF16), vband, preferred_element_type=F32)
      for g in range(GROUP):
        ao_ref[kh * GROUP + g, pl.ds(q0, CHUNK), :] = o[g * CHUNK:(g + 1) * CHUNK, :].astype(BF16)
    return carry

  assert n_seg == 1 or chunks_per_seg == 1
  lax.fori_loop(0, n_seg * chunks_per_seg, attn_body, 0)

  for h in range(N_HEADS):
    mixed_ref[:, CONV_WIDTH + h * HEAD_DIM:CONV_WIDTH + (h + 1) * HEAD_DIM] = ao_ref[h]


def _mixer_call(x, state, cache_k, cache_v, bkt, rel_table, sinks, g_pre, w_in, conv_w, *, has_state):
  n_seq, seq_len, _ = x.shape
  if has_state:
    n_seg, seg_len = MIX_ROWS // seq_len, seq_len
    grid = (n_seq // n_seg, 1)
    x = x.reshape(n_seq // n_seg, MIX_ROWS, D_MODEL)
    x_map = lambda b, s: (b, 0, 0)
    seq_map = lambda b, s: (b, 0, 0)
  else:
    n_seg, seg_len = 1, MIX_ROWS
    grid = (n_seq, seq_len // MIX_ROWS)
    x_map = lambda b, s: (b, s, 0)
    seq_map = lambda b, s: (b, 0, 0)
  const2 = lambda b, s: (0, 0)
  once = pl.Buffered(1)

  smem = pl.BlockSpec(memory_space=pltpu.SMEM)
  in_specs = [
      smem, smem,
      pl.BlockSpec((CHUNK, BAND), const2, pipeline_mode=once),
      pl.BlockSpec((None, MIX_ROWS, D_MODEL), x_map),
      pl.BlockSpec((1, D_MODEL), const2, pipeline_mode=once),
      pl.BlockSpec((D_MODEL, IN_COLS), const2, pipeline_mode=once),
      pl.BlockSpec((CONV_K, CONV_WIDTH), const2, pipeline_mode=once),
  ]
  args = [rel_table, sinks, bkt, x, g_pre, w_in, conv_w]
  if has_state:
    in_specs += [
        pl.BlockSpec((n_seg, CONV_K - 1, CONV_WIDTH), seq_map),
        pl.BlockSpec((n_seg, WINDOW, KV_WIDTH), seq_map),
        pl.BlockSpec((n_seg, WINDOW, KV_WIDTH), seq_map),
    ]
    args += [state, cache_k, cache_v]
  out_shape = (
      jax.ShapeDtypeStruct(x.shape, BF16),
      jax.ShapeDtypeStruct((n_seq, CONV_K - 1, CONV_WIDTH), F32),
      jax.ShapeDtypeStruct((n_seq, WINDOW, KV_WIDTH), F32),
      jax.ShapeDtypeStruct((n_seq, WINDOW, KV_WIDTH), F32),
  )
  out_specs = (
      pl.BlockSpec((None, MIX_ROWS, D_MODEL), x_map),
      pl.BlockSpec((n_seg, CONV_K - 1, CONV_WIDTH), seq_map),
      pl.BlockSpec((n_seg, WINDOW, KV_WIDTH), seq_map),
      pl.BlockSpec((n_seg, WINDOW, KV_WIDTH), seq_map),
  )
  band_rows = n_seg * (WINDOW + seg_len)
  scratch = [
      pltpu.VMEM((N_KV_HEADS, GROUP * CHUNK, BAND), F32),
      pltpu.VMEM((MIX_ROWS, D_MODEL), BF16),
      pltpu.VMEM((MIX_ROWS, 3 * CONV_COLS), F32),
      pltpu.VMEM((N_HEADS, MIX_ROWS, HEAD_DIM), BF16),
      pltpu.VMEM((MIX_ROWS, 2 * KV_WIDTH), F32),
      pltpu.VMEM((N_KV_HEADS, band_rows, HEAD_DIM), BF16),
      pltpu.VMEM((N_KV_HEADS, band_rows, HEAD_DIM), BF16),
      pltpu.VMEM((n_seg * (G_PAD + seg_len), CONV_WIDTH), F32),
      pltpu.VMEM((N_HEADS, MIX_ROWS, HEAD_DIM), BF16),
  ]
  kernel = functools.partial(_mixer_kernel, n_seg=n_seg, seg_len=seg_len, has_state=has_state)
  return pl.pallas_call(
      kernel,
      grid=grid,
      in_specs=in_specs,
      out_specs=out_specs,
      out_shape=out_shape,
      scratch_shapes=scratch,
      compiler_params=pltpu.CompilerParams(
          dimension_semantics=("arbitrary", "arbitrary"),
          vmem_limit_bytes=V7X_VMEM_LIMIT_BYTES),
      name="mixer_state" if has_state else "mixer_stream",
  )(*args)


def _ffn_kernel(mixed_ref, x_ref, wout_ref, gpm_ref, gpf_ref, gqf_ref, wg_ref, wu_ref, wd_ref,
                y_ref, hn_ref, acc_ref):
  j = pl.program_id(1)
  rows = mixed_ref.shape[0]

  @pl.when(j == 0)
  def _():
    acc_ref[...] = jnp.dot(mixed_ref[...], wout_ref[...], preferred_element_type=F32)
    gpm = gpm_ref[...]
    gpf = gpf_ref[...]

    def body(i, carry):
      r = pl.multiple_of(i * NORM_ROWS, NORM_ROWS)
      x1 = x_ref[pl.ds(r, NORM_ROWS), :] + _rms_scale(acc_ref[pl.ds(r, NORM_ROWS), :], gpm)
      y_ref[pl.ds(r, NORM_ROWS), :] = x1
      hn_ref[pl.ds(r, NORM_ROWS), :] = _rms_scale(x1, gpf).astype(BF16)
      acc_ref[pl.ds(r, NORM_ROWS), :] = jnp.zeros((NORM_ROWS, D_MODEL), F32)
      return carry

    lax.fori_loop(0, rows // NORM_ROWS, body, 0, unroll=NORM_UNROLL)

  hn = hn_ref[...]
  gate = jnp.dot(hn, wg_ref[...], preferred_element_type=F32)
  up = jnp.dot(hn, wu_ref[...], preferred_element_type=F32)
  mid = (gate * (1.0 / (1.0 + jnp.exp(-gate))) * up).astype(BF16)
  acc_ref[...] += jnp.dot(mid, wd_ref[...], preferred_element_type=F32)

  @pl.when(j == pl.num_programs(1) - 1)
  def _():
    gqf = gqf_ref[...]

    def body(i, carry):
      r = pl.multiple_of(i * NORM_ROWS, NORM_ROWS)
      y_ref[pl.ds(r, NORM_ROWS), :] = (
          y_ref[pl.ds(r, NORM_ROWS), :] + _rms_scale(acc_ref[pl.ds(r, NORM_ROWS), :], gqf))
      return carry

    lax.fori_loop(0, rows // NORM_ROWS, body, 0, unroll=NORM_UNROLL)


def _ffn_call(mixed, x, w_out, g_post_mix, g_pre_ffn, g_post_ffn, w_gate, w_up, w_down):
  rows = x.shape[0]
  row_map = lambda i, j: (i, 0)
  const2 = lambda i, j: (0, 0)
  once = pl.Buffered(1)
  return pl.pallas_call(
      _ffn_kernel,
      grid=(rows // FFN_ROWS, D_FF // FFN_TILE),
      in_specs=[
          pl.BlockSpec((FFN_ROWS, D_MODEL), row_map),
          pl.BlockSpec((FFN_ROWS, D_MODEL), row_map),
          pl.BlockSpec((D_MODEL, D_MODEL), const2, pipeline_mode=once),
          pl.BlockSpec((1, D_MODEL), const2, pipeline_mode=once),
          pl.BlockSpec((1, D_MODEL), const2, pipeline_mode=once),
          pl.BlockSpec((1, D_MODEL), const2, pipeline_mode=once),
          pl.BlockSpec((D_MODEL, FFN_TILE), lambda i, j: (0, j)),
          pl.BlockSpec((D_MODEL, FFN_TILE), lambda i, j: (0, j)),
          pl.BlockSpec((FFN_TILE, D_MODEL), lambda i, j: (j, 0)),
      ],
      out_specs=pl.BlockSpec((FFN_ROWS, D_MODEL), row_map),
      out_shape=jax.ShapeDtypeStruct((rows, D_MODEL), F32),
      scratch_shapes=[
          pltpu.VMEM((FFN_ROWS, D_MODEL), BF16),
          pltpu.VMEM((FFN_ROWS, D_MODEL), F32),
      ],
      compiler_params=pltpu.CompilerParams(
          dimension_semantics=("arbitrary", "arbitrary"),
          vmem_limit_bytes=V7X_VMEM_LIMIT_BYTES),
      name="ffn",
  )(mixed, x, w_out, g_post_mix, g_pre_ffn, g_post_ffn, w_gate, w_up, w_down)


def kernel(x_prompt, x_sample, state_conv, cache_k, cache_v, rel_table, g_pre_mix, w_in, conv_w,
           attn_sinks, w_out, g_post_mix, g_pre_ffn, w_gate, w_up, w_down, g_post_ffn):
  depth, n_dec, cache_len = cache_k.shape[:3]
  batch, seq_len, _ = x_prompt.shape
  assert depth == 1 and cache_len == WINDOW and x_sample.shape[1] == CHUNK
  assert seq_len % MIX_ROWS == 0 and MIX_ROWS % CHUNK == 0

  rel = (jnp.arange(BAND) - WINDOW)[None, :] - jnp.arange(CHUNK)[:, None]
  bkt = _t5_bucket(rel).astype(jnp.int32)

  row = lambda g: g[0].reshape(1, D_MODEL)
  win = w_in[0].astype(BF16)
  mix_args = (bkt, rel_table, attn_sinks[0], row(g_pre_mix), win, conv_w[0])
  mixed_p, conv_p, k_p, v_p = _mixer_call(x_prompt, None, None, None, *mix_args, has_state=False)
  mixed_s, conv_s, k_s, v_s = _mixer_call(
      x_sample, state_conv[0], cache_k[0].reshape(n_dec, WINDOW, KV_WIDTH),
      cache_v[0].reshape(n_dec, WINDOW, KV_WIDTH), *mix_args, has_state=True)

  ffn_args = (w_out[0].astype(BF16), row(g_post_mix), row(g_pre_ffn), row(g_post_ffn),
              w_gate[0].astype(BF16), w_up[0].astype(BF16), w_down[0].astype(BF16))
  y_p = _ffn_call(mixed_p.reshape(-1, D_MODEL), x_prompt.reshape(-1, D_MODEL), *ffn_args)
  y_s = _ffn_call(mixed_s.reshape(-1, D_MODEL), x_sample.reshape(-1, D_MODEL), *ffn_args)

  heads = lambda a: a.reshape(1, a.shape[0], WINDOW, N_KV_HEADS, HEAD_DIM)
  return (y_p.reshape(x_prompt.shape), y_s.reshape(x_sample.shape),
          conv_p[None], heads(k_p), heads(v_p), conv_s[None], heads(k_s), heads(v_s))
```

```python
import functools
import math

import jax
import jax.numpy as jnp
import numpy as np
from jax import lax
from jax.experimental import pallas as pl
from jax.experimental.pallas import tpu as pltpu

D_MODEL = 2048
CHUNK = 64
HEAD_DIM = 64
CONV_WIDTH = D_MODEL // 2
CONV_K = 3
N_HEADS = (D_MODEL - CONV_WIDTH) // HEAD_DIM
N_KV_HEADS = 4
GROUP = N_HEADS // N_KV_HEADS
ATTN_WIDTH = N_HEADS * HEAD_DIM
KV_WIDTH = N_KV_HEADS * HEAD_DIM
IN_COLS = 3 * CONV_WIDTH + ATTN_WIDTH + 2 * KV_WIDTH
WINDOW = 128
BAND = WINDOW + CHUNK
NUM_BUCKETS = 32
MAX_DISTANCE = 128
D_FF = -(-8 * D_MODEL // (3 * 256)) * 256
EPS = 1e-6
NEG = -1e30

OFF_CB, OFF_CC, OFF_CU = 0, CONV_WIDTH, 2 * CONV_WIDTH
OFF_Q = 3 * CONV_WIDTH
OFF_KV = OFF_Q + ATTN_WIDTH

SUBLANES = 8
LANES = 128
V7X_VMEM_LIMIT_BYTES = 56 << 20
MIX_ROWS = 512
FFN_ROWS = 512
FFN_TILE = 512
CONV_COLS = 512
NORM_ROWS = 16
NORM_UNROLL = 8
CONV_ROWS = 32
G_PAD = SUBLANES
N_BIAS_VARIANTS = WINDOW // CHUNK + 1

F32 = jnp.float32
BF16 = jnp.bfloat16


def _t5_bucket(rel):
  half = NUM_BUCKETS // 2
  ret = jnp.where(rel > 0, half, 0)
  n = jnp.abs(rel)
  max_exact = half // 2
  nf = jnp.maximum(n, 1).astype(jnp.float32)
  large = max_exact + (jnp.log(nf / max_exact) / math.log(MAX_DISTANCE / max_exact)
                       * (half - max_exact)).astype(jnp.int32)
  large = jnp.minimum(large, half - 1)
  return ret + jnp.where(n < max_exact, n, large)


def _rms_scale(y, gain):
  return y * lax.rsqrt(jnp.mean(y * y, axis=-1, keepdims=True) + EPS) * gain


def _mixer_kernel(tbl_ref, sink_ref, bkt_ref, x_ref, gpre_ref, win_ref, convw_ref, *rest,
                  n_seg, seg_len, has_state):
  if has_state:
    (state_ref, ck_ref, cv_ref, mixed_ref, nconv_ref, nk_ref, nv_ref,
     bias_ref, h_ref, zc_ref, q_ref, kvf_ref, kb_ref, vb_ref, gs_ref) = rest
  else:
    (mixed_ref, nconv_ref, nk_ref, nv_ref,
     bias_ref, h_ref, zc_ref, q_ref, kvf_ref, kb_ref, vb_ref, gs_ref) = rest
  rows = n_seg * seg_len
  g_stride = G_PAD + seg_len
  b_stride = WINDOW + seg_len
  first_call = (pl.program_id(0) == 0) & (pl.program_id(1) == 0)
  seq_start = pl.program_id(1) == 0

  @pl.when(first_call)
  def _():
    bkt = bkt_ref[...]
    key64 = lax.broadcasted_iota(jnp.int32, (CHUNK, BAND), 1)
    for h in range(N_HEADS):
      acc = jnp.zeros((CHUNK, BAND), F32)
      for j in range(NUM_BUCKETS):
        acc = jnp.where(bkt == j, tbl_ref[j, h], acc)
      g = h % GROUP
      for var in range(N_BIAS_VARIANTS):
        masked_keys = WINDOW - var * CHUNK
        bias_ref[var, h // GROUP, g * CHUNK:(g + 1) * CHUNK, :] = jnp.where(key64 >= masked_keys, acc, NEG)

  if has_state:
    for s in range(n_seg):
      gs_ref[s * g_stride + G_PAD - (CONV_K - 1):s * g_stride + G_PAD, :] = state_ref[s]
      for kh in range(N_KV_HEADS):
        kb_ref[kh, s * b_stride:s * b_stride + WINDOW, :] = (
            ck_ref[s, :, kh * HEAD_DIM:(kh + 1) * HEAD_DIM].astype(BF16))
        vb_ref[kh, s * b_stride:s * b_stride + WINDOW, :] = (
            cv_ref[s, :, kh * HEAD_DIM:(kh + 1) * HEAD_DIM].astype(BF16))
  else:
    @pl.when(seq_start)
    def _():
      gs_ref[0:G_PAD, :] = jnp.zeros((G_PAD, CONV_WIDTH), F32)
      kb_ref[:, 0:WINDOW, :] = jnp.zeros((N_KV_HEADS, WINDOW, HEAD_DIM), BF16)
      vb_ref[:, 0:WINDOW, :] = jnp.zeros((N_KV_HEADS, WINDOW, HEAD_DIM), BF16)

    @pl.when(jnp.logical_not(seq_start))
    def _():
      gs_ref[0:G_PAD, :] = gs_ref[seg_len:seg_len + G_PAD, :]
      kb_ref[:, 0:WINDOW, :] = kb_ref[:, seg_len:seg_len + WINDOW, :]
      vb_ref[:, 0:WINDOW, :] = vb_ref[:, seg_len:seg_len + WINDOW, :]

  gpre = gpre_ref[...]

  def norm_body(i, carry):
    r = pl.multiple_of(i * NORM_ROWS, NORM_ROWS)
    h_ref[pl.ds(r, NORM_ROWS), :] = _rms_scale(x_ref[pl.ds(r, NORM_ROWS), :], gpre).astype(BF16)
    return carry

  lax.fori_loop(0, rows // NORM_ROWS, norm_body, 0, unroll=NORM_UNROLL)

  for c0 in range(0, ATTN_WIDTH, CONV_COLS):
    zc_ref[:, 0:CONV_COLS] = jnp.dot(
        h_ref[...], win_ref[:, OFF_Q + c0:OFF_Q + c0 + CONV_COLS], preferred_element_type=F32)
    for hh in range(CONV_COLS // HEAD_DIM):
      q_ref[c0 // HEAD_DIM + hh] = (
          zc_ref[:, hh * HEAD_DIM:(hh + 1) * HEAD_DIM] * (HEAD_DIM ** -0.5)).astype(BF16)

  kvf_ref[...] = jnp.dot(h_ref[...], win_ref[:, OFF_KV:OFF_KV + 2 * KV_WIDTH], preferred_element_type=F32)
  for s in range(n_seg):
    for kh in range(N_KV_HEADS):
      kb_ref[kh, s * b_stride + WINDOW:(s + 1) * b_stride, :] = (
          kvf_ref[s * seg_len:(s + 1) * seg_len, kh * HEAD_DIM:(kh + 1) * HEAD_DIM].astype(BF16))
      vb_ref[kh, s * b_stride + WINDOW:(s + 1) * b_stride, :] = (
          kvf_ref[s * seg_len:(s + 1) * seg_len,
                  KV_WIDTH + kh * HEAD_DIM:KV_WIDTH + (kh + 1) * HEAD_DIM].astype(BF16))

  for s in range(n_seg):
    if seg_len >= WINDOW:
      nk_ref[s] = kvf_ref[(s + 1) * seg_len - WINDOW:(s + 1) * seg_len, 0:KV_WIDTH]
      nv_ref[s] = kvf_ref[(s + 1) * seg_len - WINDOW:(s + 1) * seg_len, KV_WIDTH:2 * KV_WIDTH]
    else:
      nk_ref[s, 0:WINDOW - seg_len, :] = ck_ref[s, seg_len:WINDOW, :]
      nv_ref[s, 0:WINDOW - seg_len, :] = cv_ref[s, seg_len:WINDOW, :]
      nk_ref[s, WINDOW - seg_len:WINDOW, :] = kvf_ref[s * seg_len:(s + 1) * seg_len, 0:KV_WIDTH]
      nv_ref[s, WINDOW - seg_len:WINDOW, :] = kvf_ref[s * seg_len:(s + 1) * seg_len, KV_WIDTH:2 * KV_WIDTH]

  row_id = lax.broadcasted_iota(jnp.int32, (GROUP * CHUNK, 1), 0)
  sink_cols = []
  for kh in range(N_KV_HEADS):
    col = jnp.full((GROUP * CHUNK, 1), sink_ref[kh * GROUP + GROUP - 1], F32)
    for g in range(GROUP - 2, -1, -1):
      col = jnp.where(row_id < (g + 1) * CHUNK, sink_ref[kh * GROUP + g], col)
    sink_cols.append(col)
  n_items = rows // CHUNK
  assert n_seg == 1 or seg_len == CHUNK
  band_step = CHUNK if n_seg == 1 else b_stride
  piece = 3 * CONV_WIDTH // n_items
  assert piece * n_items == 3 * CONV_WIDTH and piece % LANES == 0

  def attn_body(it, carry):
    q0 = pl.multiple_of(it * CHUNK, CHUNK)
    b0 = pl.multiple_of(it * band_step, CHUNK)
    c0 = pl.multiple_of(it * piece, LANES)
    zc_ref[:, pl.ds(c0, piece)] = jnp.dot(
        h_ref[...], win_ref[:, pl.ds(c0, piece)], preferred_element_type=F32)
    if has_state:
      var = N_BIAS_VARIANTS - 1
    else:
      var = jnp.where(seq_start, jnp.minimum(it, N_BIAS_VARIANTS - 1), N_BIAS_VARIANTS - 1)
    for kh in range(N_KV_HEADS):
      qs = jnp.concatenate(
          [q_ref[kh * GROUP + g, pl.ds(q0, CHUNK), :] for g in range(GROUP)], axis=0)
      kband = kb_ref[kh, pl.ds(b0, BAND), :]
      vband = vb_ref[kh, pl.ds(b0, BAND), :]
      s = lax.dot_general(qs, kband, (((1,), (1,)), ((), ())), preferred_element_type=F32)
      s = s + bias_ref[var, kh]
      sink = sink_cols[kh]
      m = jnp.maximum(jnp.max(s, axis=-1, keepdims=True), sink)
      p = jnp.exp(s - m)
      denom = jnp.sum(p, axis=-1, keepdims=True) + jnp.exp(sink - m)
      o = jnp.dot(p.astype(BF16), vband, preferred_element_type=F32) * (1.0 / denom)
      for g in range(GROUP):
        q_ref[kh * GROUP + g, pl.ds(q0, CHUNK), :] = o[g * CHUNK:(g + 1) * CHUNK, :].astype(BF16)
    return carry

  lax.fori_loop(0, n_items, attn_body, 0)

  for c0 in range(0, CONV_WIDTH, CONV_COLS):
    for s in range(n_seg):
      for r in range(0, seg_len, CONV_ROWS):
        zr = s * seg_len + r
        gr = s * g_stride + G_PAD + r
        gs_ref[gr:gr + CONV_ROWS, c0:c0 + CONV_COLS] = (
            zc_ref[zr:zr + CONV_ROWS, OFF_CC + c0:OFF_CC + c0 + CONV_COLS]
            * zc_ref[zr:zr + CONV_ROWS, OFF_CU + c0:OFF_CU + c0 + CONV_COLS])
    for s in range(n_seg):
      for r in range(0, seg_len, CONV_ROWS):
        zr = s * seg_len + r
        gr = s * g_stride + G_PAD + r
        conv = convw_ref[0:1, c0:c0 + CONV_COLS] * gs_ref[gr - 2:gr - 2 + CONV_ROWS, c0:c0 + CONV_COLS]
        conv = conv + convw_ref[1:2, c0:c0 + CONV_COLS] * gs_ref[gr - 1:gr - 1 + CONV_ROWS, c0:c0 + CONV_COLS]
        conv = conv + convw_ref[2:3, c0:c0 + CONV_COLS] * gs_ref[gr:gr + CONV_ROWS, c0:c0 + CONV_COLS]
        mixed_ref[zr:zr + CONV_ROWS, c0:c0 + CONV_COLS] = (
            zc_ref[zr:zr + CONV_ROWS, OFF_CB + c0:OFF_CB + c0 + CONV_COLS] * conv).astype(BF16)
  for s in range(n_seg):
    g_end = s * g_stride + G_PAD + seg_len
    nconv_ref[s] = gs_ref[g_end - (CONV_K - 1):g_end, :]

  for h in range(N_HEADS):
    mixed_ref[:, CONV_WIDTH + h * HEAD_DIM:CONV_WIDTH + (h + 1) * HEAD_DIM] = q_ref[h]


def _mixer_call(x, state, cache_k, cache_v, bkt, rel_table, sinks, g_pre, w_in, conv_w, *, has_state):
  n_seq, seq_len, _ = x.shape
  if has_state:
    n_seg, seg_len = MIX_ROWS // seq_len, seq_len
    grid = (n_seq // n_seg, 1)
    x = x.reshape(n_seq // n_seg, MIX_ROWS, D_MODEL)
    x_map = lambda b, s: (b, 0, 0)
    seq_map = lambda b, s: (b, 0, 0)
  else:
    n_seg, seg_len = 1, MIX_ROWS
    grid = (n_seq, seq_len // MIX_ROWS)
    x_map = lambda b, s: (b, s, 0)
    seq_map = lambda b, s: (b, 0, 0)
  const2 = lambda b, s: (0, 0)
  once = pl.Buffered(1)

  smem = pl.BlockSpec(memory_space=pltpu.SMEM)
  in_specs = [
      smem, smem,
      pl.BlockSpec((CHUNK, BAND), const2, pipeline_mode=once),
      pl.BlockSpec((None, MIX_ROWS, D_MODEL), x_map, pipeline_mode=once if has_state else None),
      pl.BlockSpec((1, D_MODEL), const2, pipeline_mode=once),
      pl.BlockSpec((D_MODEL, IN_COLS), const2, pipeline_mode=once),
      pl.BlockSpec((CONV_K, CONV_WIDTH), const2, pipeline_mode=once),
  ]
  args = [rel_table, sinks, bkt, x, g_pre, w_in, conv_w]
  if has_state:
    in_specs += [
        pl.BlockSpec((n_seg, CONV_K - 1, CONV_WIDTH), seq_map),
        pl.BlockSpec((n_seg, WINDOW, KV_WIDTH), seq_map),
        pl.BlockSpec((n_seg, WINDOW, KV_WIDTH), seq_map),
    ]
    args += [state, cache_k, cache_v]
  out_shape = (
      jax.ShapeDtypeStruct(x.shape, BF16),
      jax.ShapeDtypeStruct((n_seq, CONV_K - 1, CONV_WIDTH), F32),
      jax.ShapeDtypeStruct((n_seq, WINDOW, KV_WIDTH), F32),
      jax.ShapeDtypeStruct((n_seq, WINDOW, KV_WIDTH), F32),
  )
  out_specs = (
      pl.BlockSpec((None, MIX_ROWS, D_MODEL), x_map),
      pl.BlockSpec((n_seg, CONV_K - 1, CONV_WIDTH), seq_map),
      pl.BlockSpec((n_seg, WINDOW, KV_WIDTH), seq_map),
      pl.BlockSpec((n_seg, WINDOW, KV_WIDTH), seq_map),
  )
  band_rows = n_seg * (WINDOW + seg_len)
  scratch = [
      pltpu.VMEM((N_BIAS_VARIANTS, N_KV_HEADS, GROUP * CHUNK, BAND), F32),
      pltpu.VMEM((MIX_ROWS, D_MODEL), BF16),
      pltpu.VMEM((MIX_ROWS, 3 * CONV_WIDTH), F32),
      pltpu.VMEM((N_HEADS, MIX_ROWS, HEAD_DIM), BF16),
      pltpu.VMEM((MIX_ROWS, 2 * KV_WIDTH), F32),
      pltpu.VMEM((N_KV_HEADS, band_rows, HEAD_DIM), BF16),
      pltpu.VMEM((N_KV_HEADS, band_rows, HEAD_DIM), BF16),
      pltpu.VMEM((n_seg * (G_PAD + seg_len), CONV_WIDTH), F32),
  ]
  kernel = functools.partial(_mixer_kernel, n_seg=n_seg, seg_len=seg_len, has_state=has_state)
  return pl.pallas_call(
      kernel,
      grid=grid,
      in_specs=in_specs,
      out_specs=out_specs,
      out_shape=out_shape,
      scratch_shapes=scratch,
      compiler_params=pltpu.CompilerParams(
          dimension_semantics=("arbitrary", "arbitrary"),
          vmem_limit_bytes=V7X_VMEM_LIMIT_BYTES),
      name="mixer_state" if has_state else "mixer_stream",
  )(*args)


def _ffn_kernel(mixed_ref, x_ref, wout_ref, gpm_ref, gpf_ref, gqf_ref, wg_ref, wu_ref, wd_ref,
                y_ref, hn_ref, acc_ref):
  j = pl.program_id(1)
  rows = mixed_ref.shape[0]

  @pl.when(j == 0)
  def _():
    acc_ref[...] = jnp.dot(mixed_ref[...], wout_ref[...], preferred_element_type=F32)
    gpm = gpm_ref[...]
    gpf = gpf_ref[...]

    def body(i, carry):
      r = pl.multiple_of(i * NORM_ROWS, NORM_ROWS)
      x1 = x_ref[pl.ds(r, NORM_ROWS), :] + _rms_scale(acc_ref[pl.ds(r, NORM_ROWS), :], gpm)
      y_ref[pl.ds(r, NORM_ROWS), :] = x1
      hn_ref[pl.ds(r, NORM_ROWS), :] = _rms_scale(x1, gpf).astype(BF16)
      acc_ref[pl.ds(r, NORM_ROWS), :] = jnp.zeros((NORM_ROWS, D_MODEL), F32)
      return carry

    lax.fori_loop(0, rows // NORM_ROWS, body, 0, unroll=NORM_UNROLL)

  hn = hn_ref[...]
  gate = jnp.dot(hn, wg_ref[...], preferred_element_type=F32)
  up = jnp.dot(hn, wu_ref[...], preferred_element_type=F32)
  mid = (gate * (1.0 / (1.0 + jnp.exp(-gate))) * up).astype(BF16)
  acc_ref[...] += jnp.dot(mid, wd_ref[...], preferred_element_type=F32)

  @pl.when(j == pl.num_programs(1) - 1)
  def _():
    gqf = gqf_ref[...]

    def body(i, carry):
      r = pl.multiple_of(i * NORM_ROWS, NORM_ROWS)
      y_ref[pl.ds(r, NORM_ROWS), :] = (
          y_ref[pl.ds(r, NORM_ROWS), :] + _rms_scale(acc_ref[pl.ds(r, NORM_ROWS), :], gqf))
      return carry

    lax.fori_loop(0, rows // NORM_ROWS, body, 0, unroll=NORM_UNROLL)


def _ffn_call(mixed, x, w_out, g_post_mix, g_pre_ffn, g_post_ffn, w_gate, w_up, w_down):
  rows = x.shape[0]
  row_map = lambda i, j: (i, 0)
  const2 = lambda i, j: (0, 0)
  once = pl.Buffered(1)
  return pl.pallas_call(
      _ffn_kernel,
      grid=(rows // FFN_ROWS, D_FF // FFN_TILE),
      in_specs=[
          pl.BlockSpec((FFN_ROWS, D_MODEL), row_map),
          pl.BlockSpec((FFN_ROWS, D_MODEL), row_map),
          pl.BlockSpec((D_MODEL, D_MODEL), const2, pipeline_mode=once),
          pl.BlockSpec((1, D_MODEL), const2, pipeline_mode=once),
          pl.BlockSpec((1, D_MODEL), const2, pipeline_mode=once),
          pl.BlockSpec((1, D_MODEL), const2, pipeline_mode=once),
          pl.BlockSpec((D_MODEL, FFN_TILE), lambda i, j: (0, j)),
          pl.BlockSpec((D_MODEL, FFN_TILE), lambda i, j: (0, j)),
          pl.BlockSpec((FFN_TILE, D_MODEL), lambda i, j: (j, 0)),
      ],
      out_specs=pl.BlockSpec((FFN_ROWS, D_MODEL), row_map),
      out_shape=jax.ShapeDtypeStruct((rows, D_MODEL), F32),
      scratch_shapes=[
          pltpu.VMEM((FFN_ROWS, D_MODEL), BF16),
          pltpu.VMEM((FFN_ROWS, D_MODEL), F32),
      ],
      compiler_params=pltpu.CompilerParams(
          dimension_semantics=("arbitrary", "arbitrary"),
          vmem_limit_bytes=V7X_VMEM_LIMIT_BYTES),
      name="ffn",
  )(mixed, x, w_out, g_post_mix, g_pre_ffn, g_post_ffn, w_gate, w_up, w_down)


def kernel(x_prompt, x_sample, state_conv, cache_k, cache_v, rel_table, g_pre_mix, w_in, conv_w,
           attn_sinks, w_out, g_post_mix, g_pre_ffn, w_gate, w_up, w_down, g_post_ffn):
  depth, n_dec, cache_len = cache_k.shape[:3]
  batch, seq_len, _ = x_prompt.shape
  assert depth == 1 and cache_len == WINDOW and x_sample.shape[1] == CHUNK
  assert seq_len % MIX_ROWS == 0 and MIX_ROWS % CHUNK == 0

  rel = (jnp.arange(BAND) - WINDOW)[None, :] - jnp.arange(CHUNK)[:, None]
  bkt = _t5_bucket(rel).astype(jnp.int32)

  row = lambda g: g[0].reshape(1, D_MODEL)
  win = w_in[0].astype(BF16)
  mix_args = (bkt, rel_table, attn_sinks[0], row(g_pre_mix), win, conv_w[0])
  mixed_p, conv_p, k_p, v_p = _mixer_call(x_prompt, None, None, None, *mix_args, has_state=False)
  mixed_s, conv_s, k_s, v_s = _mixer_call(
      x_sample, state_conv[0], cache_k[0].reshape(n_dec, WINDOW, KV_WIDTH),
      cache_v[0].reshape(n_dec, WINDOW, KV_WIDTH), *mix_args, has_state=True)

  ffn_args = (w_out[0].astype(BF16), row(g_post_mix), row(g_pre_ffn), row(g_post_ffn),
              w_gate[0].astype(BF16), w_up[0].astype(BF16), w_down[0].astype(BF16))
  y_p = _ffn_call(mixed_p.reshape(-1, D_MODEL), x_prompt.reshape(-1, D_MODEL), *ffn_args)
  y_s = _ffn_call(mixed_s.reshape(-1, D_MODEL), x_sample.reshape(-1, D_MODEL), *ffn_args)

  heads = lambda a: a.reshape(1, a.shape[0], WINDOW, N_KV_HEADS, HEAD_DIM)
  return (y_p.reshape(x_prompt.shape), y_s.reshape(x_sample.shape),
          conv_p[None], heads(k_p), heads(v_p), conv_s[None], heads(k_s), heads(v_s))
```

```python
import functools
import math

import jax
import jax.numpy as jnp
import numpy as np
from jax import lax
from jax.experimental import pallas as pl
from jax.experimental.pallas import tpu as pltpu

D_MODEL = 2048
CHUNK = 64
HEAD_DIM = 64
CONV_WIDTH = D_MODEL // 2
CONV_K = 3
N_HEADS = (D_MODEL - CONV_WIDTH) // HEAD_DIM
N_KV_HEADS = 4
GROUP = N_HEADS // N_KV_HEADS
ATTN_WIDTH = N_HEADS * HEAD_DIM
KV_WIDTH = N_KV_HEADS * HEAD_DIM
IN_COLS = 3 * CONV_WIDTH + ATTN_WIDTH + 2 * KV_WIDTH
WINDOW = 128
BAND = WINDOW + CHUNK
NUM_BUCKETS = 32
MAX_DISTANCE = 128
D_FF = -(-8 * D_MODEL // (3 * 256)) * 256
EPS = 1e-6
NEG = -1e30

OFF_CB, OFF_CC, OFF_CU = 0, CONV_WIDTH, 2 * CONV_WIDTH
OFF_Q = 3 * CONV_WIDTH
OFF_KV = OFF_Q + ATTN_WIDTH

SUBLANES = 8
LANES = 128
V7X_VMEM_LIMIT_BYTES = 56 << 20
MIX_ROWS = 512
FFN_ROWS = 512
FFN_TILE = 512
CONV_COLS = 512
MXU_COLS = 256
ITEMS_PER_STEP = 8
STEP_UNROLL = 1
NORM_ROWS = 16
NORM_UNROLL = 8
CONV_ROWS = 32
G_PAD = SUBLANES
N_BIAS_VARIANTS = WINDOW // CHUNK + 1

F32 = jnp.float32
BF16 = jnp.bfloat16


def _t5_bucket(rel):
  half = NUM_BUCKETS // 2
  ret = jnp.where(rel > 0, half, 0)
  n = jnp.abs(rel)
  max_exact = half // 2
  nf = jnp.maximum(n, 1).astype(jnp.float32)
  large = max_exact + (jnp.log(nf / max_exact) / math.log(MAX_DISTANCE / max_exact)
                       * (half - max_exact)).astype(jnp.int32)
  large = jnp.minimum(large, half - 1)
  return ret + jnp.where(n < max_exact, n, large)


def _rms_scale(y, gain):
  return y * lax.rsqrt(jnp.mean(y * y, axis=-1, keepdims=True) + EPS) * gain


def _mixer_kernel(tbl_ref, sink_ref, bkt_ref, x_ref, gpre_ref, win_ref, convw_ref, *rest,
                  n_seg, seg_len, has_state):
  if has_state:
    (state_ref, ck_ref, cv_ref, mixed_ref, nconv_ref, nk_ref, nv_ref,
     bias_ref, h_ref, zc_ref, q_ref, kvf_ref, kb_ref, vb_ref, gs_ref) = rest
  else:
    (mixed_ref, nconv_ref, nk_ref, nv_ref,
     bias_ref, h_ref, zc_ref, q_ref, kvf_ref, kb_ref, vb_ref, gs_ref) = rest
  rows = n_seg * seg_len
  g_stride = G_PAD + seg_len
  b_stride = WINDOW + seg_len
  first_call = (pl.program_id(0) == 0) & (pl.program_id(1) == 0)
  seq_start = pl.program_id(1) == 0

  @pl.when(first_call)
  def _():
    bkt = bkt_ref[...]
    key64 = lax.broadcasted_iota(jnp.int32, (CHUNK, BAND), 1)
    for h in range(N_HEADS):
      acc = jnp.zeros((CHUNK, BAND), F32)
      for j in range(NUM_BUCKETS):
        acc = jnp.where(bkt == j, tbl_ref[j, h], acc)
      g = h % GROUP
      for var in range(bias_ref.shape[0]):
        bias_ref[var, h // GROUP, g * CHUNK:(g + 1) * CHUNK, :] = jnp.where(key64 >= var * CHUNK, acc, NEG)

  if has_state:
    for s in range(n_seg):
      gs_ref[s * g_stride + G_PAD - (CONV_K - 1):s * g_stride + G_PAD, :] = state_ref[s]
      for kh in range(N_KV_HEADS):
        kb_ref[kh, s * b_stride:s * b_stride + WINDOW, :] = (
            ck_ref[s, :, kh * HEAD_DIM:(kh + 1) * HEAD_DIM].astype(BF16))
        vb_ref[kh, s * b_stride:s * b_stride + WINDOW, :] = (
            cv_ref[s, :, kh * HEAD_DIM:(kh + 1) * HEAD_DIM].astype(BF16))
  else:
    @pl.when(seq_start)
    def _():
      gs_ref[0:G_PAD, :] = jnp.zeros((G_PAD, CONV_WIDTH), F32)
      kb_ref[:, 0:WINDOW, :] = jnp.zeros((N_KV_HEADS, WINDOW, HEAD_DIM), BF16)
      vb_ref[:, 0:WINDOW, :] = jnp.zeros((N_KV_HEADS, WINDOW, HEAD_DIM), BF16)

    @pl.when(jnp.logical_not(seq_start))
    def _():
      gs_ref[0:G_PAD, :] = gs_ref[seg_len:seg_len + G_PAD, :]
      kb_ref[:, 0:WINDOW, :] = kb_ref[:, seg_len:seg_len + WINDOW, :]
      vb_ref[:, 0:WINDOW, :] = vb_ref[:, seg_len:seg_len + WINDOW, :]

  gpre = gpre_ref[...]

  def norm_body(i, carry):
    r = pl.multiple_of(i * NORM_ROWS, NORM_ROWS)
    h_ref[pl.ds(r, NORM_ROWS), :] = _rms_scale(x_ref[pl.ds(r, NORM_ROWS), :], gpre).astype(BF16)
    return carry

  lax.fori_loop(0, rows // NORM_ROWS, norm_body, 0, unroll=NORM_UNROLL)

  for c0 in range(0, ATTN_WIDTH, CONV_COLS):
    zc_ref[:, 0:CONV_COLS] = jnp.dot(
        h_ref[...], win_ref[:, OFF_Q + c0:OFF_Q + c0 + CONV_COLS], preferred_element_type=F32)
    for hh in range(CONV_COLS // HEAD_DIM):
      q_ref[c0 // HEAD_DIM + hh] = (
          zc_ref[:, hh * HEAD_DIM:(hh + 1) * HEAD_DIM] * (HEAD_DIM ** -0.5)).astype(BF16)

  kvf_ref[...] = jnp.dot(h_ref[...], win_ref[:, OFF_KV:OFF_KV + 2 * KV_WIDTH], preferred_element_type=F32)
  for s in range(n_seg):
    for kh in range(N_KV_HEADS):
      kb_ref[kh, s * b_stride + WINDOW:(s + 1) * b_stride, :] = (
          kvf_ref[s * seg_len:(s + 1) * seg_len, kh * HEAD_DIM:(kh + 1) * HEAD_DIM].astype(BF16))
      vb_ref[kh, s * b_stride + WINDOW:(s + 1) * b_stride, :] = (
          kvf_ref[s * seg_len:(s + 1) * seg_len,
                  KV_WIDTH + kh * HEAD_DIM:KV_WIDTH + (kh + 1) * HEAD_DIM].astype(BF16))

  for s in range(n_seg):
    if seg_len >= WINDOW:
      nk_ref[s] = kvf_ref[(s + 1) * seg_len - WINDOW:(s + 1) * seg_len, 0:KV_WIDTH]
      nv_ref[s] = kvf_ref[(s + 1) * seg_len - WINDOW:(s + 1) * seg_len, KV_WIDTH:2 * KV_WIDTH]
    else:
      nk_ref[s, 0:WINDOW - seg_len, :] = ck_ref[s, seg_len:WINDOW, :]
      nv_ref[s, 0:WINDOW - seg_len, :] = cv_ref[s, seg_len:WINDOW, :]
      nk_ref[s, WINDOW - seg_len:WINDOW, :] = kvf_ref[s * seg_len:(s + 1) * seg_len, 0:KV_WIDTH]
      nv_ref[s, WINDOW - seg_len:WINDOW, :] = kvf_ref[s * seg_len:(s + 1) * seg_len, KV_WIDTH:2 * KV_WIDTH]

  row_id = lax.broadcasted_iota(jnp.int32, (GROUP * CHUNK, 1), 0)
  sink_cols = []
  for kh in range(N_KV_HEADS):
    col = jnp.full((GROUP * CHUNK, 1), sink_ref[kh * GROUP + GROUP - 1], F32)
    for g in range(GROUP - 2, -1, -1):
      col = jnp.where(row_id < (g + 1) * CHUNK, sink_ref[kh * GROUP + g], col)
    sink_cols.append(col)
  n_items = rows // CHUNK
  assert n_seg == 1 or seg_len == CHUNK
  band_step = CHUNK if n_seg == 1 else b_stride

  def attend(item):
    q0 = pl.multiple_of(item * CHUNK, CHUNK)
    b0 = pl.multiple_of(item * band_step, CHUNK)
    if has_state:
      var = 0
    else:
      var = jnp.where(seq_start, jnp.maximum(WINDOW // CHUNK - item, 0), 0)
    for kh in range(N_KV_HEADS):
      qs = jnp.concatenate(
          [q_ref[kh * GROUP + g, pl.ds(q0, CHUNK), :] for g in range(GROUP)], axis=0)
      kband = kb_ref[kh, pl.ds(b0, BAND), :]
      vband = vb_ref[kh, pl.ds(b0, BAND), :]
      s = lax.dot_general(qs, kband, (((1,), (1,)), ((), ())), preferred_element_type=F32)
      s = s + bias_ref[var, kh]
      sink = sink_cols[kh]
      m = jnp.maximum(jnp.max(s, axis=-1, keepdims=True), sink)
      p = jnp.exp(s - m)
      denom = jnp.sum(p, axis=-1, keepdims=True) + jnp.exp(sink - m)
      o = jnp.dot(p.astype(BF16), vband, preferred_element_type=F32) * (1.0 / denom)
      o = jnp.concatenate([o[g * CHUNK:(g + 1) * CHUNK, :] for g in range(GROUP)], axis=1)
      mixed_ref[pl.ds(q0, CHUNK), CONV_WIDTH + kh * GROUP * HEAD_DIM:
                CONV_WIDTH + (kh + 1) * GROUP * HEAD_DIM] = o.astype(BF16)

  n_steps = n_items // ITEMS_PER_STEP
  piece = 3 * CONV_WIDTH // n_steps
  assert n_steps * ITEMS_PER_STEP == n_items and piece * n_steps == 3 * CONV_WIDTH
  assert piece % CONV_COLS == 0 and piece // CONV_COLS <= ITEMS_PER_STEP

  def step_body(j, carry):
    for u in range(ITEMS_PER_STEP):
      if u < piece // CONV_COLS:
        c0 = pl.multiple_of(j * piece + u * CONV_COLS, MXU_COLS)
        zc_ref[:, pl.ds(c0, CONV_COLS)] = jnp.dot(
            h_ref[...], win_ref[:, pl.ds(c0, CONV_COLS)], preferred_element_type=F32)
      attend(j * ITEMS_PER_STEP + u)
    return carry

  lax.fori_loop(0, n_steps, step_body, 0, unroll=STEP_UNROLL)

  for c0 in range(0, CONV_WIDTH, CONV_COLS):
    cols = slice(c0, c0 + CONV_COLS)
    cb_cols, cc_cols, cu_cols = (slice(off + c0, off + c0 + CONV_COLS) for off in (OFF_CB, OFF_CC, OFF_CU))
    for s in range(n_seg):
      for r in range(0, seg_len, CONV_ROWS):
        zr = s * seg_len + r
        gr = s * g_stride + G_PAD + r
        gs_ref[gr:gr + CONV_ROWS, cols] = (
            zc_ref[zr:zr + CONV_ROWS, cc_cols] * zc_ref[zr:zr + CONV_ROWS, cu_cols])
    for s in range(n_seg):
      for r in range(0, seg_len, CONV_ROWS):
        zr = s * seg_len + r
        gr = s * g_stride + G_PAD + r
        conv = convw_ref[0:1, cols] * gs_ref[gr - 2:gr - 2 + CONV_ROWS, cols]
        conv = conv + convw_ref[1:2, cols] * gs_ref[gr - 1:gr - 1 + CONV_ROWS, cols]
        conv = conv + convw_ref[2:3, cols] * gs_ref[gr:gr + CONV_ROWS, cols]
        mixed_ref[zr:zr + CONV_ROWS, cols] = (zc_ref[zr:zr + CONV_ROWS, cb_cols] * conv).astype(BF16)

  for s in range(n_seg):
    g_end = s * g_stride + G_PAD + seg_len
    nconv_ref[s] = gs_ref[g_end - (CONV_K - 1):g_end, :]


def _mixer_call(x, state, cache_k, cache_v, bkt, rel_table, sinks, g_pre, w_in, conv_w, *, has_state):
  n_seq, seq_len, _ = x.shape
  if has_state:
    n_seg, seg_len = MIX_ROWS // seq_len, seq_len
    grid = (n_seq // n_seg, 1)
    x = x.reshape(n_seq // n_seg, MIX_ROWS, D_MODEL)
    x_map = lambda b, s: (b, 0, 0)
    seq_map = lambda b, s: (b, 0, 0)
  else:
    n_seg, seg_len = 1, MIX_ROWS
    grid = (n_seq, seq_len // MIX_ROWS)
    x_map = lambda b, s: (b, s, 0)
    seq_map = lambda b, s: (b, 0, 0)
  const2 = lambda b, s: (0, 0)
  once = pl.Buffered(1)

  smem = pl.BlockSpec(memory_space=pltpu.SMEM)
  in_specs = [
      smem, smem,
      pl.BlockSpec((CHUNK, BAND), const2, pipeline_mode=once),
      pl.BlockSpec((None, MIX_ROWS, D_MODEL), x_map, pipeline_mode=once if has_state else None),
      pl.BlockSpec((1, D_MODEL), const2, pipeline_mode=once),
      pl.BlockSpec((D_MODEL, IN_COLS), const2, pipeline_mode=once),
      pl.BlockSpec((CONV_K, CONV_WIDTH), const2, pipeline_mode=once),
  ]
  args = [rel_table, sinks, bkt, x, g_pre, w_in, conv_w]
  if has_state:
    in_specs += [
        pl.BlockSpec((n_seg, CONV_K - 1, CONV_WIDTH), seq_map),
        pl.BlockSpec((n_seg, WINDOW, KV_WIDTH), seq_map, pipeline_mode=once),
        pl.BlockSpec((n_seg, WINDOW, KV_WIDTH), seq_map, pipeline_mode=once),
    ]
    args += [state, cache_k, cache_v]
  out_shape = (
      jax.ShapeDtypeStruct(x.shape, BF16),
      jax.ShapeDtypeStruct((n_seq, CONV_K - 1, CONV_WIDTH), F32),
      jax.ShapeDtypeStruct((n_seq, WINDOW, KV_WIDTH), F32),
      jax.ShapeDtypeStruct((n_seq, WINDOW, KV_WIDTH), F32),
  )
  out_specs = (
      pl.BlockSpec((None, MIX_ROWS, D_MODEL), x_map),
      pl.BlockSpec((n_seg, CONV_K - 1, CONV_WIDTH), seq_map),
      pl.BlockSpec((n_seg, WINDOW, KV_WIDTH), seq_map),
      pl.BlockSpec((n_seg, WINDOW, KV_WIDTH), seq_map),
  )
  band_rows = n_seg * (WINDOW + seg_len)
  scratch = [
      pltpu.VMEM((1 if has_state else N_BIAS_VARIANTS, N_KV_HEADS, GROUP * CHUNK, BAND), F32),
      pltpu.VMEM((MIX_ROWS, D_MODEL), BF16),
      pltpu.VMEM((MIX_ROWS, 3 * CONV_WIDTH), F32),
      pltpu.VMEM((N_HEADS, MIX_ROWS, HEAD_DIM), BF16),
      pltpu.VMEM((MIX_ROWS, 2 * KV_WIDTH), F32),
      pltpu.VMEM((N_KV_HEADS, band_rows, HEAD_DIM), BF16),
      pltpu.VMEM((N_KV_HEADS, band_rows, HEAD_DIM), BF16),
      pltpu.VMEM((n_seg * (G_PAD + seg_len), CONV_WIDTH), F32),
  ]
  kernel = functools.partial(_mixer_kernel, n_seg=n_seg, seg_len=seg_len, has_state=has_state)
  return pl.pallas_call(
      kernel,
      grid=grid,
      in_specs=in_specs,
      out_specs=out_specs,
      out_shape=out_shape,
      scratch_shapes=scratch,
      compiler_params=pltpu.CompilerParams(
          dimension_semantics=("arbitrary", "arbitrary"),
          vmem_limit_bytes=V7X_VMEM_LIMIT_BYTES),
      name="mixer_state" if has_state else "mixer_stream",
  )(*args)


def _ffn_kernel(mixed_ref, x_ref, wout_ref, gpm_ref, gpf_ref, gqf_ref, wg_ref, wu_ref, wd_ref,
                y_ref, hn_ref, acc_ref):
  j = pl.program_id(1)
  rows = mixed_ref.shape[0]

  @pl.when(j == 0)
  def _():
    acc_ref[...] = jnp.dot(mixed_ref[...], wout_ref[...], preferred_element_type=F32)
    gpm = gpm_ref[...]
    gpf = gpf_ref[...]

    def body(i, carry):
      r = pl.multiple_of(i * NORM_ROWS, NORM_ROWS)
      x1 = x_ref[pl.ds(r, NORM_ROWS), :] + _rms_scale(acc_ref[pl.ds(r, NORM_ROWS), :], gpm)
      y_ref[pl.ds(r, NORM_ROWS), :] = x1
      hn_ref[pl.ds(r, NORM_ROWS), :] = _rms_scale(x1, gpf).astype(BF16)
      acc_ref[pl.ds(r, NORM_ROWS), :] = jnp.zeros((NORM_ROWS, D_MODEL), F32)
      return carry

    lax.fori_loop(0, rows // NORM_ROWS, body, 0, unroll=NORM_UNROLL)

  hn = hn_ref[...]
  gate = jnp.dot(hn, wg_ref[...], preferred_element_type=F32)
  up = jnp.dot(hn, wu_ref[...], preferred_element_type=F32)
  mid = (gate * (1.0 / (1.0 + jnp.exp(-gate))) * up).astype(BF16)
  acc_ref[...] += jnp.dot(mid, wd_ref[...], preferred_element_type=F32)

  @pl.when(j == pl.num_programs(1) - 1)
  def _():
    gqf = gqf_ref[...]

    def body(i, carry):
      r = pl.multiple_of(i * NORM_ROWS, NORM_ROWS)
      y_ref[pl.ds(r, NORM_ROWS), :] = (
          y_ref[pl.ds(r, NORM_ROWS), :] + _rms_scale(acc_ref[pl.ds(r, NORM_ROWS), :], gqf))
      return carry

    lax.fori_loop(0, rows // NORM_ROWS, body, 0, unroll=NORM_UNROLL)


def _ffn_call(mixed, x, w_out, g_post_mix, g_pre_ffn, g_post_ffn, w_gate, w_up, w_down):
  rows = x.shape[0]
  row_map = lambda i, j: (i, 0)
  const2 = lambda i, j: (0, 0)
  once = pl.Buffered(1)
  return pl.pallas_call(
      _ffn_kernel,
      grid=(rows // FFN_ROWS, D_FF // FFN_TILE),
      in_specs=[
          pl.BlockSpec((FFN_ROWS, D_MODEL), row_map),
          pl.BlockSpec((FFN_ROWS, D_MODEL), row_map),
          pl.BlockSpec((D_MODEL, D_MODEL), const2, pipeline_mode=once),
          pl.BlockSpec((1, D_MODEL), const2, pipeline_mode=once),
          pl.BlockSpec((1, D_MODEL), const2, pipeline_mode=once),
          pl.BlockSpec((1, D_MODEL), const2, pipeline_mode=once),
          pl.BlockSpec((D_MODEL, FFN_TILE), lambda i, j: (0, j)),
          pl.BlockSpec((D_MODEL, FFN_TILE), lambda i, j: (0, j)),
          pl.BlockSpec((FFN_TILE, D_MODEL), lambda i, j: (j, 0)),
      ],
      out_specs=pl.BlockSpec((FFN_ROWS, D_MODEL), row_map),
      out_shape=jax.ShapeDtypeStruct((rows, D_MODEL), F32),
      scratch_shapes=[
          pltpu.VMEM((FFN_ROWS, D_MODEL), BF16),
          pltpu.VMEM((FFN_ROWS, D_MODEL), F32),
      ],
      compiler_params=pltpu.CompilerParams(
          dimension_semantics=("arbitrary", "arbitrary"),
          vmem_limit_bytes=V7X_VMEM_LIMIT_BYTES),
      name="ffn",
  )(mixed, x, w_out, g_post_mix, g_pre_ffn, g_post_ffn, w_gate, w_up, w_down)


def kernel(x_prompt, x_sample, state_conv, cache_k, cache_v, rel_table, g_pre_mix, w_in, conv_w,
           attn_sinks, w_out, g_post_mix, g_pre_ffn, w_gate, w_up, w_down, g_post_ffn):
  depth, n_dec, cache_len = cache_k.shape[:3]
  batch, seq_len, _ = x_prompt.shape
  assert depth == 1 and cache_len == WINDOW and x_sample.shape[1] == CHUNK
  assert seq_len % MIX_ROWS == 0 and MIX_ROWS % CHUNK == 0

  rel = (jnp.arange(BAND) - WINDOW)[None, :] - jnp.arange(CHUNK)[:, None]
  bkt = _t5_bucket(rel).astype(jnp.int32)

  row = lambda g: g[0].reshape(1, D_MODEL)
  win = w_in[0].astype(BF16)
  mix_args = (bkt, rel_table, attn_sinks[0], row(g_pre_mix), win, conv_w[0])
  mixed_p, conv_p, k_p, v_p = _mixer_call(x_prompt, None, None, None, *mix_args, has_state=False)
  mixed_s, conv_s, k_s, v_s = _mixer_call(
      x_sample, state_conv[0], cache_k[0].reshape(n_dec, WINDOW, KV_WIDTH),
      cache_v[0].reshape(n_dec, WINDOW, KV_WIDTH), *mix_args, has_state=True)

  ffn_args = (w_out[0].astype(BF16), row(g_post_mix), row(g_pre_ffn), row(g_post_ffn),
              w_gate[0].astype(BF16), w_up[0].astype(BF16), w_down[0].astype(BF16))
  y_p = _ffn_call(mixed_p.reshape(-1, D_MODEL), x_prompt.reshape(-1, D_MODEL), *ffn_args)
  y_s = _ffn_call(mixed_s.reshape(-1, D_MODEL), x_sample.reshape(-1, D_MODEL), *ffn_args)

  heads = lambda a: a.reshape(1, a.shape[0], WINDOW, N_KV_HEADS, HEAD_DIM)
  return (y_p.reshape(x_prompt.shape), y_s.reshape(x_sample.shape),
          conv_p[None], heads(k_p), heads(v_p), conv_s[None], heads(k_s), heads(v_s))
```

```python
import functools
import math

import jax
import jax.numpy as jnp
import numpy as np
from jax import lax
from jax.experimental import pallas as pl
from jax.experimental.pallas import tpu as pltpu

D_MODEL = 2048
CHUNK = 64
HEAD_DIM = 64
CONV_WIDTH = D_MODEL // 2
CONV_K = 3
N_HEADS = (D_MODEL - CONV_WIDTH) // HEAD_DIM
N_KV_HEADS = 4
GROUP = N_HEADS // N_KV_HEADS
ATTN_WIDTH = N_HEADS * HEAD_DIM
KV_WIDTH = N_KV_HEADS * HEAD_DIM
IN_COLS = 3 * CONV_WIDTH + ATTN_WIDTH + 2 * KV_WIDTH
WINDOW = 128
BAND = WINDOW + CHUNK
NUM_BUCKETS = 32
MAX_DISTANCE = 128
D_FF = -(-8 * D_MODEL // (3 * 256)) * 256
EPS = 1e-6
NEG = -1e30

OFF_CB, OFF_CC, OFF_CU = 0, CONV_WIDTH, 2 * CONV_WIDTH
OFF_Q = 3 * CONV_WIDTH
OFF_KV = OFF_Q + ATTN_WIDTH

SUBLANES = 8
LANES = 128
V7X_VMEM_LIMIT_BYTES = 56 << 20
MIX_ROWS = 512
OUT_ROWS = 512
OUT_PARTS = 2
FFN_ROWS = 512
FFN_TILE = 1024
CONV_COLS = 512
MXU_COLS = 256
ITEMS_PER_STEP = 8
STEP_UNROLL = 1
NORM_ROWS = 16
NORM_UNROLL = 8
CONV_ROWS = 32
G_PAD = SUBLANES
N_BIAS_VARIANTS = WINDOW // CHUNK + 1

F32 = jnp.float32
BF16 = jnp.bfloat16


def _t5_bucket(rel):
  half = NUM_BUCKETS // 2
  ret = jnp.where(rel > 0, half, 0)
  n = jnp.abs(rel)
  max_exact = half // 2
  nf = jnp.maximum(n, 1).astype(jnp.float32)
  large = max_exact + (jnp.log(nf / max_exact) / math.log(MAX_DISTANCE / max_exact)
                       * (half - max_exact)).astype(jnp.int32)
  large = jnp.minimum(large, half - 1)
  return ret + jnp.where(n < max_exact, n, large)


def _rms_scale(y, gain):
  return y * lax.rsqrt(jnp.mean(y * y, axis=-1, keepdims=True) + EPS) * gain


def _mixer_kernel(tbl_ref, sink_ref, bkt_ref, x_ref, gpre_ref, win_ref, convw_ref, *rest,
                  n_seg, seg_len, has_state):
  if has_state:
    (state_ref, ck_ref, cv_ref, mixed_ref, nconv_ref, nk_ref, nv_ref,
     bias_ref, h_ref, zc_ref, q_ref, kvf_ref, kb_ref, vb_ref, gs_ref) = rest
  else:
    (mixed_ref, nconv_ref, nk_ref, nv_ref,
     bias_ref, h_ref, zc_ref, q_ref, kvf_ref, kb_ref, vb_ref, gs_ref) = rest
  rows = n_seg * seg_len
  g_stride = G_PAD + seg_len
  b_stride = WINDOW + seg_len
  first_call = (pl.program_id(0) == 0) & (pl.program_id(1) == 0)
  seq_start = pl.program_id(1) == 0

  @pl.when(first_call)
  def _():
    bkt = bkt_ref[...]
    key64 = lax.broadcasted_iota(jnp.int32, (CHUNK, BAND), 1)
    for h in range(N_HEADS):
      acc = jnp.zeros((CHUNK, BAND), F32)
      for j in range(NUM_BUCKETS):
        acc = jnp.where(bkt == j, tbl_ref[j, h], acc)
      g = h % GROUP
      for var in range(bias_ref.shape[0]):
        bias_ref[var, h // GROUP, g * CHUNK:(g + 1) * CHUNK, :] = jnp.where(key64 >= var * CHUNK, acc, NEG)

  if has_state:
    for s in range(n_seg):
      gs_ref[s * g_stride + G_PAD - (CONV_K - 1):s * g_stride + G_PAD, :] = state_ref[s]
      for kh in range(N_KV_HEADS):
        kb_ref[kh, s * b_stride:s * b_stride + WINDOW, :] = (
            ck_ref[s, :, kh * HEAD_DIM:(kh + 1) * HEAD_DIM].astype(BF16))
        vb_ref[kh, s * b_stride:s * b_stride + WINDOW, :] = (
            cv_ref[s, :, kh * HEAD_DIM:(kh + 1) * HEAD_DIM].astype(BF16))
  else:
    @pl.when(seq_start)
    def _():
      gs_ref[0:G_PAD, :] = jnp.zeros((G_PAD, CONV_WIDTH), F32)
      kb_ref[:, 0:WINDOW, :] = jnp.zeros((N_KV_HEADS, WINDOW, HEAD_DIM), BF16)
      vb_ref[:, 0:WINDOW, :] = jnp.zeros((N_KV_HEADS, WINDOW, HEAD_DIM), BF16)

    @pl.when(jnp.logical_not(seq_start))
    def _():
      gs_ref[0:G_PAD, :] = gs_ref[seg_len:seg_len + G_PAD, :]
      kb_ref[:, 0:WINDOW, :] = kb_ref[:, seg_len:seg_len + WINDOW, :]
      vb_ref[:, 0:WINDOW, :] = vb_ref[:, seg_len:seg_len + WINDOW, :]

  gpre = gpre_ref[...]

  def norm_body(i, carry):
    r = pl.multiple_of(i * NORM_ROWS, NORM_ROWS)
    h_ref[pl.ds(r, NORM_ROWS), :] = _rms_scale(x_ref[pl.ds(r, NORM_ROWS), :], gpre).astype(BF16)
    return carry

  lax.fori_loop(0, rows // NORM_ROWS, norm_body, 0, unroll=NORM_UNROLL)

  for c0 in range(0, ATTN_WIDTH, CONV_COLS):
    zc_ref[:, 0:CONV_COLS] = jnp.dot(
        h_ref[...], win_ref[:, OFF_Q + c0:OFF_Q + c0 + CONV_COLS], preferred_element_type=F32)
    for hh in range(CONV_COLS // HEAD_DIM):
      q_ref[c0 // HEAD_DIM + hh] = (
          zc_ref[:, hh * HEAD_DIM:(hh + 1) * HEAD_DIM] * (HEAD_DIM ** -0.5)).astype(BF16)

  kvf_ref[...] = jnp.dot(h_ref[...], win_ref[:, OFF_KV:OFF_KV + 2 * KV_WIDTH], preferred_element_type=F32)
  for s in range(n_seg):
    for kh in range(N_KV_HEADS):
      kb_ref[kh, s * b_stride + WINDOW:(s + 1) * b_stride, :] = (
          kvf_ref[s * seg_len:(s + 1) * seg_len, kh * HEAD_DIM:(kh + 1) * HEAD_DIM].astype(BF16))
      vb_ref[kh, s * b_stride + WINDOW:(s + 1) * b_stride, :] = (
          kvf_ref[s * seg_len:(s + 1) * seg_len,
                  KV_WIDTH + kh * HEAD_DIM:KV_WIDTH + (kh + 1) * HEAD_DIM].astype(BF16))

  for s in range(n_seg):
    if seg_len >= WINDOW:
      nk_ref[s] = kvf_ref[(s + 1) * seg_len - WINDOW:(s + 1) * seg_len, 0:KV_WIDTH]
      nv_ref[s] = kvf_ref[(s + 1) * seg_len - WINDOW:(s + 1) * seg_len, KV_WIDTH:2 * KV_WIDTH]
    else:
      nk_ref[s, 0:WINDOW - seg_len, :] = ck_ref[s, seg_len:WINDOW, :]
      nv_ref[s, 0:WINDOW - seg_len, :] = cv_ref[s, seg_len:WINDOW, :]
      nk_ref[s, WINDOW - seg_len:WINDOW, :] = kvf_ref[s * seg_len:(s + 1) * seg_len, 0:KV_WIDTH]
      nv_ref[s, WINDOW - seg_len:WINDOW, :] = kvf_ref[s * seg_len:(s + 1) * seg_len, KV_WIDTH:2 * KV_WIDTH]

  row_id = lax.broadcasted_iota(jnp.int32, (GROUP * CHUNK, 1), 0)
  sink_cols = []
  for kh in range(N_KV_HEADS):
    col = jnp.full((GROUP * CHUNK, 1), sink_ref[kh * GROUP + GROUP - 1], F32)
    for g in range(GROUP - 2, -1, -1):
      col = jnp.where(row_id < (g + 1) * CHUNK, sink_ref[kh * GROUP + g], col)
    sink_cols.append(col)
  n_items = rows // CHUNK
  assert n_seg == 1 or seg_len == CHUNK
  band_step = CHUNK if n_seg == 1 else b_stride

  def attend(item):
    q0 = pl.multiple_of(item * CHUNK, CHUNK)
    b0 = pl.multiple_of(item * band_step, CHUNK)
    if has_state:
      var = 0
    else:
      var = jnp.where(seq_start, jnp.maximum(WINDOW // CHUNK - item, 0), 0)
    for kh in range(N_KV_HEADS):
      qs = jnp.concatenate(
          [q_ref[kh * GROUP + g, pl.ds(q0, CHUNK), :] for g in range(GROUP)], axis=0)
      kband = kb_ref[kh, pl.ds(b0, BAND), :]
      vband = vb_ref[kh, pl.ds(b0, BAND), :]
      s = lax.dot_general(qs, kband, (((1,), (1,)), ((), ())), preferred_element_type=F32)
      s = s + bias_ref[var, kh]
      sink = sink_cols[kh]
      m = jnp.maximum(jnp.max(s, axis=-1, keepdims=True), sink)
      p = jnp.exp(s - m)
      denom = jnp.sum(p, axis=-1, keepdims=True) + jnp.exp(sink - m)
      o = jnp.dot(p.astype(BF16), vband, preferred_element_type=F32) * (1.0 / denom)
      o = jnp.concatenate([o[g * CHUNK:(g + 1) * CHUNK, :] for g in range(GROUP)], axis=1)
      mixed_ref[pl.ds(q0, CHUNK), CONV_WIDTH + kh * GROUP * HEAD_DIM:
                CONV_WIDTH + (kh + 1) * GROUP * HEAD_DIM] = o.astype(BF16)

  n_steps = n_items // ITEMS_PER_STEP
  piece = 3 * CONV_WIDTH // n_steps
  assert n_steps * ITEMS_PER_STEP == n_items and piece * n_steps == 3 * CONV_WIDTH
  assert piece % CONV_COLS == 0 and piece // CONV_COLS <= ITEMS_PER_STEP

  def step_body(j, carry):
    for u in range(ITEMS_PER_STEP):
      if u < piece // CONV_COLS:
        c0 = pl.multiple_of(j * piece + u * CONV_COLS, MXU_COLS)
        zc_ref[:, pl.ds(c0, CONV_COLS)] = jnp.dot(
            h_ref[...], win_ref[:, pl.ds(c0, CONV_COLS)], preferred_element_type=F32)
      attend(j * ITEMS_PER_STEP + u)
    return carry

  lax.fori_loop(0, n_steps, step_body, 0, unroll=STEP_UNROLL)

  for c0 in range(0, CONV_WIDTH, CONV_COLS):
    cols = slice(c0, c0 + CONV_COLS)
    cb_cols, cc_cols, cu_cols = (slice(off + c0, off + c0 + CONV_COLS) for off in (OFF_CB, OFF_CC, OFF_CU))
    for s in range(n_seg):
      for r in range(0, seg_len, CONV_ROWS):
        zr = s * seg_len + r
        gr = s * g_stride + G_PAD + r
        gs_ref[gr:gr + CONV_ROWS, cols] = (
            zc_ref[zr:zr + CONV_ROWS, cc_cols] * zc_ref[zr:zr + CONV_ROWS, cu_cols])
    for s in range(n_seg):
      for r in range(0, seg_len, CONV_ROWS):
        zr = s * seg_len + r
        gr = s * g_stride + G_PAD + r
        conv = convw_ref[0:1, cols] * gs_ref[gr - 2:gr - 2 + CONV_ROWS, cols]
        conv = conv + convw_ref[1:2, cols] * gs_ref[gr - 1:gr - 1 + CONV_ROWS, cols]
        conv = conv + convw_ref[2:3, cols] * gs_ref[gr:gr + CONV_ROWS, cols]
        mixed_ref[zr:zr + CONV_ROWS, cols] = (zc_ref[zr:zr + CONV_ROWS, cb_cols] * conv).astype(BF16)

  for s in range(n_seg):
    g_end = s * g_stride + G_PAD + seg_len
    nconv_ref[s] = gs_ref[g_end - (CONV_K - 1):g_end, :]


def _mixer_call(x, state, cache_k, cache_v, bkt, rel_table, sinks, g_pre, w_in, conv_w, *, has_state):
  n_seq, seq_len, _ = x.shape
  if has_state:
    n_seg, seg_len = MIX_ROWS // seq_len, seq_len
    grid = (n_seq // n_seg, 1)
    x = x.reshape(n_seq // n_seg, MIX_ROWS, D_MODEL)
    x_map = lambda b, s: (b, 0, 0)
    seq_map = lambda b, s: (b, 0, 0)
  else:
    n_seg, seg_len = 1, MIX_ROWS
    grid = (n_seq, seq_len // MIX_ROWS)
    x_map = lambda b, s: (b, s, 0)
    seq_map = lambda b, s: (b, 0, 0)
  const2 = lambda b, s: (0, 0)
  once = pl.Buffered(1)

  smem = pl.BlockSpec(memory_space=pltpu.SMEM)
  in_specs = [
      smem, smem,
      pl.BlockSpec((CHUNK, BAND), const2, pipeline_mode=once),
      pl.BlockSpec((None, MIX_ROWS, D_MODEL), x_map, pipeline_mode=once if has_state else None),
      pl.BlockSpec((1, D_MODEL), const2, pipeline_mode=once),
      pl.BlockSpec((D_MODEL, IN_COLS), const2, pipeline_mode=once),
      pl.BlockSpec((CONV_K, CONV_WIDTH), const2, pipeline_mode=once),
  ]
  args = [rel_table, sinks, bkt, x, g_pre, w_in, conv_w]
  if has_state:
    in_specs += [
        pl.BlockSpec((n_seg, CONV_K - 1, CONV_WIDTH), seq_map),
        pl.BlockSpec((n_seg, WINDOW, KV_WIDTH), seq_map, pipeline_mode=once),
        pl.BlockSpec((n_seg, WINDOW, KV_WIDTH), seq_map, pipeline_mode=once),
    ]
    args += [state, cache_k, cache_v]
  out_shape = (
      jax.ShapeDtypeStruct(x.shape, BF16),
      jax.ShapeDtypeStruct((n_seq, CONV_K - 1, CONV_WIDTH), F32),
      jax.ShapeDtypeStruct((n_seq, WINDOW, KV_WIDTH), F32),
      jax.ShapeDtypeStruct((n_seq, WINDOW, KV_WIDTH), F32),
  )
  out_specs = (
      pl.BlockSpec((None, MIX_ROWS, D_MODEL), x_map),
      pl.BlockSpec((n_seg, CONV_K - 1, CONV_WIDTH), seq_map),
      pl.BlockSpec((n_seg, WINDOW, KV_WIDTH), seq_map),
      pl.BlockSpec((n_seg, WINDOW, KV_WIDTH), seq_map),
  )
  band_rows = n_seg * (WINDOW + seg_len)
  scratch = [
      pltpu.VMEM((1 if has_state else N_BIAS_VARIANTS, N_KV_HEADS, GROUP * CHUNK, BAND), F32),
      pltpu.VMEM((MIX_ROWS, D_MODEL), BF16),
      pltpu.VMEM((MIX_ROWS, 3 * CONV_WIDTH), F32),
      pltpu.VMEM((N_HEADS, MIX_ROWS, HEAD_DIM), BF16),
      pltpu.VMEM((MIX_ROWS, 2 * KV_WIDTH), F32),
      pltpu.VMEM((N_KV_HEADS, band_rows, HEAD_DIM), BF16),
      pltpu.VMEM((N_KV_HEADS, band_rows, HEAD_DIM), BF16),
      pltpu.VMEM((n_seg * (G_PAD + seg_len), CONV_WIDTH), F32),
  ]
  kernel = functools.partial(_mixer_kernel, n_seg=n_seg, seg_len=seg_len, has_state=has_state)
  return pl.pallas_call(
      kernel,
      grid=grid,
      in_specs=in_specs,
      out_specs=out_specs,
      out_shape=out_shape,
      scratch_shapes=scratch,
      compiler_params=pltpu.CompilerParams(
          dimension_semantics=("arbitrary", "arbitrary"),
          vmem_limit_bytes=V7X_VMEM_LIMIT_BYTES),
      name="mixer_state" if has_state else "mixer_stream",
  )(*args)


def _outproj_kernel(mixed_ref, x_ref, wout_ref, gpm_ref, gpf_ref, x1_ref, hn_ref, acc_ref):
  gpm = gpm_ref[...]
  gpf = gpf_ref[...]
  part = mixed_ref.shape[0] // OUT_PARTS
  for r0 in range(0, mixed_ref.shape[0], part):
    acc_ref[r0:r0 + part, :] = jnp.dot(
        mixed_ref[r0:r0 + part, :], wout_ref[...], preferred_element_type=F32)
    for r in range(r0, r0 + part, NORM_ROWS):
      x1 = x_ref[r:r + NORM_ROWS, :] + _rms_scale(acc_ref[r:r + NORM_ROWS, :], gpm)
      x1_ref[r:r + NORM_ROWS, :] = x1
      hn_ref[r:r + NORM_ROWS, :] = _rms_scale(x1, gpf).astype(BF16)


def _outproj_call(mixed, x, w_out, g_post_mix, g_pre_ffn):
  rows = x.shape[0]
  row_map = lambda i: (i, 0)
  const2 = lambda i: (0, 0)
  once = pl.Buffered(1)
  return pl.pallas_call(
      _outproj_kernel,
      grid=(rows // OUT_ROWS,),
      in_specs=[
          pl.BlockSpec((OUT_ROWS, D_MODEL), row_map),
          pl.BlockSpec((OUT_ROWS, D_MODEL), row_map),
          pl.BlockSpec((D_MODEL, D_MODEL), const2, pipeline_mode=once),
          pl.BlockSpec((1, D_MODEL), const2, pipeline_mode=once),
          pl.BlockSpec((1, D_MODEL), const2, pipeline_mode=once),
      ],
      out_specs=(pl.BlockSpec((OUT_ROWS, D_MODEL), row_map), pl.BlockSpec((OUT_ROWS, D_MODEL), row_map)),
      out_shape=(jax.ShapeDtypeStruct((rows, D_MODEL), F32), jax.ShapeDtypeStruct((rows, D_MODEL), BF16)),
      scratch_shapes=[pltpu.VMEM((OUT_ROWS, D_MODEL), F32)],
      compiler_params=pltpu.CompilerParams(
          dimension_semantics=("arbitrary",), vmem_limit_bytes=V7X_VMEM_LIMIT_BYTES),
      name="outproj",
  )(mixed, x, w_out, g_post_mix, g_pre_ffn)


def _ffn_kernel(x1_ref, hn_ref, gqf_ref, wg_ref, wu_ref, wd_ref, y_ref):
  j = pl.program_id(1)
  last = pl.num_programs(1) - 1
  rows = y_ref.shape[0]

  @pl.when(j == 0)
  def _():
    y_ref[...] = jnp.zeros(y_ref.shape, F32)

  def accumulate(cols):
    hn = hn_ref[...]
    gate = jnp.dot(hn, wg_ref[:, 0:cols], preferred_element_type=F32)
    up = jnp.dot(hn, wu_ref[:, 0:cols], preferred_element_type=F32)
    mid = (gate * (1.0 / (1.0 + jnp.exp(-gate))) * up).astype(BF16)
    y_ref[...] += jnp.dot(mid, wd_ref[0:cols, :], preferred_element_type=F32)

  @pl.when(j < last)
  def _():
    accumulate(FFN_TILE)

  @pl.when(j == last)
  def _():
    accumulate(D_FF - (pl.cdiv(D_FF, FFN_TILE) - 1) * FFN_TILE)
    gqf = gqf_ref[...]
    for r in range(0, rows, NORM_ROWS):
      y_ref[r:r + NORM_ROWS, :] = x1_ref[r:r + NORM_ROWS, :] + _rms_scale(y_ref[r:r + NORM_ROWS, :], gqf)


def _ffn_call(x1, hn, g_post_ffn, w_gate, w_up, w_down):
  rows = x1.shape[0]
  row_map = lambda i, j: (i, 0)
  return pl.pallas_call(
      _ffn_kernel,
      grid=(rows // FFN_ROWS, pl.cdiv(D_FF, FFN_TILE)),
      in_specs=[
          pl.BlockSpec((FFN_ROWS, D_MODEL), row_map),
          pl.BlockSpec((FFN_ROWS, D_MODEL), row_map),
          pl.BlockSpec((1, D_MODEL), lambda i, j: (0, 0), pipeline_mode=pl.Buffered(1)),
          pl.BlockSpec((D_MODEL, FFN_TILE), lambda i, j: (0, j)),
          pl.BlockSpec((D_MODEL, FFN_TILE), lambda i, j: (0, j)),
          pl.BlockSpec((FFN_TILE, D_MODEL), lambda i, j: (j, 0)),
      ],
      out_specs=pl.BlockSpec((FFN_ROWS, D_MODEL), row_map),
      out_shape=jax.ShapeDtypeStruct((rows, D_MODEL), F32),
      compiler_params=pltpu.CompilerParams(
          dimension_semantics=("arbitrary", "arbitrary"),
          vmem_limit_bytes=V7X_VMEM_LIMIT_BYTES),
      name="ffn",
  )(x1, hn, g_post_ffn, w_gate, w_up, w_down)


def kernel(x_prompt, x_sample, state_conv, cache_k, cache_v, rel_table, g_pre_mix, w_in, conv_w,
           attn_sinks, w_out, g_post_mix, g_pre_ffn, w_gate, w_up, w_down, g_post_ffn):
  depth, n_dec, cache_len = cache_k.shape[:3]
  batch, seq_len, _ = x_prompt.shape
  assert depth == 1 and cache_len == WINDOW and x_sample.shape[1] == CHUNK
  assert seq_len % MIX_ROWS == 0 and MIX_ROWS % CHUNK == 0

  rel = (jnp.arange(BAND) - WINDOW)[None, :] - jnp.arange(CHUNK)[:, None]
  bkt = _t5_bucket(rel).astype(jnp.int32)

  row = lambda g: g[0].reshape(1, D_MODEL)
  win = w_in[0].astype(BF16)
  mix_args = (bkt, rel_table, attn_sinks[0], row(g_pre_mix), win, conv_w[0])
  mixed_p, conv_p, k_p, v_p = _mixer_call(x_prompt, None, None, None, *mix_args, has_state=False)
  mixed_s, conv_s, k_s, v_s = _mixer_call(
      x_sample, state_conv[0], cache_k[0].reshape(n_dec, WINDOW, KV_WIDTH),
      cache_v[0].reshape(n_dec, WINDOW, KV_WIDTH), *mix_args, has_state=True)

  out_args = (w_out[0].astype(BF16), row(g_post_mix), row(g_pre_ffn))
  ffn_args = (row(g_post_ffn), w_gate[0].astype(BF16), w_up[0].astype(BF16), w_down[0].astype(BF16))
  y_p = _ffn_call(*_outproj_call(mixed_p.reshape(-1, D_MODEL), x_prompt.reshape(-1, D_MODEL), *out_args), *ffn_args)
  y_s = _ffn_call(*_outproj_call(mixed_s.reshape(-1, D_MODEL), x_sample.reshape(-1, D_MODEL), *out_args), *ffn_args)

  heads = lambda a: a.reshape(1, a.shape[0], WINDOW, N_KV_HEADS, HEAD_DIM)
  return (y_p.reshape(x_prompt.shape), y_s.reshape(x_sample.shape),
          conv_p[None], heads(k_p), heads(v_p), conv_s[None], heads(k_s), heads(v_s))
```

```python
import functools
import math

import jax
import jax.numpy as jnp
import numpy as np
from jax import lax
from jax.experimental import pallas as pl
from jax.experimental.pallas import tpu as pltpu

D_MODEL = 2048
CHUNK = 64
HEAD_DIM = 64
CONV_WIDTH = D_MODEL // 2
CONV_K = 3
N_HEADS = (D_MODEL - CONV_WIDTH) // HEAD_DIM
N_KV_HEADS = 4
GROUP = N_HEADS // N_KV_HEADS
ATTN_WIDTH = N_HEADS * HEAD_DIM
KV_WIDTH = N_KV_HEADS * HEAD_DIM
IN_COLS = 3 * CONV_WIDTH + ATTN_WIDTH + 2 * KV_WIDTH
WINDOW = 128
BAND = WINDOW + CHUNK
NUM_BUCKETS = 32
MAX_DISTANCE = 128
D_FF = -(-8 * D_MODEL // (3 * 256)) * 256
EPS = 1e-6
NEG = -1e30

OFF_CB, OFF_CC, OFF_CU = 0, CONV_WIDTH, 2 * CONV_WIDTH
OFF_Q = 3 * CONV_WIDTH
OFF_KV = OFF_Q + ATTN_WIDTH

SUBLANES = 8
LANES = 128
V7X_VMEM_LIMIT_BYTES = 56 << 20
MIX_ROWS = 512
OUT_ROWS = 512
OUT_PARTS = 2
FFN_ROWS = 512
FFN_TILE = 1024
CONV_COLS = 512
MXU_COLS = 256
ITEMS_PER_STEP = 8
STEP_UNROLL = 1
NORM_ROWS = 16
NORM_UNROLL = 8
CONV_ROWS = 32
G_PAD = SUBLANES
N_BIAS_VARIANTS = WINDOW // CHUNK + 1

F32 = jnp.float32
BF16 = jnp.bfloat16


def _t5_bucket(rel):
  half = NUM_BUCKETS // 2
  ret = jnp.where(rel > 0, half, 0)
  n = jnp.abs(rel)
  max_exact = half // 2
  nf = jnp.maximum(n, 1).astype(jnp.float32)
  large = max_exact + (jnp.log(nf / max_exact) / math.log(MAX_DISTANCE / max_exact)
                       * (half - max_exact)).astype(jnp.int32)
  large = jnp.minimum(large, half - 1)
  return ret + jnp.where(n < max_exact, n, large)


def _rms_scale(y, gain):
  return y * lax.rsqrt(jnp.mean(y * y, axis=-1, keepdims=True) + EPS) * gain


def _mixer_kernel(tbl_ref, sink_ref, bkt_ref, x_ref, gpre_ref, win_ref, convw_ref, *rest,
                  n_seg, seg_len, has_state):
  if has_state:
    (state_ref, ck_ref, cv_ref, mixed_ref, nconv_ref, nk_ref, nv_ref,
     bias_ref, h_ref, zc_ref, q_ref, kvf_ref, kb_ref, vb_ref, gs_ref) = rest
  else:
    (mixed_ref, nconv_ref, nk_ref, nv_ref,
     bias_ref, h_ref, zc_ref, q_ref, kvf_ref, kb_ref, vb_ref, gs_ref) = rest
  rows = n_seg * seg_len
  g_stride = G_PAD + seg_len
  b_stride = WINDOW + seg_len
  first_call = (pl.program_id(0) == 0) & (pl.program_id(1) == 0)
  seq_start = pl.program_id(1) == 0

  @pl.when(first_call)
  def _():
    bkt = bkt_ref[...]
    key64 = lax.broadcasted_iota(jnp.int32, (CHUNK, BAND), 1)
    for h in range(N_HEADS):
      acc = jnp.zeros((CHUNK, BAND), F32)
      for j in range(NUM_BUCKETS):
        acc = jnp.where(bkt == j, tbl_ref[j, h], acc)
      g = h % GROUP
      for var in range(bias_ref.shape[0]):
        bias_ref[var, h // GROUP, g * CHUNK:(g + 1) * CHUNK, :] = jnp.where(key64 >= var * CHUNK, acc, NEG)

  if has_state:
    for s in range(n_seg):
      gs_ref[s * g_stride + G_PAD - (CONV_K - 1):s * g_stride + G_PAD, :] = state_ref[s]
      for kh in range(N_KV_HEADS):
        kb_ref[kh, s * b_stride:s * b_stride + WINDOW, :] = (
            ck_ref[s, :, kh * HEAD_DIM:(kh + 1) * HEAD_DIM].astype(BF16))
        vb_ref[kh, s * b_stride:s * b_stride + WINDOW, :] = (
            cv_ref[s, :, kh * HEAD_DIM:(kh + 1) * HEAD_DIM].astype(BF16))
  else:
    @pl.when(seq_start)
    def _():
      gs_ref[0:G_PAD, :] = jnp.zeros((G_PAD, CONV_WIDTH), F32)
      kb_ref[:, 0:WINDOW, :] = jnp.zeros((N_KV_HEADS, WINDOW, HEAD_DIM), BF16)
      vb_ref[:, 0:WINDOW, :] = jnp.zeros((N_KV_HEADS, WINDOW, HEAD_DIM), BF16)

    @pl.when(jnp.logical_not(seq_start))
    def _():
      gs_ref[0:G_PAD, :] = gs_ref[seg_len:seg_len + G_PAD, :]
      kb_ref[:, 0:WINDOW, :] = kb_ref[:, seg_len:seg_len + WINDOW, :]
      vb_ref[:, 0:WINDOW, :] = vb_ref[:, seg_len:seg_len + WINDOW, :]

  gpre = gpre_ref[...]

  def norm_body(i, carry):
    r = pl.multiple_of(i * NORM_ROWS, NORM_ROWS)
    h_ref[pl.ds(r, NORM_ROWS), :] = _rms_scale(x_ref[pl.ds(r, NORM_ROWS), :], gpre).astype(BF16)
    return carry

  lax.fori_loop(0, rows // NORM_ROWS, norm_body, 0, unroll=NORM_UNROLL)

  for c0 in range(0, ATTN_WIDTH, CONV_COLS):
    zc_ref[:, 0:CONV_COLS] = jnp.dot(
        h_ref[...], win_ref[:, OFF_Q + c0:OFF_Q + c0 + CONV_COLS], preferred_element_type=F32)
    for hh in range(CONV_COLS // HEAD_DIM):
      q_ref[c0 // HEAD_DIM + hh] = (
          zc_ref[:, hh * HEAD_DIM:(hh + 1) * HEAD_DIM] * (HEAD_DIM ** -0.5)).astype(BF16)

  kvf_ref[...] = jnp.dot(h_ref[...], win_ref[:, OFF_KV:OFF_KV + 2 * KV_WIDTH], preferred_element_type=F32)
  for s in range(n_seg):
    for kh in range(N_KV_HEADS):
      kb_ref[kh, s * b_stride + WINDOW:(s + 1) * b_stride, :] = (
          kvf_ref[s * seg_len:(s + 1) * seg_len, kh * HEAD_DIM:(kh + 1) * HEAD_DIM].astype(BF16))
      vb_ref[kh, s * b_stride + WINDOW:(s + 1) * b_stride, :] = (
          kvf_ref[s * seg_len:(s + 1) * seg_len,
                  KV_WIDTH + kh * HEAD_DIM:KV_WIDTH + (kh + 1) * HEAD_DIM].astype(BF16))

  for s in range(n_seg):
    if seg_len >= WINDOW:
      nk_ref[s] = kvf_ref[(s + 1) * seg_len - WINDOW:(s + 1) * seg_len, 0:KV_WIDTH]
      nv_ref[s] = kvf_ref[(s + 1) * seg_len - WINDOW:(s + 1) * seg_len, KV_WIDTH:2 * KV_WIDTH]
    else:
      nk_ref[s, 0:WINDOW - seg_len, :] = ck_ref[s, seg_len:WINDOW, :]
      nv_ref[s, 0:WINDOW - seg_len, :] = cv_ref[s, seg_len:WINDOW, :]
      nk_ref[s, WINDOW - seg_len:WINDOW, :] = kvf_ref[s * seg_len:(s + 1) * seg_len, 0:KV_WIDTH]
      nv_ref[s, WINDOW - seg_len:WINDOW, :] = kvf_ref[s * seg_len:(s + 1) * seg_len, KV_WIDTH:2 * KV_WIDTH]

  row_id = lax.broadcasted_iota(jnp.int32, (GROUP * CHUNK, 1), 0)
  sink_cols = []
  for kh in range(N_KV_HEADS):
    col = jnp.full((GROUP * CHUNK, 1), sink_ref[kh * GROUP + GROUP - 1], F32)
    for g in range(GROUP - 2, -1, -1):
      col = jnp.where(row_id < (g + 1) * CHUNK, sink_ref[kh * GROUP + g], col)
    sink_cols.append(col)
  n_items = rows // CHUNK
  assert n_seg == 1 or seg_len == CHUNK
  band_step = CHUNK if n_seg == 1 else b_stride

  def attend(item):
    q0 = pl.multiple_of(item * CHUNK, CHUNK)
    b0 = pl.multiple_of(item * band_step, CHUNK)
    if has_state:
      var = 0
    else:
      var = jnp.where(seq_start, jnp.maximum(WINDOW // CHUNK - item, 0), 0)
    for kh in range(N_KV_HEADS):
      qs = jnp.concatenate(
          [q_ref[kh * GROUP + g, pl.ds(q0, CHUNK), :] for g in range(GROUP)], axis=0)
      kband = kb_ref[kh, pl.ds(b0, BAND), :]
      vband = vb_ref[kh, pl.ds(b0, BAND), :]
      s = lax.dot_general(qs, kband, (((1,), (1,)), ((), ())), preferred_element_type=F32)
      s = s + bias_ref[var, kh]
      sink = sink_cols[kh]
      m = jnp.maximum(jnp.max(s, axis=-1, keepdims=True), sink)
      p = jnp.exp(s - m)
      denom = jnp.sum(p, axis=-1, keepdims=True) + jnp.exp(sink - m)
      o = jnp.dot(p.astype(BF16), vband, preferred_element_type=F32) * (1.0 / denom)
      o = jnp.concatenate([o[g * CHUNK:(g + 1) * CHUNK, :] for g in range(GROUP)], axis=1)
      mixed_ref[pl.ds(q0, CHUNK), CONV_WIDTH + kh * GROUP * HEAD_DIM:
                CONV_WIDTH + (kh + 1) * GROUP * HEAD_DIM] = o.astype(BF16)

  n_steps = n_items // ITEMS_PER_STEP
  piece = 3 * CONV_WIDTH // n_steps
  assert n_steps * ITEMS_PER_STEP == n_items and piece * n_steps == 3 * CONV_WIDTH
  assert piece % CONV_COLS == 0 and piece // CONV_COLS <= ITEMS_PER_STEP

  def step_body(j, carry):
    for u in range(ITEMS_PER_STEP):
      if u < piece // CONV_COLS:
        c0 = pl.multiple_of(j * piece + u * CONV_COLS, MXU_COLS)
        zc_ref[:, pl.ds(c0, CONV_COLS)] = jnp.dot(
            h_ref[...], win_ref[:, pl.ds(c0, CONV_COLS)], preferred_element_type=F32)
      attend(j * ITEMS_PER_STEP + u)
    return carry

  lax.fori_loop(0, n_steps, step_body, 0, unroll=STEP_UNROLL)

  for c0 in range(0, CONV_WIDTH, CONV_COLS):
    cols = slice(c0, c0 + CONV_COLS)
    cb_cols, cc_cols, cu_cols = (slice(off + c0, off + c0 + CONV_COLS) for off in (OFF_CB, OFF_CC, OFF_CU))
    for s in range(n_seg):
      for r in range(0, seg_len, CONV_ROWS):
        zr = s * seg_len + r
        gr = s * g_stride + G_PAD + r
        gs_ref[gr:gr + CONV_ROWS, cols] = (
            zc_ref[zr:zr + CONV_ROWS, cc_cols] * zc_ref[zr:zr + CONV_ROWS, cu_cols])
    for s in range(n_seg):
      for r in range(0, seg_len, CONV_ROWS):
        zr = s * seg_len + r
        gr = s * g_stride + G_PAD + r
        conv = convw_ref[0:1, cols] * gs_ref[gr - 2:gr - 2 + CONV_ROWS, cols]
        conv = conv + convw_ref[1:2, cols] * gs_ref[gr - 1:gr - 1 + CONV_ROWS, cols]
        conv = conv + convw_ref[2:3, cols] * gs_ref[gr:gr + CONV_ROWS, cols]
        mixed_ref[zr:zr + CONV_ROWS, cols] = (zc_ref[zr:zr + CONV_ROWS, cb_cols] * conv).astype(BF16)

  for s in range(n_seg):
    g_end = s * g_stride + G_PAD + seg_len
    nconv_ref[s] = gs_ref[g_end - (CONV_K - 1):g_end, :]


def _mixer_call(x, state, cache_k, cache_v, bkt, rel_table, sinks, g_pre, w_in, conv_w, *, has_state):
  n_seq, seq_len, _ = x.shape
  if has_state:
    n_seg, seg_len = MIX_ROWS // seq_len, seq_len
    grid = (n_seq // n_seg, 1)
    x = x.reshape(n_seq // n_seg, MIX_ROWS, D_MODEL)
    x_map = lambda b, s: (b, 0, 0)
    seq_map = lambda b, s: (b, 0, 0)
  else:
    n_seg, seg_len = 1, MIX_ROWS
    grid = (n_seq, seq_len // MIX_ROWS)
    x_map = lambda b, s: (b, s, 0)
    seq_map = lambda b, s: (b, 0, 0)
  const2 = lambda b, s: (0, 0)
  once = pl.Buffered(1)

  smem = pl.BlockSpec(memory_space=pltpu.SMEM)
  in_specs = [
      smem, smem,
      pl.BlockSpec((CHUNK, BAND), const2, pipeline_mode=once),
      pl.BlockSpec((None, MIX_ROWS, D_MODEL), x_map, pipeline_mode=once if has_state else None),
      pl.BlockSpec((1, D_MODEL), const2, pipeline_mode=once),
      pl.BlockSpec((D_MODEL, IN_COLS), const2, pipeline_mode=once),
      pl.BlockSpec((CONV_K, CONV_WIDTH), const2, pipeline_mode=once),
  ]
  args = [rel_table, sinks, bkt, x, g_pre, w_in, conv_w]
  if has_state:
    in_specs += [
        pl.BlockSpec((n_seg, CONV_K - 1, CONV_WIDTH), seq_map),
        pl.BlockSpec((n_seg, WINDOW, KV_WIDTH), seq_map, pipeline_mode=once),
        pl.BlockSpec((n_seg, WINDOW, KV_WIDTH), seq_map, pipeline_mode=once),
    ]
    args += [state, cache_k, cache_v]
  out_shape = (
      jax.ShapeDtypeStruct(x.shape, BF16),
      jax.ShapeDtypeStruct((n_seq, CONV_K - 1, CONV_WIDTH), F32),
      jax.ShapeDtypeStruct((n_seq, WINDOW, KV_WIDTH), F32),
      jax.ShapeDtypeStruct((n_seq, WINDOW, KV_WIDTH), F32),
  )
  out_specs = (
      pl.BlockSpec((None, MIX_ROWS, D_MODEL), x_map),
      pl.BlockSpec((n_seg, CONV_K - 1, CONV_WIDTH), seq_map),
      pl.BlockSpec((n_seg, WINDOW, KV_WIDTH), seq_map),
      pl.BlockSpec((n_seg, WINDOW, KV_WIDTH), seq_map),
  )
  band_rows = n_seg * (WINDOW + seg_len)
  scratch = [
      pltpu.VMEM((1 if has_state else N_BIAS_VARIANTS, N_KV_HEADS, GROUP * CHUNK, BAND), F32),
      pltpu.VMEM((MIX_ROWS, D_MODEL), BF16),
      pltpu.VMEM((MIX_ROWS, 3 * CONV_WIDTH), F32),
      pltpu.VMEM((N_HEADS, MIX_ROWS, HEAD_DIM), BF16),
      pltpu.VMEM((MIX_ROWS, 2 * KV_WIDTH), F32),
      pltpu.VMEM((N_KV_HEADS, band_rows, HEAD_DIM), BF16),
      pltpu.VMEM((N_KV_HEADS, band_rows, HEAD_DIM), BF16),
      pltpu.VMEM((n_seg * (G_PAD + seg_len), CONV_WIDTH), F32),
  ]
  kernel = functools.partial(_mixer_kernel, n_seg=n_seg, seg_len=seg_len, has_state=has_state)
  return pl.pallas_call(
      kernel,
      grid=grid,
      in_specs=in_specs,
      out_specs=out_specs,
      out_shape=out_shape,
      scratch_shapes=scratch,
      compiler_params=pltpu.CompilerParams(
          dimension_semantics=("arbitrary", "arbitrary"),
          vmem_limit_bytes=V7X_VMEM_LIMIT_BYTES),
      name="mixer_state" if has_state else "mixer_stream",
  )(*args)


def _outproj_kernel(mixed_ref, x_ref, wout_ref, gpm_ref, gpf_ref, x1_ref, hn_ref, acc_ref):
  gpm = gpm_ref[...]
  gpf = gpf_ref[...]
  part = mixed_ref.shape[0] // OUT_PARTS
  for r0 in range(0, mixed_ref.shape[0], part):
    acc_ref[r0:r0 + part, :] = jnp.dot(
        mixed_ref[r0:r0 + part, :], wout_ref[...], preferred_element_type=F32)
    for r in range(r0, r0 + part, NORM_ROWS):
      x1 = x_ref[r:r + NORM_ROWS, :] + _rms_scale(acc_ref[r:r + NORM_ROWS, :], gpm)
      x1_ref[r:r + NORM_ROWS, :] = x1
      hn_ref[r:r + NORM_ROWS, :] = _rms_scale(x1, gpf).astype(BF16)


def _outproj_call(mixed, x, w_out, g_post_mix, g_pre_ffn):
  rows = x.shape[0]
  row_map = lambda i: (i, 0)
  const2 = lambda i: (0, 0)
  once = pl.Buffered(1)
  return pl.pallas_call(
      _outproj_kernel,
      grid=(rows // OUT_ROWS,),
      in_specs=[
          pl.BlockSpec((OUT_ROWS, D_MODEL), row_map),
          pl.BlockSpec((OUT_ROWS, D_MODEL), row_map),
          pl.BlockSpec((D_MODEL, D_MODEL), const2, pipeline_mode=once),
          pl.BlockSpec((1, D_MODEL), const2, pipeline_mode=once),
          pl.BlockSpec((1, D_MODEL), const2, pipeline_mode=once),
      ],
      out_specs=(pl.BlockSpec((OUT_ROWS, D_MODEL), row_map), pl.BlockSpec((OUT_ROWS, D_MODEL), row_map)),
      out_shape=(jax.ShapeDtypeStruct((rows, D_MODEL), F32), jax.ShapeDtypeStruct((rows, D_MODEL), BF16)),
      scratch_shapes=[pltpu.VMEM((OUT_ROWS, D_MODEL), F32)],
      compiler_params=pltpu.CompilerParams(
          dimension_semantics=("arbitrary",), vmem_limit_bytes=V7X_VMEM_LIMIT_BYTES),
      name="outproj",
  )(mixed, x, w_out, g_post_mix, g_pre_ffn)


def _ffn_tile(i, j):
  n_tiles = pl.cdiv(D_FF, FFN_TILE)
  return jnp.where(i % 2 == 0, j, n_tiles - 1 - j)


def _ffn_kernel(x1_ref, hn_ref, gqf_ref, wg_ref, wu_ref, wd_ref, y_ref):
  j = pl.program_id(1)
  tile = _ffn_tile(pl.program_id(0), j)
  n_tiles = pl.cdiv(D_FF, FFN_TILE)
  rows = y_ref.shape[0]

  @pl.when(j == 0)
  def _():
    y_ref[...] = jnp.zeros(y_ref.shape, F32)

  def accumulate(cols):
    hn = hn_ref[...]
    gate = jnp.dot(hn, wg_ref[:, 0:cols], preferred_element_type=F32)
    up = jnp.dot(hn, wu_ref[:, 0:cols], preferred_element_type=F32)
    mid = (gate * (1.0 / (1.0 + jnp.exp(-gate))) * up).astype(BF16)
    y_ref[...] += jnp.dot(mid, wd_ref[0:cols, :], preferred_element_type=F32)

  @pl.when(tile < n_tiles - 1)
  def _():
    accumulate(FFN_TILE)

  @pl.when(tile == n_tiles - 1)
  def _():
    accumulate(D_FF - (n_tiles - 1) * FFN_TILE)

  @pl.when(j == n_tiles - 1)
  def _():
    gqf = gqf_ref[...]
    for r in range(0, rows, NORM_ROWS):
      y_ref[r:r + NORM_ROWS, :] = x1_ref[r:r + NORM_ROWS, :] + _rms_scale(y_ref[r:r + NORM_ROWS, :], gqf)


def _ffn_call(x1, hn, g_post_ffn, w_gate, w_up, w_down):
  rows = x1.shape[0]
  row_map = lambda i, j: (i, 0)
  return pl.pallas_call(
      _ffn_kernel,
      grid=(rows // FFN_ROWS, pl.cdiv(D_FF, FFN_TILE)),
      in_specs=[
          pl.BlockSpec((FFN_ROWS, D_MODEL), row_map),
          pl.BlockSpec((FFN_ROWS, D_MODEL), row_map),
          pl.BlockSpec((1, D_MODEL), lambda i, j: (0, 0), pipeline_mode=pl.Buffered(1)),
          pl.BlockSpec((D_MODEL, FFN_TILE), lambda i, j: (0, _ffn_tile(i, j))),
          pl.BlockSpec((D_MODEL, FFN_TILE), lambda i, j: (0, _ffn_tile(i, j))),
          pl.BlockSpec((FFN_TILE, D_MODEL), lambda i, j: (_ffn_tile(i, j), 0)),
      ],
      out_specs=pl.BlockSpec((FFN_ROWS, D_MODEL), row_map),
      out_shape=jax.ShapeDtypeStruct((rows, D_MODEL), F32),
      compiler_params=pltpu.CompilerParams(
          dimension_semantics=("arbitrary", "arbitrary"),
          vmem_limit_bytes=V7X_VMEM_LIMIT_BYTES),
      name="ffn",
  )(x1, hn, g_post_ffn, w_gate, w_up, w_down)


def kernel(x_prompt, x_sample, state_conv, cache_k, cache_v, rel_table, g_pre_mix, w_in, conv_w,
           attn_sinks, w_out, g_post_mix, g_pre_ffn, w_gate, w_up, w_down, g_post_ffn):
  depth, n_dec, cache_len = cache_k.shape[:3]
  batch, seq_len, _ = x_prompt.shape
  assert depth == 1 and cache_len == WINDOW and x_sample.shape[1] == CHUNK
  assert seq_len % MIX_ROWS == 0 and MIX_ROWS % CHUNK == 0

  rel = (jnp.arange(BAND) - WINDOW)[None, :] - jnp.arange(CHUNK)[:, None]
  bkt = _t5_bucket(rel).astype(jnp.int32)

  row = lambda g: g[0].reshape(1, D_MODEL)
  win = w_in[0].astype(BF16)
  mix_args = (bkt, rel_table, attn_sinks[0], row(g_pre_mix), win, conv_w[0])
  mixed_p, conv_p, k_p, v_p = _mixer_call(x_prompt, None, None, None, *mix_args, has_state=False)
  mixed_s, conv_s, k_s, v_s = _mixer_call(
      x_sample, state_conv[0], cache_k[0].reshape(n_dec, WINDOW, KV_WIDTH),
      cache_v[0].reshape(n_dec, WINDOW, KV_WIDTH), *mix_args, has_state=True)

  out_args = (w_out[0].astype(BF16), row(g_post_mix), row(g_pre_ffn))
  ffn_args = (row(g_post_ffn), w_gate[0].astype(BF16), w_up[0].astype(BF16), w_down[0].astype(BF16))
  y_p = _ffn_call(*_outproj_call(mixed_p.reshape(-1, D_MODEL), x_prompt.reshape(-1, D_MODEL), *out_args), *ffn_args)
  y_s = _ffn_call(*_outproj_call(mixed_s.reshape(-1, D_MODEL), x_sample.reshape(-1, D_MODEL), *out_args), *ffn_args)

  heads = lambda a: a.reshape(1, a.shape[0], WINDOW, N_KV_HEADS, HEAD_DIM)
  return (y_p.reshape(x_prompt.shape), y_s.reshape(x_sample.shape),
          conv_p[None], heads(k_p), heads(v_p), conv_s[None], heads(k_s), heads(v_s))
```

```python
import functools
import math

import jax
import jax.numpy as jnp
import numpy as np
from jax import lax
from jax.experimental import pallas as pl
from jax.experimental.pallas import tpu as pltpu

D_MODEL = 2048
CHUNK = 64
HEAD_DIM = 64
CONV_WIDTH = D_MODEL // 2
CONV_K = 3
N_HEADS = (D_MODEL - CONV_WIDTH) // HEAD_DIM
N_KV_HEADS = 4
GROUP = N_HEADS // N_KV_HEADS
ATTN_WIDTH = N_HEADS * HEAD_DIM
KV_WIDTH = N_KV_HEADS * HEAD_DIM
IN_COLS = 3 * CONV_WIDTH + ATTN_WIDTH + 2 * KV_WIDTH
WINDOW = 128
BAND = WINDOW + CHUNK
NUM_BUCKETS = 32
MAX_DISTANCE = 128
D_FF = -(-8 * D_MODEL // (3 * 256)) * 256
EPS = 1e-6
NEG = -1e30

OFF_CB, OFF_CC, OFF_CU = 0, CONV_WIDTH, 2 * CONV_WIDTH
OFF_Q = 3 * CONV_WIDTH
OFF_KV = OFF_Q + ATTN_WIDTH

SUBLANES = 8
LANES = 128
V7X_VMEM_LIMIT_BYTES = 56 << 20
MIX_ROWS = 512
OUT_ROWS = 512
OUT_PARTS = 2
FFN_ROWS = 512
FFN_TILE = 1024
CONV_COLS = 512
MXU_COLS = 256
ITEMS_PER_STEP = 8
STEP_UNROLL = 1
NORM_ROWS = 16
NORM_UNROLL = 8
CONV_ROWS = 32
G_PAD = SUBLANES
N_BIAS_VARIANTS = WINDOW // CHUNK + 1

F32 = jnp.float32
BF16 = jnp.bfloat16


def _t5_bucket(rel):
  half = NUM_BUCKETS // 2
  ret = jnp.where(rel > 0, half, 0)
  n = jnp.abs(rel)
  max_exact = half // 2
  nf = jnp.maximum(n, 1).astype(jnp.float32)
  large = max_exact + (jnp.log(nf / max_exact) / math.log(MAX_DISTANCE / max_exact)
                       * (half - max_exact)).astype(jnp.int32)
  large = jnp.minimum(large, half - 1)
  return ret + jnp.where(n < max_exact, n, large)


def _rms_scale(y, gain):
  return y * lax.rsqrt(jnp.mean(y * y, axis=-1, keepdims=True) + EPS) * gain


def _mixer_kernel(tbl_ref, sink_ref, bkt_ref, x_ref, gpre_ref, win_ref, convw_ref, *rest,
                  n_seg, seg_len, has_state):
  if has_state:
    (state_ref, ck_ref, cv_ref, mixed_ref, nconv_ref, nk_ref, nv_ref,
     bias_ref, h_ref, zc_ref, q_ref, kvf_ref, kb_ref, vb_ref, gs_ref) = rest
  else:
    (mixed_ref, nconv_ref, nk_ref, nv_ref,
     bias_ref, h_ref, zc_ref, q_ref, kvf_ref, kb_ref, vb_ref, gs_ref) = rest
  rows = n_seg * seg_len
  g_stride = G_PAD + seg_len
  b_stride = WINDOW + seg_len
  first_call = (pl.program_id(0) == 0) & (pl.program_id(1) == 0)
  seq_start = pl.program_id(1) == 0

  @pl.when(first_call)
  def _():
    bkt = bkt_ref[...]
    key64 = lax.broadcasted_iota(jnp.int32, (CHUNK, BAND), 1)
    for h in range(N_HEADS):
      acc = jnp.zeros((CHUNK, BAND), F32)
      for j in range(NUM_BUCKETS):
        acc = jnp.where(bkt == j, tbl_ref[j, h], acc)
      g = h % GROUP
      for var in range(bias_ref.shape[0]):
        bias_ref[var, h // GROUP, g * CHUNK:(g + 1) * CHUNK, :] = jnp.where(key64 >= var * CHUNK, acc, NEG)

  if has_state:
    for s in range(n_seg):
      gs_ref[s * g_stride + G_PAD - (CONV_K - 1):s * g_stride + G_PAD, :] = state_ref[s]
      for kh in range(N_KV_HEADS):
        kb_ref[kh, s * b_stride:s * b_stride + WINDOW, :] = (
            ck_ref[s, :, kh * HEAD_DIM:(kh + 1) * HEAD_DIM].astype(BF16))
        vb_ref[kh, s * b_stride:s * b_stride + WINDOW, :] = (
            cv_ref[s, :, kh * HEAD_DIM:(kh + 1) * HEAD_DIM].astype(BF16))
  else:
    @pl.when(seq_start)
    def _():
      gs_ref[0:G_PAD, :] = jnp.zeros((G_PAD, CONV_WIDTH), F32)
      kb_ref[:, 0:WINDOW, :] = jnp.zeros((N_KV_HEADS, WINDOW, HEAD_DIM), BF16)
      vb_ref[:, 0:WINDOW, :] = jnp.zeros((N_KV_HEADS, WINDOW, HEAD_DIM), BF16)

    @pl.when(jnp.logical_not(seq_start))
    def _():
      gs_ref[0:G_PAD, :] = gs_ref[seg_len:seg_len + G_PAD, :]
      kb_ref[:, 0:WINDOW, :] = kb_ref[:, seg_len:seg_len + WINDOW, :]
      vb_ref[:, 0:WINDOW, :] = vb_ref[:, seg_len:seg_len + WINDOW, :]

  gpre = gpre_ref[...]

  def norm_body(i, carry):
    r = pl.multiple_of(i * NORM_ROWS, NORM_ROWS)
    h_ref[pl.ds(r, NORM_ROWS), :] = _rms_scale(x_ref[pl.ds(r, NORM_ROWS), :], gpre).astype(BF16)
    return carry

  lax.fori_loop(0, rows // NORM_ROWS, norm_body, 0, unroll=NORM_UNROLL)

  for c0 in range(0, ATTN_WIDTH, CONV_COLS):
    zc_ref[:, 0:CONV_COLS] = jnp.dot(
        h_ref[...], win_ref[:, OFF_Q + c0:OFF_Q + c0 + CONV_COLS], preferred_element_type=F32)
    for hh in range(CONV_COLS // HEAD_DIM):
      q_ref[c0 // HEAD_DIM + hh] = (
          zc_ref[:, hh * HEAD_DIM:(hh + 1) * HEAD_DIM] * (HEAD_DIM ** -0.5)).astype(BF16)

  kvf_ref[...] = jnp.dot(h_ref[...], win_ref[:, OFF_KV:OFF_KV + 2 * KV_WIDTH], preferred_element_type=F32)
  for s in range(n_seg):
    for kh in range(N_KV_HEADS):
      kb_ref[kh, s * b_stride + WINDOW:(s + 1) * b_stride, :] = (
          kvf_ref[s * seg_len:(s + 1) * seg_len, kh * HEAD_DIM:(kh + 1) * HEAD_DIM].astype(BF16))
      vb_ref[kh, s * b_stride + WINDOW:(s + 1) * b_stride, :] = (
          kvf_ref[s * seg_len:(s + 1) * seg_len,
                  KV_WIDTH + kh * HEAD_DIM:KV_WIDTH + (kh + 1) * HEAD_DIM].astype(BF16))

  for s in range(n_seg):
    if seg_len >= WINDOW:
      nk_ref[s] = kvf_ref[(s + 1) * seg_len - WINDOW:(s + 1) * seg_len, 0:KV_WIDTH]
      nv_ref[s] = kvf_ref[(s + 1) * seg_len - WINDOW:(s + 1) * seg_len, KV_WIDTH:2 * KV_WIDTH]
    else:
      nk_ref[s, 0:WINDOW - seg_len, :] = ck_ref[s, seg_len:WINDOW, :]
      nv_ref[s, 0:WINDOW - seg_len, :] = cv_ref[s, seg_len:WINDOW, :]
      nk_ref[s, WINDOW - seg_len:WINDOW, :] = kvf_ref[s * seg_len:(s + 1) * seg_len, 0:KV_WIDTH]
      nv_ref[s, WINDOW - seg_len:WINDOW, :] = kvf_ref[s * seg_len:(s + 1) * seg_len, KV_WIDTH:2 * KV_WIDTH]

  row_id = lax.broadcasted_iota(jnp.int32, (GROUP * CHUNK, 1), 0)
  sink_cols = []
  for kh in range(N_KV_HEADS):
    col = jnp.full((GROUP * CHUNK, 1), sink_ref[kh * GROUP + GROUP - 1], F32)
    for g in range(GROUP - 2, -1, -1):
      col = jnp.where(row_id < (g + 1) * CHUNK, sink_ref[kh * GROUP + g], col)
    sink_cols.append(col)
  n_items = rows // CHUNK
  assert n_seg == 1 or seg_len == CHUNK
  band_step = CHUNK if n_seg == 1 else b_stride

  def attend(item):
    q0 = pl.multiple_of(item * CHUNK, CHUNK)
    b0 = pl.multiple_of(item * band_step, CHUNK)
    if has_state:
      var = 0
    else:
      var = jnp.where(seq_start, jnp.maximum(WINDOW // CHUNK - item, 0), 0)
    for kh in range(N_KV_HEADS):
      qs = jnp.concatenate(
          [q_ref[kh * GROUP + g, pl.ds(q0, CHUNK), :] for g in range(GROUP)], axis=0)
      kband = kb_ref[kh, pl.ds(b0, BAND), :]
      vband = vb_ref[kh, pl.ds(b0, BAND), :]
      s = lax.dot_general(qs, kband, (((1,), (1,)), ((), ())), preferred_element_type=F32)
      s = s + bias_ref[var, kh]
      sink = sink_cols[kh]
      m = jnp.maximum(jnp.max(s, axis=-1, keepdims=True), sink)
      p = jnp.exp(s - m)
      denom = jnp.sum(p, axis=-1, keepdims=True) + jnp.exp(sink - m)
      o = jnp.dot(p.astype(BF16), vband, preferred_element_type=F32) * (1.0 / denom)
      o = jnp.concatenate([o[g * CHUNK:(g + 1) * CHUNK, :] for g in range(GROUP)], axis=1)
      mixed_ref[pl.ds(q0, CHUNK), CONV_WIDTH + kh * GROUP * HEAD_DIM:
                CONV_WIDTH + (kh + 1) * GROUP * HEAD_DIM] = o.astype(BF16)

  n_steps = n_items // ITEMS_PER_STEP
  piece = 3 * CONV_WIDTH // n_steps
  assert n_steps * ITEMS_PER_STEP == n_items and piece * n_steps == 3 * CONV_WIDTH
  assert piece % CONV_COLS == 0 and piece // CONV_COLS <= ITEMS_PER_STEP

  def step_body(j, carry):
    for u in range(ITEMS_PER_STEP):
      if u < piece // CONV_COLS:
        c0 = pl.multiple_of(j * piece + u * CONV_COLS, MXU_COLS)
        zc_ref[:, pl.ds(c0, CONV_COLS)] = jnp.dot(
            h_ref[...], win_ref[:, pl.ds(c0, CONV_COLS)], preferred_element_type=F32)
      attend(j * ITEMS_PER_STEP + u)
    return carry

  lax.fori_loop(0, n_steps, step_body, 0, unroll=STEP_UNROLL)

  for c0 in range(0, CONV_WIDTH, CONV_COLS):
    cols = slice(c0, c0 + CONV_COLS)
    cb_cols, cc_cols, cu_cols = (slice(off + c0, off + c0 + CONV_COLS) for off in (OFF_CB, OFF_CC, OFF_CU))
    for s in range(n_seg):
      for r in range(0, seg_len, CONV_ROWS):
        zr = s * seg_len + r
        gr = s * g_stride + G_PAD + r
        gs_ref[gr:gr + CONV_ROWS, cols] = (
            zc_ref[zr:zr + CONV_ROWS, cc_cols] * zc_ref[zr:zr + CONV_ROWS, cu_cols])
    for s in range(n_seg):
      for r in range(0, seg_len, CONV_ROWS):
        zr = s * seg_len + r
        gr = s * g_stride + G_PAD + r
        conv = convw_ref[0:1, cols] * gs_ref[gr - 2:gr - 2 + CONV_ROWS, cols]
        conv = conv + convw_ref[1:2, cols] * gs_ref[gr - 1:gr - 1 + CONV_ROWS, cols]
        conv = conv + convw_ref[2:3, cols] * gs_ref[gr:gr + CONV_ROWS, cols]
        mixed_ref[zr:zr + CONV_ROWS, cols] = (zc_ref[zr:zr + CONV_ROWS, cb_cols] * conv).astype(BF16)

  for s in range(n_seg):
    g_end = s * g_stride + G_PAD + seg_len
    nconv_ref[s] = gs_ref[g_end - (CONV_K - 1):g_end, :]


def _mixer_call(x, state, cache_k, cache_v, bkt, rel_table, sinks, g_pre, w_in, conv_w, *, has_state):
  n_seq, seq_len, _ = x.shape
  if has_state:
    n_seg, seg_len = MIX_ROWS // seq_len, seq_len
    grid = (n_seq // n_seg, 1)
    x = x.reshape(n_seq // n_seg, MIX_ROWS, D_MODEL)
    x_map = lambda b, s: (b, 0, 0)
    seq_map = lambda b, s: (b, 0, 0)
  else:
    n_seg, seg_len = 1, MIX_ROWS
    grid = (n_seq, seq_len // MIX_ROWS)
    x_map = lambda b, s: (b, s, 0)
    seq_map = lambda b, s: (b, 0, 0)
  const2 = lambda b, s: (0, 0)
  once = pl.Buffered(1)

  smem = pl.BlockSpec(memory_space=pltpu.SMEM)
  in_specs = [
      smem, smem,
      pl.BlockSpec((CHUNK, BAND), const2, pipeline_mode=once),
      pl.BlockSpec((None, MIX_ROWS, D_MODEL), x_map, pipeline_mode=once if has_state else None),
      pl.BlockSpec((1, D_MODEL), const2, pipeline_mode=once),
      pl.BlockSpec((D_MODEL, IN_COLS), const2, pipeline_mode=once),
      pl.BlockSpec((CONV_K, CONV_WIDTH), const2, pipeline_mode=once),
  ]
  args = [rel_table, sinks, bkt, x, g_pre, w_in, conv_w]
  if has_state:
    in_specs += [
        pl.BlockSpec((n_seg, CONV_K - 1, CONV_WIDTH), seq_map),
        pl.BlockSpec((n_seg, WINDOW, KV_WIDTH), seq_map, pipeline_mode=once),
        pl.BlockSpec((n_seg, WINDOW, KV_WIDTH), seq_map, pipeline_mode=once),
    ]
    args += [state, cache_k, cache_v]
  out_shape = (
      jax.ShapeDtypeStruct(x.shape, BF16),
      jax.ShapeDtypeStruct((n_seq, CONV_K - 1, CONV_WIDTH), F32),
      jax.ShapeDtypeStruct((n_seq, WINDOW, KV_WIDTH), F32),
      jax.ShapeDtypeStruct((n_seq, WINDOW, KV_WIDTH), F32),
  )
  out_specs = (
      pl.BlockSpec((None, MIX_ROWS, D_MODEL), x_map),
      pl.BlockSpec((n_seg, CONV_K - 1, CONV_WIDTH), seq_map),
      pl.BlockSpec((n_seg, WINDOW, KV_WIDTH), seq_map),
      pl.BlockSpec((n_seg, WINDOW, KV_WIDTH), seq_map),
  )
  band_rows = n_seg * (WINDOW + seg_len)
  scratch = [
      pltpu.VMEM((1 if has_state else N_BIAS_VARIANTS, N_KV_HEADS, GROUP * CHUNK, BAND), F32),
      pltpu.VMEM((MIX_ROWS, D_MODEL), BF16),
      pltpu.VMEM((MIX_ROWS, 3 * CONV_WIDTH), F32),
      pltpu.VMEM((N_HEADS, MIX_ROWS, HEAD_DIM), BF16),
      pltpu.VMEM((MIX_ROWS, 2 * KV_WIDTH), F32),
      pltpu.VMEM((N_KV_HEADS, band_rows, HEAD_DIM), BF16),
      pltpu.VMEM((N_KV_HEADS, band_rows, HEAD_DIM), BF16),
      pltpu.VMEM((n_seg * (G_PAD + seg_len), CONV_WIDTH), F32),
  ]
  kernel = functools.partial(_mixer_kernel, n_seg=n_seg, seg_len=seg_len, has_state=has_state)
  return pl.pallas_call(
      kernel,
      grid=grid,
      in_specs=in_specs,
      out_specs=out_specs,
      out_shape=out_shape,
      scratch_shapes=scratch,
      compiler_params=pltpu.CompilerParams(
          dimension_semantics=("arbitrary", "arbitrary"),
          vmem_limit_bytes=V7X_VMEM_LIMIT_BYTES),
      name="mixer_state" if has_state else "mixer_stream",
  )(*args)


def _outproj_kernel(mixed_ref, x_ref, wout_ref, gpm_ref, gpf_ref, x1_ref, hn_ref, acc_ref):
  gpm = gpm_ref[...]
  gpf = gpf_ref[...]
  part = mixed_ref.shape[0] // OUT_PARTS
  for r0 in range(0, mixed_ref.shape[0], part):
    acc_ref[r0:r0 + part, :] = jnp.dot(
        mixed_ref[r0:r0 + part, :], wout_ref[...], preferred_element_type=F32)
    for r in range(r0, r0 + part, NORM_ROWS):
      x1 = x_ref[r:r + NORM_ROWS, :] + _rms_scale(acc_ref[r:r + NORM_ROWS, :], gpm)
      x1_ref[r:r + NORM_ROWS, :] = x1
      hn_ref[r:r + NORM_ROWS, :] = _rms_scale(x1, gpf).astype(BF16)


def _outproj_call(mixed, x, w_out, g_post_mix, g_pre_ffn):
  rows = x.shape[0]
  row_map = lambda i: (i, 0)
  const2 = lambda i: (0, 0)
  once = pl.Buffered(1)
  return pl.pallas_call(
      _outproj_kernel,
      grid=(rows // OUT_ROWS,),
      in_specs=[
          pl.BlockSpec((OUT_ROWS, D_MODEL), row_map),
          pl.BlockSpec((OUT_ROWS, D_MODEL), row_map),
          pl.BlockSpec((D_MODEL, D_MODEL), const2, pipeline_mode=once),
          pl.BlockSpec((1, D_MODEL), const2, pipeline_mode=once),
          pl.BlockSpec((1, D_MODEL), const2, pipeline_mode=once),
      ],
      out_specs=(pl.BlockSpec((OUT_ROWS, D_MODEL), row_map), pl.BlockSpec((OUT_ROWS, D_MODEL), row_map)),
      out_shape=(jax.ShapeDtypeStruct((rows, D_MODEL), F32), jax.ShapeDtypeStruct((rows, D_MODEL), BF16)),
      scratch_shapes=[pltpu.VMEM((OUT_ROWS, D_MODEL), F32)],
      compiler_params=pltpu.CompilerParams(
          dimension_semantics=("arbitrary",), vmem_limit_bytes=V7X_VMEM_LIMIT_BYTES),
      name="outproj",
  )(mixed, x, w_out, g_post_mix, g_pre_ffn)


def _ffn_tile(i, j):
  n_tiles = pl.cdiv(D_FF, FFN_TILE)
  step = jnp.where(i % 2 == 0, j, n_tiles - 1 - j)
  mid = n_tiles // 2 - 1
  return jnp.where(step < mid, step, jnp.where(step == mid, n_tiles - 1, step - 1))


def _ffn_kernel(x1_ref, hn_ref, gqf_ref, wg_ref, wu_ref, wd_ref, y_ref):
  j = pl.program_id(1)
  tile = _ffn_tile(pl.program_id(0), j)
  n_tiles = pl.cdiv(D_FF, FFN_TILE)
  rows = y_ref.shape[0]

  @pl.when(j == 0)
  def _():
    y_ref[...] = jnp.zeros(y_ref.shape, F32)

  def accumulate(cols):
    hn = hn_ref[...]
    gate = jnp.dot(hn, wg_ref[:, 0:cols], preferred_element_type=F32)
    up = jnp.dot(hn, wu_ref[:, 0:cols], preferred_element_type=F32)
    mid = (gate * (1.0 / (1.0 + jnp.exp(-gate))) * up).astype(BF16)
    y_ref[...] += jnp.dot(mid, wd_ref[0:cols, :], preferred_element_type=F32)

  @pl.when(tile < n_tiles - 1)
  def _():
    accumulate(FFN_TILE)

  @pl.when(tile == n_tiles - 1)
  def _():
    accumulate(D_FF - (n_tiles - 1) * FFN_TILE)

  @pl.when(j == n_tiles - 1)
  def _():
    gqf = gqf_ref[...]
    for r in range(0, rows, NORM_ROWS):
      y_ref[r:r + NORM_ROWS, :] = x1_ref[r:r + NORM_ROWS, :] + _rms_scale(y_ref[r:r + NORM_ROWS, :], gqf)


def _ffn_call(x1, hn, g_post_ffn, w_gate, w_up, w_down):
  rows = x1.shape[0]
  row_map = lambda i, j: (i, 0)
  return pl.pallas_call(
      _ffn_kernel,
      grid=(rows // FFN_ROWS, pl.cdiv(D_FF, FFN_TILE)),
      in_specs=[
          pl.BlockSpec((FFN_ROWS, D_MODEL), row_map),
          pl.BlockSpec((FFN_ROWS, D_MODEL), row_map),
          pl.BlockSpec((1, D_MODEL), lambda i, j: (0, 0), pipeline_mode=pl.Buffered(1)),
          pl.BlockSpec((D_MODEL, FFN_TILE), lambda i, j: (0, _ffn_tile(i, j))),
          pl.BlockSpec((D_MODEL, FFN_TILE), lambda i, j: (0, _ffn_tile(i, j))),
          pl.BlockSpec((FFN_TILE, D_MODEL), lambda i, j: (_ffn_tile(i, j), 0)),
      ],
      out_specs=pl.BlockSpec((FFN_ROWS, D_MODEL), row_map),
      out_shape=jax.ShapeDtypeStruct((rows, D_MODEL), F32),
      compiler_params=pltpu.CompilerParams(
          dimension_semantics=("arbitrary", "arbitrary"),
          vmem_limit_bytes=V7X_VMEM_LIMIT_BYTES),
      name="ffn",
  )(x1, hn, g_post_ffn, w_gate, w_up, w_down)


def kernel(x_prompt, x_sample, state_conv, cache_k, cache_v, rel_table, g_pre_mix, w_in, conv_w,
           attn_sinks, w_out, g_post_mix, g_pre_ffn, w_gate, w_up, w_down, g_post_ffn):
  depth, n_dec, cache_len = cache_k.shape[:3]
  batch, seq_len, _ = x_prompt.shape
  assert depth == 1 and cache_len == WINDOW and x_sample.shape[1] == CHUNK
  assert seq_len % MIX_ROWS == 0 and MIX_ROWS % CHUNK == 0

  rel = (jnp.arange(BAND) - WINDOW)[None, :] - jnp.arange(CHUNK)[:, None]
  bkt = _t5_bucket(rel).astype(jnp.int32)

  row = lambda g: g[0].reshape(1, D_MODEL)
  win = w_in[0].astype(BF16)
  mix_args = (bkt, rel_table, attn_sinks[0], row(g_pre_mix), win, conv_w[0])
  mixed_p, conv_p, k_p, v_p = _mixer_call(x_prompt, None, None, None, *mix_args, has_state=False)
  mixed_s, conv_s, k_s, v_s = _mixer_call(
      x_sample, state_conv[0], cache_k[0].reshape(n_dec, WINDOW, KV_WIDTH),
      cache_v[0].reshape(n_dec, WINDOW, KV_WIDTH), *mix_args, has_state=True)

  out_args = (w_out[0].astype(BF16), row(g_post_mix), row(g_pre_ffn))
  ffn_args = (row(g_post_ffn), w_gate[0].astype(BF16), w_up[0].astype(BF16), w_down[0].astype(BF16))
  y_p = _ffn_call(*_outproj_call(mixed_p.reshape(-1, D_MODEL), x_prompt.reshape(-1, D_MODEL), *out_args), *ffn_args)
  y_s = _ffn_call(*_outproj_call(mixed_s.reshape(-1, D_MODEL), x_sample.reshape(-1, D_MODEL), *out_args), *ffn_args)

  heads = lambda a: a.reshape(1, a.shape[0], WINDOW, N_KV_HEADS, HEAD_DIM)
  return (y_p.reshape(x_prompt.shape), y_s.reshape(x_sample.shape),
          conv_p[None], heads(k_p), heads(v_p), conv_s[None], heads(k_s), heads(v_s))
```

```python
import functools
import math

import jax
import jax.numpy as jnp
import numpy as np
from jax import lax
from jax.experimental import pallas as pl
from jax.experimental.pallas import tpu as pltpu

D_MODEL = 2048
CHUNK = 64
HEAD_DIM = 64
CONV_WIDTH = D_MODEL // 2
CONV_K = 3
N_HEADS = (D_MODEL - CONV_WIDTH) // HEAD_DIM
N_KV_HEADS = 4
GROUP = N_HEADS // N_KV_HEADS
ATTN_WIDTH = N_HEADS * HEAD_DIM
KV_WIDTH = N_KV_HEADS * HEAD_DIM
IN_COLS = 3 * CONV_WIDTH + ATTN_WIDTH + 2 * KV_WIDTH
WINDOW = 128
BAND = WINDOW + CHUNK
NUM_BUCKETS = 32
MAX_DISTANCE = 128
D_FF = -(-8 * D_MODEL // (3 * 256)) * 256
EPS = 1e-6
NEG = -1e30

OFF_CB, OFF_CC, OFF_CU = 0, CONV_WIDTH, 2 * CONV_WIDTH
OFF_Q = 3 * CONV_WIDTH
OFF_KV = OFF_Q + ATTN_WIDTH

SUBLANES = 8
LANES = 128
V7X_VMEM_LIMIT_BYTES = 56 << 20
MIX_ROWS = 512
OUT_ROWS = 512
OUT_PARTS = 4
FFN_ROWS = 512
FFN_TILE = 1024
CONV_COLS = 512
MXU_COLS = 256
ITEMS_PER_STEP = 8
STEP_UNROLL = 1
NORM_ROWS = 16
NORM_UNROLL = 8
CONV_ROWS = 32
G_PAD = SUBLANES
N_BIAS_VARIANTS = WINDOW // CHUNK + 1

F32 = jnp.float32
BF16 = jnp.bfloat16


def _t5_bucket(rel):
  half = NUM_BUCKETS // 2
  ret = jnp.where(rel > 0, half, 0)
  n = jnp.abs(rel)
  max_exact = half // 2
  nf = jnp.maximum(n, 1).astype(jnp.float32)
  large = max_exact + (jnp.log(nf / max_exact) / math.log(MAX_DISTANCE / max_exact)
                       * (half - max_exact)).astype(jnp.int32)
  large = jnp.minimum(large, half - 1)
  return ret + jnp.where(n < max_exact, n, large)


def _rms_scale(y, gain):
  return y * lax.rsqrt(jnp.mean(y * y, axis=-1, keepdims=True) + EPS) * gain


def _mixer_kernel(tbl_ref, sink_ref, bkt_ref, x_ref, gpre_ref, win_ref, convw_ref, *rest,
                  n_seg, seg_len, has_state):
  if has_state:
    (state_ref, ck_ref, cv_ref, mixed_ref, nconv_ref, nk_ref, nv_ref,
     bias_ref, h_ref, zc_ref, q_ref, kvf_ref, kb_ref, vb_ref, gs_ref) = rest
  else:
    (mixed_ref, nconv_ref, nk_ref, nv_ref,
     bias_ref, h_ref, zc_ref, q_ref, kvf_ref, kb_ref, vb_ref, gs_ref) = rest
  rows = n_seg * seg_len
  g_stride = G_PAD + seg_len
  b_stride = WINDOW + seg_len
  first_call = (pl.program_id(0) == 0) & (pl.program_id(1) == 0)
  seq_start = pl.program_id(1) == 0

  @pl.when(first_call)
  def _():
    bkt = bkt_ref[...]
    key64 = lax.broadcasted_iota(jnp.int32, (CHUNK, BAND), 1)
    for h in range(N_HEADS):
      acc = jnp.zeros((CHUNK, BAND), F32)
      for j in range(NUM_BUCKETS):
        acc = jnp.where(bkt == j, tbl_ref[j, h], acc)
      g = h % GROUP
      for var in range(bias_ref.shape[0]):
        bias_ref[var, h // GROUP, g * CHUNK:(g + 1) * CHUNK, :] = jnp.where(key64 >= var * CHUNK, acc, NEG)

  if has_state:
    for s in range(n_seg):
      gs_ref[s * g_stride + G_PAD - (CONV_K - 1):s * g_stride + G_PAD, :] = state_ref[s]
      for kh in range(N_KV_HEADS):
        kb_ref[kh, s * b_stride:s * b_stride + WINDOW, :] = (
            ck_ref[s, :, kh * HEAD_DIM:(kh + 1) * HEAD_DIM].astype(BF16))
        vb_ref[kh, s * b_stride:s * b_stride + WINDOW, :] = (
            cv_ref[s, :, kh * HEAD_DIM:(kh + 1) * HEAD_DIM].astype(BF16))
  else:
    @pl.when(seq_start)
    def _():
      gs_ref[0:G_PAD, :] = jnp.zeros((G_PAD, CONV_WIDTH), F32)
      kb_ref[:, 0:WINDOW, :] = jnp.zeros((N_KV_HEADS, WINDOW, HEAD_DIM), BF16)
      vb_ref[:, 0:WINDOW, :] = jnp.zeros((N_KV_HEADS, WINDOW, HEAD_DIM), BF16)

    @pl.when(jnp.logical_not(seq_start))
    def _():
      gs_ref[0:G_PAD, :] = gs_ref[seg_len:seg_len + G_PAD, :]
      kb_ref[:, 0:WINDOW, :] = kb_ref[:, seg_len:seg_len + WINDOW, :]
      vb_ref[:, 0:WINDOW, :] = vb_ref[:, seg_len:seg_len + WINDOW, :]

  gpre = gpre_ref[...]

  def norm_body(i, carry):
    r = pl.multiple_of(i * NORM_ROWS, NORM_ROWS)
    h_ref[pl.ds(r, NORM_ROWS), :] = _rms_scale(x_ref[pl.ds(r, NORM_ROWS), :], gpre).astype(BF16)
    return carry

  lax.fori_loop(0, rows // NORM_ROWS, norm_body, 0, unroll=NORM_UNROLL)

  for c0 in range(0, ATTN_WIDTH, CONV_COLS):
    zc_ref[:, 0:CONV_COLS] = jnp.dot(
        h_ref[...], win_ref[:, OFF_Q + c0:OFF_Q + c0 + CONV_COLS], preferred_element_type=F32)
    for hh in range(CONV_COLS // HEAD_DIM):
      q_ref[c0 // HEAD_DIM + hh] = (
          zc_ref[:, hh * HEAD_DIM:(hh + 1) * HEAD_DIM] * (HEAD_DIM ** -0.5)).astype(BF16)

  kvf_ref[...] = jnp.dot(h_ref[...], win_ref[:, OFF_KV:OFF_KV + 2 * KV_WIDTH], preferred_element_type=F32)
  for s in range(n_seg):
    for kh in range(N_KV_HEADS):
      kb_ref[kh, s * b_stride + WINDOW:(s + 1) * b_stride, :] = (
          kvf_ref[s * seg_len:(s + 1) * seg_len, kh * HEAD_DIM:(kh + 1) * HEAD_DIM].astype(BF16))
      vb_ref[kh, s * b_stride + WINDOW:(s + 1) * b_stride, :] = (
          kvf_ref[s * seg_len:(s + 1) * seg_len,
                  KV_WIDTH + kh * HEAD_DIM:KV_WIDTH + (kh + 1) * HEAD_DIM].astype(BF16))

  for s in range(n_seg):
    if seg_len >= WINDOW:
      nk_ref[s] = kvf_ref[(s + 1) * seg_len - WINDOW:(s + 1) * seg_len, 0:KV_WIDTH]
      nv_ref[s] = kvf_ref[(s + 1) * seg_len - WINDOW:(s + 1) * seg_len, KV_WIDTH:2 * KV_WIDTH]
    else:
      nk_ref[s, 0:WINDOW - seg_len, :] = ck_ref[s, seg_len:WINDOW, :]
      nv_ref[s, 0:WINDOW - seg_len, :] = cv_ref[s, seg_len:WINDOW, :]
      nk_ref[s, WINDOW - seg_len:WINDOW, :] = kvf_ref[s * seg_len:(s + 1) * seg_len, 0:KV_WIDTH]
      nv_ref[s, WINDOW - seg_len:WINDOW, :] = kvf_ref[s * seg_len:(s + 1) * seg_len, KV_WIDTH:2 * KV_WIDTH]

  row_id = lax.broadcasted_iota(jnp.int32, (GROUP * CHUNK, 1), 0)
  sink_cols = []
  for kh in range(N_KV_HEADS):
    col = jnp.full((GROUP * CHUNK, 1), sink_ref[kh * GROUP + GROUP - 1], F32)
    for g in range(GROUP - 2, -1, -1):
      col = jnp.where(row_id < (g + 1) * CHUNK, sink_ref[kh * GROUP + g], col)
    sink_cols.append(col)
  n_items = rows // CHUNK
  assert n_seg == 1 or seg_len == CHUNK
  band_step = CHUNK if n_seg == 1 else b_stride

  def attend(item):
    q0 = pl.multiple_of(item * CHUNK, CHUNK)
    b0 = pl.multiple_of(item * band_step, CHUNK)
    if has_state:
      var = 0
    else:
      var = jnp.where(seq_start, jnp.maximum(WINDOW // CHUNK - item, 0), 0)
    for kh in range(N_KV_HEADS):
      qs = jnp.concatenate(
          [q_ref[kh * GROUP + g, pl.ds(q0, CHUNK), :] for g in range(GROUP)], axis=0)
      kband = kb_ref[kh, pl.ds(b0, BAND), :]
      vband = vb_ref[kh, pl.ds(b0, BAND), :]
      s = lax.dot_general(qs, kband, (((1,), (1,)), ((), ())), preferred_element_type=F32)
      s = s + bias_ref[var, kh]
      sink = sink_cols[kh]
      m = jnp.maximum(jnp.max(s, axis=-1, keepdims=True), sink)
      p = jnp.exp(s - m)
      denom = jnp.sum(p, axis=-1, keepdims=True) + jnp.exp(sink - m)
      o = jnp.dot(p.astype(BF16), vband, preferred_element_type=F32) * (1.0 / denom)
      o = jnp.concatenate([o[g * CHUNK:(g + 1) * CHUNK, :] for g in range(GROUP)], axis=1)
      mixed_ref[pl.ds(q0, CHUNK), CONV_WIDTH + kh * GROUP * HEAD_DIM:
                CONV_WIDTH + (kh + 1) * GROUP * HEAD_DIM] = o.astype(BF16)

  n_steps = n_items // ITEMS_PER_STEP
  piece = 3 * CONV_WIDTH // n_steps
  assert n_steps * ITEMS_PER_STEP == n_items and piece * n_steps == 3 * CONV_WIDTH
  assert piece % CONV_COLS == 0 and piece // CONV_COLS <= ITEMS_PER_STEP

  def step_body(j, carry):
    for u in range(ITEMS_PER_STEP):
      if u < piece // CONV_COLS:
        c0 = pl.multiple_of(j * piece + u * CONV_COLS, MXU_COLS)
        zc_ref[:, pl.ds(c0, CONV_COLS)] = jnp.dot(
            h_ref[...], win_ref[:, pl.ds(c0, CONV_COLS)], preferred_element_type=F32)
      attend(j * ITEMS_PER_STEP + u)
    return carry

  lax.fori_loop(0, n_steps, step_body, 0, unroll=STEP_UNROLL)

  for c0 in range(0, CONV_WIDTH, CONV_COLS):
    cols = slice(c0, c0 + CONV_COLS)
    cb_cols, cc_cols, cu_cols = (slice(off + c0, off + c0 + CONV_COLS) for off in (OFF_CB, OFF_CC, OFF_CU))
    for s in range(n_seg):
      for r in range(0, seg_len, CONV_ROWS):
        zr = s * seg_len + r
        gr = s * g_stride + G_PAD + r
        gs_ref[gr:gr + CONV_ROWS, cols] = (
            zc_ref[zr:zr + CONV_ROWS, cc_cols] * zc_ref[zr:zr + CONV_ROWS, cu_cols])
    for s in range(n_seg):
      for r in range(0, seg_len, CONV_ROWS):
        zr = s * seg_len + r
        gr = s * g_stride + G_PAD + r
        conv = convw_ref[0:1, cols] * gs_ref[gr - 2:gr - 2 + CONV_ROWS, cols]
        conv = conv + convw_ref[1:2, cols] * gs_ref[gr - 1:gr - 1 + CONV_ROWS, cols]
        conv = conv + convw_ref[2:3, cols] * gs_ref[gr:gr + CONV_ROWS, cols]
        mixed_ref[zr:zr + CONV_ROWS, cols] = (zc_ref[zr:zr + CONV_ROWS, cb_cols] * conv).astype(BF16)

  for s in range(n_seg):
    g_end = s * g_stride + G_PAD + seg_len
    nconv_ref[s] = gs_ref[g_end - (CONV_K - 1):g_end, :]


def _mixer_call(x, state, cache_k, cache_v, bkt, rel_table, sinks, g_pre, w_in, conv_w, *, has_state):
  n_seq, seq_len, _ = x.shape
  if has_state:
    n_seg, seg_len = MIX_ROWS // seq_len, seq_len
    grid = (n_seq // n_seg, 1)
    x = x.reshape(n_seq // n_seg, MIX_ROWS, D_MODEL)
    x_map = lambda b, s: (b, 0, 0)
    seq_map = lambda b, s: (b, 0, 0)
  else:
    n_seg, seg_len = 1, MIX_ROWS
    grid = (n_seq, seq_len // MIX_ROWS)
    x_map = lambda b, s: (b, s, 0)
    seq_map = lambda b, s: (b, 0, 0)
  const2 = lambda b, s: (0, 0)
  once = pl.Buffered(1)

  smem = pl.BlockSpec(memory_space=pltpu.SMEM)
  in_specs = [
      smem, smem,
      pl.BlockSpec((CHUNK, BAND), const2, pipeline_mode=once),
      pl.BlockSpec((None, MIX_ROWS, D_MODEL), x_map, pipeline_mode=once if has_state else None),
      pl.BlockSpec((1, D_MODEL), const2, pipeline_mode=once),
      pl.BlockSpec((D_MODEL, IN_COLS), const2, pipeline_mode=once),
      pl.BlockSpec((CONV_K, CONV_WIDTH), const2, pipeline_mode=once),
  ]
  args = [rel_table, sinks, bkt, x, g_pre, w_in, conv_w]
  if has_state:
    in_specs += [
        pl.BlockSpec((n_seg, CONV_K - 1, CONV_WIDTH), seq_map),
        pl.BlockSpec((n_seg, WINDOW, KV_WIDTH), seq_map, pipeline_mode=once),
        pl.BlockSpec((n_seg, WINDOW, KV_WIDTH), seq_map, pipeline_mode=once),
    ]
    args += [state, cache_k, cache_v]
  out_shape = (
      jax.ShapeDtypeStruct(x.shape, BF16),
      jax.ShapeDtypeStruct((n_seq, CONV_K - 1, CONV_WIDTH), F32),
      jax.ShapeDtypeStruct((n_seq, WINDOW, KV_WIDTH), F32),
      jax.ShapeDtypeStruct((n_seq, WINDOW, KV_WIDTH), F32),
  )
  out_specs = (
      pl.BlockSpec((None, MIX_ROWS, D_MODEL), x_map),
      pl.BlockSpec((n_seg, CONV_K - 1, CONV_WIDTH), seq_map),
      pl.BlockSpec((n_seg, WINDOW, KV_WIDTH), seq_map),
      pl.BlockSpec((n_seg, WINDOW, KV_WIDTH), seq_map),
  )
  band_rows = n_seg * (WINDOW + seg_len)
  scratch = [
      pltpu.VMEM((1 if has_state else N_BIAS_VARIANTS, N_KV_HEADS, GROUP * CHUNK, BAND), F32),
      pltpu.VMEM((MIX_ROWS, D_MODEL), BF16),
      pltpu.VMEM((MIX_ROWS, 3 * CONV_WIDTH), F32),
      pltpu.VMEM((N_HEADS, MIX_ROWS, HEAD_DIM), BF16),
      pltpu.VMEM((MIX_ROWS, 2 * KV_WIDTH), F32),
      pltpu.VMEM((N_KV_HEADS, band_rows, HEAD_DIM), BF16),
      pltpu.VMEM((N_KV_HEADS, band_rows, HEAD_DIM), BF16),
      pltpu.VMEM((n_seg * (G_PAD + seg_len), CONV_WIDTH), F32),
  ]
  kernel = functools.partial(_mixer_kernel, n_seg=n_seg, seg_len=seg_len, has_state=has_state)
  return pl.pallas_call(
      kernel,
      grid=grid,
      in_specs=in_specs,
      out_specs=out_specs,
      out_shape=out_shape,
      scratch_shapes=scratch,
      compiler_params=pltpu.CompilerParams(
          dimension_semantics=("arbitrary", "arbitrary"),
          vmem_limit_bytes=V7X_VMEM_LIMIT_BYTES),
      name="mixer_state" if has_state else "mixer_stream",
  )(*args)


def _outproj_kernel(mixed_ref, x_ref, wout_ref, gpm_ref, gpf_ref, x1_ref, hn_ref, acc_ref):
  gpm = gpm_ref[...]
  gpf = gpf_ref[...]
  part = mixed_ref.shape[0] // OUT_PARTS
  for r0 in range(0, mixed_ref.shape[0], part):
    acc_ref[r0:r0 + part, :] = jnp.dot(
        mixed_ref[r0:r0 + part, :], wout_ref[...], preferred_element_type=F32)
    for r in range(r0, r0 + part, NORM_ROWS):
      x1 = x_ref[r:r + NORM_ROWS, :] + _rms_scale(acc_ref[r:r + NORM_ROWS, :], gpm)
      x1_ref[r:r + NORM_ROWS, :] = x1
      hn_ref[r:r + NORM_ROWS, :] = _rms_scale(x1, gpf).astype(BF16)


def _outproj_call(mixed, x, w_out, g_post_mix, g_pre_ffn):
  rows = x.shape[0]
  row_map = lambda i: (i, 0)
  const2 = lambda i: (0, 0)
  once = pl.Buffered(1)
  return pl.pallas_call(
      _outproj_kernel,
      grid=(rows // OUT_ROWS,),
      in_specs=[
          pl.BlockSpec((OUT_ROWS, D_MODEL), row_map),
          pl.BlockSpec((OUT_ROWS, D_MODEL), row_map),
          pl.BlockSpec((D_MODEL, D_MODEL), const2, pipeline_mode=once),
          pl.BlockSpec((1, D_MODEL), const2, pipeline_mode=once),
          pl.BlockSpec((1, D_MODEL), const2, pipeline_mode=once),
      ],
      out_specs=(pl.BlockSpec((OUT_ROWS, D_MODEL), row_map), pl.BlockSpec((OUT_ROWS, D_MODEL), row_map)),
      out_shape=(jax.ShapeDtypeStruct((rows, D_MODEL), F32), jax.ShapeDtypeStruct((rows, D_MODEL), BF16)),
      scratch_shapes=[pltpu.VMEM((OUT_ROWS, D_MODEL), F32)],
      compiler_params=pltpu.CompilerParams(
          dimension_semantics=("arbitrary",), vmem_limit_bytes=V7X_VMEM_LIMIT_BYTES),
      name="outproj",
  )(mixed, x, w_out, g_post_mix, g_pre_ffn)


def _ffn_tile(i, j):
  n_tiles = pl.cdiv(D_FF, FFN_TILE)
  step = jnp.where(i % 2 == 0, j, n_tiles - 1 - j)
  mid = n_tiles // 2 - 1
  return jnp.where(step < mid, step, jnp.where(step == mid, n_tiles - 1, step - 1))


def _ffn_kernel(x1_ref, hn_ref, gqf_ref, wg_ref, wu_ref, wd_ref, y_ref):
  j = pl.program_id(1)
  tile = _ffn_tile(pl.program_id(0), j)
  n_tiles = pl.cdiv(D_FF, FFN_TILE)
  rows = y_ref.shape[0]

  def swiglu_down(cols):
    hn = hn_ref[...]
    gate = jnp.dot(hn, wg_ref[:, 0:cols], preferred_element_type=F32)
    up = jnp.dot(hn, wu_ref[:, 0:cols], preferred_element_type=F32)
    mid = (gate * (1.0 / (1.0 + jnp.exp(-gate))) * up).astype(BF16)
    return jnp.dot(mid, wd_ref[0:cols, :], preferred_element_type=F32)

  assert n_tiles >= 4
  full = tile < n_tiles - 1

  @pl.when(full & (j == 0))
  def _():
    y_ref[...] = swiglu_down(FFN_TILE)

  @pl.when(full & (j > 0))
  def _():
    y_ref[...] += swiglu_down(FFN_TILE)

  @pl.when(jnp.logical_not(full))
  def _():
    y_ref[...] += swiglu_down(D_FF - (n_tiles - 1) * FFN_TILE)

  @pl.when(j == n_tiles - 1)
  def _():
    gqf = gqf_ref[...]
    for r in range(0, rows, NORM_ROWS):
      y_ref[r:r + NORM_ROWS, :] = x1_ref[r:r + NORM_ROWS, :] + _rms_scale(y_ref[r:r + NORM_ROWS, :], gqf)


def _ffn_call(x1, hn, g_post_ffn, w_gate, w_up, w_down):
  rows = x1.shape[0]
  row_map = lambda i, j: (i, 0)
  return pl.pallas_call(
      _ffn_kernel,
      grid=(rows // FFN_ROWS, pl.cdiv(D_FF, FFN_TILE)),
      in_specs=[
          pl.BlockSpec((FFN_ROWS, D_MODEL), row_map),
          pl.BlockSpec((FFN_ROWS, D_MODEL), row_map),
          pl.BlockSpec((1, D_MODEL), lambda i, j: (0, 0), pipeline_mode=pl.Buffered(1)),
          pl.BlockSpec((D_MODEL, FFN_TILE), lambda i, j: (0, _ffn_tile(i, j))),
          pl.BlockSpec((D_MODEL, FFN_TILE), lambda i, j: (0, _ffn_tile(i, j))),
          pl.BlockSpec((FFN_TILE, D_MODEL), lambda i, j: (_ffn_tile(i, j), 0)),
      ],
      out_specs=pl.BlockSpec((FFN_ROWS, D_MODEL), row_map),
      out_shape=jax.ShapeDtypeStruct((rows, D_MODEL), F32),
      compiler_params=pltpu.CompilerParams(
          dimension_semantics=("arbitrary", "arbitrary"),
          vmem_limit_bytes=V7X_VMEM_LIMIT_BYTES),
      name="ffn",
  )(x1, hn, g_post_ffn, w_gate, w_up, w_down)


def kernel(x_prompt, x_sample, state_conv, cache_k, cache_v, rel_table, g_pre_mix, w_in, conv_w,
           attn_sinks, w_out, g_post_mix, g_pre_ffn, w_gate, w_up, w_down, g_post_ffn):
  depth, n_dec, cache_len = cache_k.shape[:3]
  batch, seq_len, _ = x_prompt.shape
  assert depth == 1 and cache_len == WINDOW and x_sample.shape[1] == CHUNK
  assert seq_len % MIX_ROWS == 0 and MIX_ROWS % CHUNK == 0

  rel = (jnp.arange(BAND) - WINDOW)[None, :] - jnp.arange(CHUNK)[:, None]
  bkt = _t5_bucket(rel).astype(jnp.int32)

  row = lambda g: g[0].reshape(1, D_MODEL)
  win = w_in[0].astype(BF16)
  mix_args = (bkt, rel_table, attn_sinks[0], row(g_pre_mix), win, conv_w[0])
  mixed_p, conv_p, k_p, v_p = _mixer_call(x_prompt, None, None, None, *mix_args, has_state=False)
  mixed_s, conv_s, k_s, v_s = _mixer_call(
      x_sample, state_conv[0], cache_k[0].reshape(n_dec, WINDOW, KV_WIDTH),
      cache_v[0].reshape(n_dec, WINDOW, KV_WIDTH), *mix_args, has_state=True)

  out_args = (w_out[0].astype(BF16), row(g_post_mix), row(g_pre_ffn))
  ffn_args = (row(g_post_ffn), w_gate[0].astype(BF16), w_up[0].astype(BF16), w_down[0].astype(BF16))
  y_p = _ffn_call(*_outproj_call(mixed_p.reshape(-1, D_MODEL), x_prompt.reshape(-1, D_MODEL), *out_args), *ffn_args)
  y_s = _ffn_call(*_outproj_call(mixed_s.reshape(-1, D_MODEL), x_sample.reshape(-1, D_MODEL), *out_args), *ffn_args)

  heads = lambda a: a.reshape(1, a.shape[0], WINDOW, N_KV_HEADS, HEAD_DIM)
  return (y_p.reshape(x_prompt.shape), y_s.reshape(x_sample.shape),
          conv_p[None], heads(k_p), heads(v_p), conv_s[None], heads(k_s), heads(v_s))
```

```python
import functools
import math

import jax
import jax.numpy as jnp
import numpy as np
from jax import lax
from jax.experimental import pallas as pl
from jax.experimental.pallas import tpu as pltpu

D_MODEL = 2048
CHUNK = 64
HEAD_DIM = 64
CONV_WIDTH = D_MODEL // 2
CONV_K = 3
N_HEADS = (D_MODEL - CONV_WIDTH) // HEAD_DIM
N_KV_HEADS = 4
GROUP = N_HEADS // N_KV_HEADS
ATTN_WIDTH = N_HEADS * HEAD_DIM
KV_WIDTH = N_KV_HEADS * HEAD_DIM
IN_COLS = 3 * CONV_WIDTH + ATTN_WIDTH + 2 * KV_WIDTH
WINDOW = 128
BAND = WINDOW + CHUNK
NUM_BUCKETS = 32
MAX_DISTANCE = 128
D_FF = -(-8 * D_MODEL // (3 * 256)) * 256
EPS = 1e-6
NEG = -1e30

OFF_CB, OFF_CC, OFF_CU = 0, CONV_WIDTH, 2 * CONV_WIDTH
OFF_Q = 3 * CONV_WIDTH
OFF_KV = OFF_Q + ATTN_WIDTH

SUBLANES = 8
LANES = 128
V7X_VMEM_LIMIT_BYTES = 60 << 20
MIX_ROWS = 512
OUT_ROWS = 512
OUT_PARTS = 4
FFN_ROWS = 1024
FFN_TILE = 512
CONV_COLS = 512
MXU_COLS = 256
ITEMS_PER_STEP = 8
STEP_UNROLL = 1
NORM_ROWS = 16
NORM_UNROLL = 8
CONV_ROWS = 32
G_PAD = SUBLANES
N_BIAS_VARIANTS = WINDOW // CHUNK + 1

F32 = jnp.float32
BF16 = jnp.bfloat16


def _t5_bucket(rel):
  half = NUM_BUCKETS // 2
  ret = jnp.where(rel > 0, half, 0)
  n = jnp.abs(rel)
  max_exact = half // 2
  nf = jnp.maximum(n, 1).astype(jnp.float32)
  large = max_exact + (jnp.log(nf / max_exact) / math.log(MAX_DISTANCE / max_exact)
                       * (half - max_exact)).astype(jnp.int32)
  large = jnp.minimum(large, half - 1)
  return ret + jnp.where(n < max_exact, n, large)


def _rms_scale(y, gain):
  return y * lax.rsqrt(jnp.mean(y * y, axis=-1, keepdims=True) + EPS) * gain


def _mixer_kernel(tbl_ref, sink_ref, bkt_ref, x_ref, gpre_ref, win_ref, convw_ref, *rest,
                  n_seg, seg_len, has_state):
  if has_state:
    (state_ref, ck_ref, cv_ref, mixed_ref, nconv_ref, nk_ref, nv_ref,
     bias_ref, h_ref, zc_ref, q_ref, kvf_ref, kb_ref, vb_ref, gs_ref) = rest
  else:
    (mixed_ref, nconv_ref, nk_ref, nv_ref,
     bias_ref, h_ref, zc_ref, q_ref, kvf_ref, kb_ref, vb_ref, gs_ref) = rest
  rows = n_seg * seg_len
  g_stride = G_PAD + seg_len
  b_stride = WINDOW + seg_len
  first_call = (pl.program_id(0) == 0) & (pl.program_id(1) == 0)
  seq_start = pl.program_id(1) == 0

  @pl.when(first_call)
  def _():
    bkt = bkt_ref[...]
    key64 = lax.broadcasted_iota(jnp.int32, (CHUNK, BAND), 1)
    for h in range(N_HEADS):
      acc = jnp.zeros((CHUNK, BAND), F32)
      for j in range(NUM_BUCKETS):
        acc = jnp.where(bkt == j, tbl_ref[j, h], acc)
      g = h % GROUP
      for var in range(bias_ref.shape[0]):
        bias_ref[var, h // GROUP, g * CHUNK:(g + 1) * CHUNK, :] = jnp.where(key64 >= var * CHUNK, acc, NEG)

  if has_state:
    for s in range(n_seg):
      gs_ref[s * g_stride + G_PAD - (CONV_K - 1):s * g_stride + G_PAD, :] = state_ref[s]
      for kh in range(N_KV_HEADS):
        kb_ref[kh, s * b_stride:s * b_stride + WINDOW, :] = (
            ck_ref[s, :, kh * HEAD_DIM:(kh + 1) * HEAD_DIM].astype(BF16))
        vb_ref[kh, s * b_stride:s * b_stride + WINDOW, :] = (
            cv_ref[s, :, kh * HEAD_DIM:(kh + 1) * HEAD_DIM].astype(BF16))
  else:
    @pl.when(seq_start)
    def _():
      gs_ref[0:G_PAD, :] = jnp.zeros((G_PAD, CONV_WIDTH), F32)
      kb_ref[:, 0:WINDOW, :] = jnp.zeros((N_KV_HEADS, WINDOW, HEAD_DIM), BF16)
      vb_ref[:, 0:WINDOW, :] = jnp.zeros((N_KV_HEADS, WINDOW, HEAD_DIM), BF16)

    @pl.when(jnp.logical_not(seq_start))
    def _():
      gs_ref[0:G_PAD, :] = gs_ref[seg_len:seg_len + G_PAD, :]
      kb_ref[:, 0:WINDOW, :] = kb_ref[:, seg_len:seg_len + WINDOW, :]
      vb_ref[:, 0:WINDOW, :] = vb_ref[:, seg_len:seg_len + WINDOW, :]

  gpre = gpre_ref[...]

  def norm_body(i, carry):
    r = pl.multiple_of(i * NORM_ROWS, NORM_ROWS)
    h_ref[pl.ds(r, NORM_ROWS), :] = _rms_scale(x_ref[pl.ds(r, NORM_ROWS), :], gpre).astype(BF16)
    return carry

  lax.fori_loop(0, rows // NORM_ROWS, norm_body, 0, unroll=NORM_UNROLL)

  for c0 in range(0, ATTN_WIDTH, CONV_COLS):
    zc_ref[:, 0:CONV_COLS] = jnp.dot(
        h_ref[...], win_ref[:, OFF_Q + c0:OFF_Q + c0 + CONV_COLS], preferred_element_type=F32)
    for hh in range(CONV_COLS // HEAD_DIM):
      q_ref[c0 // HEAD_DIM + hh] = (
          zc_ref[:, hh * HEAD_DIM:(hh + 1) * HEAD_DIM] * (HEAD_DIM ** -0.5)).astype(BF16)

  kvf_ref[...] = jnp.dot(h_ref[...], win_ref[:, OFF_KV:OFF_KV + 2 * KV_WIDTH], preferred_element_type=F32)
  for s in range(n_seg):
    for kh in range(N_KV_HEADS):
      kb_ref[kh, s * b_stride + WINDOW:(s + 1) * b_stride, :] = (
          kvf_ref[s * seg_len:(s + 1) * seg_len, kh * HEAD_DIM:(kh + 1) * HEAD_DIM].astype(BF16))
      vb_ref[kh, s * b_stride + WINDOW:(s + 1) * b_stride, :] = (
          kvf_ref[s * seg_len:(s + 1) * seg_len,
                  KV_WIDTH + kh * HEAD_DIM:KV_WIDTH + (kh + 1) * HEAD_DIM].astype(BF16))

  for s in range(n_seg):
    if seg_len >= WINDOW:
      nk_ref[s] = kvf_ref[(s + 1) * seg_len - WINDOW:(s + 1) * seg_len, 0:KV_WIDTH]
      nv_ref[s] = kvf_ref[(s + 1) * seg_len - WINDOW:(s + 1) * seg_len, KV_WIDTH:2 * KV_WIDTH]
    else:
      nk_ref[s, 0:WINDOW - seg_len, :] = ck_ref[s, seg_len:WINDOW, :]
      nv_ref[s, 0:WINDOW - seg_len, :] = cv_ref[s, seg_len:WINDOW, :]
      nk_ref[s, WINDOW - seg_len:WINDOW, :] = kvf_ref[s * seg_len:(s + 1) * seg_len, 0:KV_WIDTH]
      nv_ref[s, WINDOW - seg_len:WINDOW, :] = kvf_ref[s * seg_len:(s + 1) * seg_len, KV_WIDTH:2 * KV_WIDTH]

  row_id = lax.broadcasted_iota(jnp.int32, (GROUP * CHUNK, 1), 0)
  sink_cols = []
  for kh in range(N_KV_HEADS):
    col = jnp.full((GROUP * CHUNK, 1), sink_ref[kh * GROUP + GROUP - 1], F32)
    for g in range(GROUP - 2, -1, -1):
      col = jnp.where(row_id < (g + 1) * CHUNK, sink_ref[kh * GROUP + g], col)
    sink_cols.append(col)
  n_items = rows // CHUNK
  assert n_seg == 1 or seg_len == CHUNK
  band_step = CHUNK if n_seg == 1 else b_stride

  def attend(item):
    q0 = pl.multiple_of(item * CHUNK, CHUNK)
    b0 = pl.multiple_of(item * band_step, CHUNK)
    if has_state:
      var = 0
    else:
      var = jnp.where(seq_start, jnp.maximum(WINDOW // CHUNK - item, 0), 0)
    for kh in range(N_KV_HEADS):
      qs = jnp.concatenate(
          [q_ref[kh * GROUP + g, pl.ds(q0, CHUNK), :] for g in range(GROUP)], axis=0)
      kband = kb_ref[kh, pl.ds(b0, BAND), :]
      vband = vb_ref[kh, pl.ds(b0, BAND), :]
      s = lax.dot_general(qs, kband, (((1,), (1,)), ((), ())), preferred_element_type=F32)
      s = s + bias_ref[var, kh]
      sink = sink_cols[kh]
      m = jnp.maximum(jnp.max(s, axis=-1, keepdims=True), sink)
      p = jnp.exp(s - m)
      denom = jnp.sum(p, axis=-1, keepdims=True) + jnp.exp(sink - m)
      o = jnp.dot(p.astype(BF16), vband, preferred_element_type=F32) * (1.0 / denom)
      o = jnp.concatenate([o[g * CHUNK:(g + 1) * CHUNK, :] for g in range(GROUP)], axis=1)
      mixed_ref[pl.ds(q0, CHUNK), CONV_WIDTH + kh * GROUP * HEAD_DIM:
                CONV_WIDTH + (kh + 1) * GROUP * HEAD_DIM] = o.astype(BF16)

  n_steps = n_items // ITEMS_PER_STEP
  piece = 3 * CONV_WIDTH // n_steps
  assert n_steps * ITEMS_PER_STEP == n_items and piece * n_steps == 3 * CONV_WIDTH
  assert piece % CONV_COLS == 0 and piece // CONV_COLS <= ITEMS_PER_STEP

  def step_body(j, carry):
    for u in range(ITEMS_PER_STEP):
      if u < piece // CONV_COLS:
        c0 = pl.multiple_of(j * piece + u * CONV_COLS, MXU_COLS)
        zc_ref[:, pl.ds(c0, CONV_COLS)] = jnp.dot(
            h_ref[...], win_ref[:, pl.ds(c0, CONV_COLS)], preferred_element_type=F32)
      attend(j * ITEMS_PER_STEP + u)
    return carry

  lax.fori_loop(0, n_steps, step_body, 0, unroll=STEP_UNROLL)

  for c0 in range(0, CONV_WIDTH, CONV_COLS):
    cols = slice(c0, c0 + CONV_COLS)
    cb_cols, cc_cols, cu_cols = (slice(off + c0, off + c0 + CONV_COLS) for off in (OFF_CB, OFF_CC, OFF_CU))
    for s in range(n_seg):
      for r in range(0, seg_len, CONV_ROWS):
        zr = s * seg_len + r
        gr = s * g_stride + G_PAD + r
        gs_ref[gr:gr + CONV_ROWS, cols] = (
            zc_ref[zr:zr + CONV_ROWS, cc_cols] * zc_ref[zr:zr + CONV_ROWS, cu_cols])
    for s in range(n_seg):
      for r in range(0, seg_len, CONV_ROWS):
        zr = s * seg_len + r
        gr = s * g_stride + G_PAD + r
        conv = convw_ref[0:1, cols] * gs_ref[gr - 2:gr - 2 + CONV_ROWS, cols]
        conv = conv + convw_ref[1:2, cols] * gs_ref[gr - 1:gr - 1 + CONV_ROWS, cols]
        conv = conv + convw_ref[2:3, cols] * gs_ref[gr:gr + CONV_ROWS, cols]
        mixed_ref[zr:zr + CONV_ROWS, cols] = (zc_ref[zr:zr + CONV_ROWS, cb_cols] * conv).astype(BF16)

  for s in range(n_seg):
    g_end = s * g_stride + G_PAD + seg_len
    nconv_ref[s] = gs_ref[g_end - (CONV_K - 1):g_end, :]


def _mixer_call(x, state, cache_k, cache_v, bkt, rel_table, sinks, g_pre, w_in, conv_w, *, has_state):
  n_seq, seq_len, _ = x.shape
  if has_state:
    n_seg, seg_len = MIX_ROWS // seq_len, seq_len
    grid = (n_seq // n_seg, 1)
    x = x.reshape(n_seq // n_seg, MIX_ROWS, D_MODEL)
    x_map = lambda b, s: (b, 0, 0)
    seq_map = lambda b, s: (b, 0, 0)
  else:
    n_seg, seg_len = 1, MIX_ROWS
    grid = (n_seq, seq_len // MIX_ROWS)
    x_map = lambda b, s: (b, s, 0)
    seq_map = lambda b, s: (b, 0, 0)
  const2 = lambda b, s: (0, 0)
  once = pl.Buffered(1)

  smem = pl.BlockSpec(memory_space=pltpu.SMEM)
  in_specs = [
      smem, smem,
      pl.BlockSpec((CHUNK, BAND), const2, pipeline_mode=once),
      pl.BlockSpec((None, MIX_ROWS, D_MODEL), x_map, pipeline_mode=once if has_state else None),
      pl.BlockSpec((1, D_MODEL), const2, pipeline_mode=once),
      pl.BlockSpec((D_MODEL, IN_COLS), const2, pipeline_mode=once),
      pl.BlockSpec((CONV_K, CONV_WIDTH), const2, pipeline_mode=once),
  ]
  args = [rel_table, sinks, bkt, x, g_pre, w_in, conv_w]
  if has_state:
    in_specs += [
        pl.BlockSpec((n_seg, CONV_K - 1, CONV_WIDTH), seq_map),
        pl.BlockSpec((n_seg, WINDOW, KV_WIDTH), seq_map, pipeline_mode=once),
        pl.BlockSpec((n_seg, WINDOW, KV_WIDTH), seq_map, pipeline_mode=once),
    ]
    args += [state, cache_k, cache_v]
  out_shape = (
      jax.ShapeDtypeStruct(x.shape, BF16),
      jax.ShapeDtypeStruct((n_seq, CONV_K - 1, CONV_WIDTH), F32),
      jax.ShapeDtypeStruct((n_seq, WINDOW, KV_WIDTH), F32),
      jax.ShapeDtypeStruct((n_seq, WINDOW, KV_WIDTH), F32),
  )
  out_specs = (
      pl.BlockSpec((None, MIX_ROWS, D_MODEL), x_map),
      pl.BlockSpec((n_seg, CONV_K - 1, CONV_WIDTH), seq_map),
      pl.BlockSpec((n_seg, WINDOW, KV_WIDTH), seq_map),
      pl.BlockSpec((n_seg, WINDOW, KV_WIDTH), seq_map),
  )
  band_rows = n_seg * (WINDOW + seg_len)
  scratch = [
      pltpu.VMEM((1 if has_state else N_BIAS_VARIANTS, N_KV_HEADS, GROUP * CHUNK, BAND), F32),
      pltpu.VMEM((MIX_ROWS, D_MODEL), BF16),
      pltpu.VMEM((MIX_ROWS, 3 * CONV_WIDTH), F32),
      pltpu.VMEM((N_HEADS, MIX_ROWS, HEAD_DIM), BF16),
      pltpu.VMEM((MIX_ROWS, 2 * KV_WIDTH), F32),
      pltpu.VMEM((N_KV_HEADS, band_rows, HEAD_DIM), BF16),
      pltpu.VMEM((N_KV_HEADS, band_rows, HEAD_DIM), BF16),
      pltpu.VMEM((n_seg * (G_PAD + seg_len), CONV_WIDTH), F32),
  ]
  kernel = functools.partial(_mixer_kernel, n_seg=n_seg, seg_len=seg_len, has_state=has_state)
  return pl.pallas_call(
      kernel,
      grid=grid,
      in_specs=in_specs,
      out_specs=out_specs,
      out_shape=out_shape,
      scratch_shapes=scratch,
      compiler_params=pltpu.CompilerParams(
          dimension_semantics=("arbitrary", "arbitrary"),
          vmem_limit_bytes=V7X_VMEM_LIMIT_BYTES),
      name="mixer_state" if has_state else "mixer_stream",
  )(*args)


def _outproj_kernel(mixed_ref, x_ref, wout_ref, gpm_ref, gpf_ref, x1_ref, hn_ref, acc_ref):
  gpm = gpm_ref[...]
  gpf = gpf_ref[...]
  part = mixed_ref.shape[0] // OUT_PARTS
  for r0 in range(0, mixed_ref.shape[0], part):
    acc_ref[r0:r0 + part, :] = jnp.dot(
        mixed_ref[r0:r0 + part, :], wout_ref[...], preferred_element_type=F32)
    for r in range(r0, r0 + part, NORM_ROWS):
      x1 = x_ref[r:r + NORM_ROWS, :] + _rms_scale(acc_ref[r:r + NORM_ROWS, :], gpm)
      x1_ref[r:r + NORM_ROWS, :] = x1
      hn_ref[r:r + NORM_ROWS, :] = _rms_scale(x1, gpf).astype(BF16)


def _outproj_call(mixed, x, w_out, g_post_mix, g_pre_ffn):
  rows = x.shape[0]
  row_map = lambda i: (i, 0)
  const2 = lambda i: (0, 0)
  once = pl.Buffered(1)
  return pl.pallas_call(
      _outproj_kernel,
      grid=(rows // OUT_ROWS,),
      in_specs=[
          pl.BlockSpec((OUT_ROWS, D_MODEL), row_map),
          pl.BlockSpec((OUT_ROWS, D_MODEL), row_map),
          pl.BlockSpec((D_MODEL, D_MODEL), const2, pipeline_mode=once),
          pl.BlockSpec((1, D_MODEL), const2, pipeline_mode=once),
          pl.BlockSpec((1, D_MODEL), const2, pipeline_mode=once),
      ],
      out_specs=(pl.BlockSpec((OUT_ROWS, D_MODEL), row_map), pl.BlockSpec((OUT_ROWS, D_MODEL), row_map)),
      out_shape=(jax.ShapeDtypeStruct((rows, D_MODEL), F32), jax.ShapeDtypeStruct((rows, D_MODEL), BF16)),
      scratch_shapes=[pltpu.VMEM((OUT_ROWS, D_MODEL), F32)],
      compiler_params=pltpu.CompilerParams(
          dimension_semantics=("arbitrary",), vmem_limit_bytes=V7X_VMEM_LIMIT_BYTES),
      name="outproj",
  )(mixed, x, w_out, g_post_mix, g_pre_ffn)


def _ffn_tile(i, j):
  n_tiles = pl.cdiv(D_FF, FFN_TILE)
  step = jnp.where(i % 2 == 0, j, n_tiles - 1 - j)
  mid = n_tiles // 2 - 1
  return jnp.where(step < mid, step, jnp.where(step == mid, n_tiles - 1, step - 1))


def _ffn_kernel(x1_ref, hn_ref, gqf_ref, wg_ref, wu_ref, wd_ref, y_ref):
  j = pl.program_id(1)
  tile = _ffn_tile(pl.program_id(0), j)
  n_tiles = pl.cdiv(D_FF, FFN_TILE)
  rows = y_ref.shape[0]

  def swiglu_down(cols):
    hn = hn_ref[...]
    gate = jnp.dot(hn, wg_ref[:, 0:cols], preferred_element_type=F32)
    up = jnp.dot(hn, wu_ref[:, 0:cols], preferred_element_type=F32)
    mid = (gate * (1.0 / (1.0 + jnp.exp(-gate))) * up).astype(BF16)
    return jnp.dot(mid, wd_ref[0:cols, :], preferred_element_type=F32)

  assert n_tiles >= 4
  full = tile < n_tiles - 1

  @pl.when(full & (j == 0))
  def _():
    y_ref[...] = swiglu_down(FFN_TILE)

  @pl.when(full & (j > 0))
  def _():
    y_ref[...] += swiglu_down(FFN_TILE)

  @pl.when(jnp.logical_not(full))
  def _():
    y_ref[...] += swiglu_down(D_FF - (n_tiles - 1) * FFN_TILE)

  @pl.when(j == n_tiles - 1)
  def _():
    gqf = gqf_ref[...]
    for r in range(0, rows, NORM_ROWS):
      y_ref[r:r + NORM_ROWS, :] = x1_ref[r:r + NORM_ROWS, :] + _rms_scale(y_ref[r:r + NORM_ROWS, :], gqf)


def _ffn_call(x1, hn, g_post_ffn, w_gate, w_up, w_down):
  rows = x1.shape[0]
  row_map = lambda i, j: (i, 0)
  return pl.pallas_call(
      _ffn_kernel,
      grid=(rows // FFN_ROWS, pl.cdiv(D_FF, FFN_TILE)),
      in_specs=[
          pl.BlockSpec((FFN_ROWS, D_MODEL), row_map),
          pl.BlockSpec((FFN_ROWS, D_MODEL), row_map),
          pl.BlockSpec((1, D_MODEL), lambda i, j: (0, 0), pipeline_mode=pl.Buffered(1)),
          pl.BlockSpec((D_MODEL, FFN_TILE), lambda i, j: (0, _ffn_tile(i, j))),
          pl.BlockSpec((D_MODEL, FFN_TILE), lambda i, j: (0, _ffn_tile(i, j))),
          pl.BlockSpec((FFN_TILE, D_MODEL), lambda i, j: (_ffn_tile(i, j), 0)),
      ],
      out_specs=pl.BlockSpec((FFN_ROWS, D_MODEL), row_map),
      out_shape=jax.ShapeDtypeStruct((rows, D_MODEL), F32),
      compiler_params=pltpu.CompilerParams(
          dimension_semantics=("arbitrary", "arbitrary"),
          vmem_limit_bytes=V7X_VMEM_LIMIT_BYTES),
      name="ffn",
  )(x1, hn, g_post_ffn, w_gate, w_up, w_down)


def kernel(x_prompt, x_sample, state_conv, cache_k, cache_v, rel_table, g_pre_mix, w_in, conv_w,
           attn_sinks, w_out, g_post_mix, g_pre_ffn, w_gate, w_up, w_down, g_post_ffn):
  depth, n_dec, cache_len = cache_k.shape[:3]
  batch, seq_len, _ = x_prompt.shape
  assert depth == 1 and cache_len == WINDOW and x_sample.shape[1] == CHUNK
  assert seq_len % MIX_ROWS == 0 and MIX_ROWS % CHUNK == 0

  rel = (jnp.arange(BAND) - WINDOW)[None, :] - jnp.arange(CHUNK)[:, None]
  bkt = _t5_bucket(rel).astype(jnp.int32)

  row = lambda g: g[0].reshape(1, D_MODEL)
  win = w_in[0].astype(BF16)
  mix_args = (bkt, rel_table, attn_sinks[0], row(g_pre_mix), win, conv_w[0])
  mixed_p, conv_p, k_p, v_p = _mixer_call(x_prompt, None, None, None, *mix_args, has_state=False)
  mixed_s, conv_s, k_s, v_s = _mixer_call(
      x_sample, state_conv[0], cache_k[0].reshape(n_dec, WINDOW, KV_WIDTH),
      cache_v[0].reshape(n_dec, WINDOW, KV_WIDTH), *mix_args, has_state=True)

  out_args = (w_out[0].astype(BF16), row(g_post_mix), row(g_pre_ffn))
  ffn_args = (row(g_post_ffn), w_gate[0].astype(BF16), w_up[0].astype(BF16), w_down[0].astype(BF16))
  y_p = _ffn_call(*_outproj_call(mixed_p.reshape(-1, D_MODEL), x_prompt.reshape(-1, D_MODEL), *out_args), *ffn_args)
  y_s = _ffn_call(*_outproj_call(mixed_s.reshape(-1, D_MODEL), x_sample.reshape(-1, D_MODEL), *out_args), *ffn_args)

  heads = lambda a: a.reshape(1, a.shape[0], WINDOW, N_KV_HEADS, HEAD_DIM)
  return (y_p.reshape(x_prompt.shape), y_s.reshape(x_sample.shape),
          conv_p[None], heads(k_p), heads(v_p), conv_s[None], heads(k_s), heads(v_s))
```

```python
import functools
import math

import jax
import jax.numpy as jnp
import numpy as np
from jax import lax
from jax.experimental import pallas as pl
from jax.experimental.pallas import tpu as pltpu

D_MODEL = 2048
CHUNK = 64
HEAD_DIM = 64
CONV_WIDTH = D_MODEL // 2
CONV_K = 3
N_HEADS = (D_MODEL - CONV_WIDTH) // HEAD_DIM
N_KV_HEADS = 4
GROUP = N_HEADS // N_KV_HEADS
ATTN_WIDTH = N_HEADS * HEAD_DIM
KV_WIDTH = N_KV_HEADS * HEAD_DIM
IN_COLS = 3 * CONV_WIDTH + ATTN_WIDTH + 2 * KV_WIDTH
WINDOW = 128
BAND = WINDOW + CHUNK
NUM_BUCKETS = 32
MAX_DISTANCE = 128
D_FF = -(-8 * D_MODEL // (3 * 256)) * 256
EPS = 1e-6
NEG = -1e30

OFF_CB, OFF_CC, OFF_CU = 0, CONV_WIDTH, 2 * CONV_WIDTH
OFF_Q = 3 * CONV_WIDTH
OFF_KV = OFF_Q + ATTN_WIDTH

SUBLANES = 8
BF16_ROWS = 16
LANES = 128
V7X_VMEM_LIMIT_BYTES = 60 << 20
MIX_ROWS = 512
OUT_ROWS = 512
OUT_PARTS = 4
FFN_ROWS = 1024
FFN_TILE = 512
CONV_COLS = 512
MXU_COLS = 256
ITEMS_PER_STEP = 8
STEP_UNROLL = 1
NORM_ROWS = 16
NORM_UNROLL = 8
CONV_ROWS = 32
G_PAD = SUBLANES
N_BIAS_VARIANTS = WINDOW // CHUNK + 1

F32 = jnp.float32
BF16 = jnp.bfloat16


def _t5_bucket(rel):
  half = NUM_BUCKETS // 2
  ret = jnp.where(rel > 0, half, 0)
  n = jnp.abs(rel)
  max_exact = half // 2
  nf = jnp.maximum(n, 1).astype(jnp.float32)
  large = max_exact + (jnp.log(nf / max_exact) / math.log(MAX_DISTANCE / max_exact)
                       * (half - max_exact)).astype(jnp.int32)
  large = jnp.minimum(large, half - 1)
  return ret + jnp.where(n < max_exact, n, large)


def _rms_scale(y, gain):
  return y * lax.rsqrt(jnp.mean(y * y, axis=-1, keepdims=True) + EPS) * gain


def _mixer_kernel(tbl_ref, sink_ref, bkt_ref, x_ref, gpre_ref, win_ref, convw_ref, *rest,
                  n_seg, seg_len, has_state, n_cast):
  rest = list(rest)
  if has_state:
    state_ref, ck_ref, cv_ref = rest[:3]
    rest = rest[3:]
  cast_src, rest = rest[:n_cast], rest[n_cast:]
  (mixed_ref, nconv_ref, nk_ref, nv_ref), rest = rest[:4], rest[4:]
  cast_dst, rest = rest[:n_cast], rest[n_cast:]
  bias_ref, h_ref, zc_ref, q_ref, kvf_ref, kb_ref, vb_ref, gs_ref = rest
  rows = n_seg * seg_len
  g_stride = G_PAD + seg_len
  b_stride = WINDOW + seg_len
  first_call = (pl.program_id(0) == 0) & (pl.program_id(1) == 0)
  seq_start = pl.program_id(1) == 0

  @pl.when(first_call)
  def _():
    bkt = bkt_ref[...]
    key64 = lax.broadcasted_iota(jnp.int32, (CHUNK, BAND), 1)
    for h in range(N_HEADS):
      acc = jnp.zeros((CHUNK, BAND), F32)
      for j in range(NUM_BUCKETS):
        acc = jnp.where(bkt == j, tbl_ref[j, h], acc)
      g = h % GROUP
      for var in range(bias_ref.shape[0]):
        bias_ref[var, h // GROUP, g * CHUNK:(g + 1) * CHUNK, :] = jnp.where(key64 >= var * CHUNK, acc, NEG)

  if has_state:
    for s in range(n_seg):
      gs_ref[s * g_stride + G_PAD - (CONV_K - 1):s * g_stride + G_PAD, :] = state_ref[s]
      for kh in range(N_KV_HEADS):
        kb_ref[kh, s * b_stride:s * b_stride + WINDOW, :] = (
            ck_ref[s, :, kh * HEAD_DIM:(kh + 1) * HEAD_DIM].astype(BF16))
        vb_ref[kh, s * b_stride:s * b_stride + WINDOW, :] = (
            cv_ref[s, :, kh * HEAD_DIM:(kh + 1) * HEAD_DIM].astype(BF16))
  else:
    @pl.when(seq_start)
    def _():
      gs_ref[0:G_PAD, :] = jnp.zeros((G_PAD, CONV_WIDTH), F32)
      kb_ref[:, 0:WINDOW, :] = jnp.zeros((N_KV_HEADS, WINDOW, HEAD_DIM), BF16)
      vb_ref[:, 0:WINDOW, :] = jnp.zeros((N_KV_HEADS, WINDOW, HEAD_DIM), BF16)

    @pl.when(jnp.logical_not(seq_start))
    def _():
      gs_ref[0:G_PAD, :] = gs_ref[seg_len:seg_len + G_PAD, :]
      kb_ref[:, 0:WINDOW, :] = kb_ref[:, seg_len:seg_len + WINDOW, :]
      vb_ref[:, 0:WINDOW, :] = vb_ref[:, seg_len:seg_len + WINDOW, :]

  gpre = gpre_ref[...]

  def norm_body(i, carry):
    r = pl.multiple_of(i * NORM_ROWS, NORM_ROWS)
    h_ref[pl.ds(r, NORM_ROWS), :] = _rms_scale(x_ref[pl.ds(r, NORM_ROWS), :], gpre).astype(BF16)
    return carry

  lax.fori_loop(0, rows // NORM_ROWS, norm_body, 0, unroll=NORM_UNROLL)

  for src, dst in zip(cast_src, cast_dst):
    dst[...] = src[...].astype(BF16)

  for c0 in range(0, ATTN_WIDTH, CONV_COLS):
    zc_ref[:, 0:CONV_COLS] = jnp.dot(
        h_ref[...], win_ref[:, OFF_Q + c0:OFF_Q + c0 + CONV_COLS], preferred_element_type=F32)
    for hh in range(CONV_COLS // HEAD_DIM):
      q_ref[c0 // HEAD_DIM + hh] = (
          zc_ref[:, hh * HEAD_DIM:(hh + 1) * HEAD_DIM] * (HEAD_DIM ** -0.5)).astype(BF16)

  kvf_ref[...] = jnp.dot(h_ref[...], win_ref[:, OFF_KV:OFF_KV + 2 * KV_WIDTH], preferred_element_type=F32)
  for s in range(n_seg):
    for kh in range(N_KV_HEADS):
      kb_ref[kh, s * b_stride + WINDOW:(s + 1) * b_stride, :] = (
          kvf_ref[s * seg_len:(s + 1) * seg_len, kh * HEAD_DIM:(kh + 1) * HEAD_DIM].astype(BF16))
      vb_ref[kh, s * b_stride + WINDOW:(s + 1) * b_stride, :] = (
          kvf_ref[s * seg_len:(s + 1) * seg_len,
                  KV_WIDTH + kh * HEAD_DIM:KV_WIDTH + (kh + 1) * HEAD_DIM].astype(BF16))

  for s in range(n_seg):
    if seg_len >= WINDOW:
      nk_ref[s] = kvf_ref[(s + 1) * seg_len - WINDOW:(s + 1) * seg_len, 0:KV_WIDTH]
      nv_ref[s] = kvf_ref[(s + 1) * seg_len - WINDOW:(s + 1) * seg_len, KV_WIDTH:2 * KV_WIDTH]
    else:
      nk_ref[s, 0:WINDOW - seg_len, :] = ck_ref[s, seg_len:WINDOW, :]
      nv_ref[s, 0:WINDOW - seg_len, :] = cv_ref[s, seg_len:WINDOW, :]
      nk_ref[s, WINDOW - seg_len:WINDOW, :] = kvf_ref[s * seg_len:(s + 1) * seg_len, 0:KV_WIDTH]
      nv_ref[s, WINDOW - seg_len:WINDOW, :] = kvf_ref[s * seg_len:(s + 1) * seg_len, KV_WIDTH:2 * KV_WIDTH]

  row_id = lax.broadcasted_iota(jnp.int32, (GROUP * CHUNK, 1), 0)
  sink_cols = []
  for kh in range(N_KV_HEADS):
    col = jnp.full((GROUP * CHUNK, 1), sink_ref[kh * GROUP + GROUP - 1], F32)
    for g in range(GROUP - 2, -1, -1):
      col = jnp.where(row_id < (g + 1) * CHUNK, sink_ref[kh * GROUP + g], col)
    sink_cols.append(col)
  n_items = rows // CHUNK
  assert n_seg == 1 or seg_len == CHUNK
  band_step = CHUNK if n_seg == 1 else b_stride

  def attend(item):
    q0 = pl.multiple_of(item * CHUNK, CHUNK)
    b0 = pl.multiple_of(item * band_step, CHUNK)
    if has_state:
      var = 0
    else:
      var = jnp.where(seq_start, jnp.maximum(WINDOW // CHUNK - item, 0), 0)
    for kh in range(N_KV_HEADS):
      qs = jnp.concatenate(
          [q_ref[kh * GROUP + g, pl.ds(q0, CHUNK), :] for g in range(GROUP)], axis=0)
      kband = kb_ref[kh, pl.ds(b0, BAND), :]
      vband = vb_ref[kh, pl.ds(b0, BAND), :]
      s = lax.dot_general(qs, kband, (((1,), (1,)), ((), ())), preferred_element_type=F32)
      s = s + bias_ref[var, kh]
      sink = sink_cols[kh]
      m = jnp.maximum(jnp.max(s, axis=-1, keepdims=True), sink)
      p = jnp.exp(s - m)
      denom = jnp.sum(p, axis=-1, keepdims=True) + jnp.exp(sink - m)
      o = jnp.dot(p.astype(BF16), vband, preferred_element_type=F32) * (1.0 / denom)
      o = jnp.concatenate([o[g * CHUNK:(g + 1) * CHUNK, :] for g in range(GROUP)], axis=1)
      mixed_ref[pl.ds(q0, CHUNK), CONV_WIDTH + kh * GROUP * HEAD_DIM:
                CONV_WIDTH + (kh + 1) * GROUP * HEAD_DIM] = o.astype(BF16)

  n_steps = n_items // ITEMS_PER_STEP
  piece = 3 * CONV_WIDTH // n_steps
  assert n_steps * ITEMS_PER_STEP == n_items and piece * n_steps == 3 * CONV_WIDTH
  assert piece % CONV_COLS == 0 and piece // CONV_COLS <= ITEMS_PER_STEP

  def step_body(j, carry):
    for u in range(ITEMS_PER_STEP):
      if u < piece // CONV_COLS:
        c0 = pl.multiple_of(j * piece + u * CONV_COLS, MXU_COLS)
        zc_ref[:, pl.ds(c0, CONV_COLS)] = jnp.dot(
            h_ref[...], win_ref[:, pl.ds(c0, CONV_COLS)], preferred_element_type=F32)
      attend(j * ITEMS_PER_STEP + u)
    return carry

  lax.fori_loop(0, n_steps, step_body, 0, unroll=STEP_UNROLL)

  for c0 in range(0, CONV_WIDTH, CONV_COLS):
    cols = slice(c0, c0 + CONV_COLS)
    cb_cols, cc_cols, cu_cols = (slice(off + c0, off + c0 + CONV_COLS) for off in (OFF_CB, OFF_CC, OFF_CU))
    for s in range(n_seg):
      for r in range(0, seg_len, CONV_ROWS):
        zr = s * seg_len + r
        gr = s * g_stride + G_PAD + r
        gs_ref[gr:gr + CONV_ROWS, cols] = (
            zc_ref[zr:zr + CONV_ROWS, cc_cols] * zc_ref[zr:zr + CONV_ROWS, cu_cols])
    for s in range(n_seg):
      for r in range(0, seg_len, CONV_ROWS):
        zr = s * seg_len + r
        gr = s * g_stride + G_PAD + r
        conv = convw_ref[0:1, cols] * gs_ref[gr - 2:gr - 2 + CONV_ROWS, cols]
        conv = conv + convw_ref[1:2, cols] * gs_ref[gr - 1:gr - 1 + CONV_ROWS, cols]
        conv = conv + convw_ref[2:3, cols] * gs_ref[gr:gr + CONV_ROWS, cols]
        mixed_ref[zr:zr + CONV_ROWS, cols] = (zc_ref[zr:zr + CONV_ROWS, cb_cols] * conv).astype(BF16)

  for s in range(n_seg):
    g_end = s * g_stride + G_PAD + seg_len
    nconv_ref[s] = gs_ref[g_end - (CONV_K - 1):g_end, :]


def _cast_specs(weights, n_steps, step_of):
  specs, shapes = [], []
  for w in weights:
    assert w.shape[0] % (n_steps * BF16_ROWS) == 0
    specs.append(pl.BlockSpec((w.shape[0] // n_steps, w.shape[1]), lambda *ids: (step_of(*ids), 0)))
    shapes.append(jax.ShapeDtypeStruct(w.shape, BF16))
  return specs, shapes


def _mixer_call(x, state, cache_k, cache_v, bkt, rel_table, sinks, g_pre, w_in, conv_w, *, has_state,
                cast_weights=()):
  n_seq, seq_len, _ = x.shape
  if has_state:
    n_seg, seg_len = MIX_ROWS // seq_len, seq_len
    grid = (n_seq // n_seg, 1)
    x = x.reshape(n_seq // n_seg, MIX_ROWS, D_MODEL)
    x_map = lambda b, s: (b, 0, 0)
    seq_map = lambda b, s: (b, 0, 0)
  else:
    n_seg, seg_len = 1, MIX_ROWS
    grid = (n_seq, seq_len // MIX_ROWS)
    x_map = lambda b, s: (b, s, 0)
    seq_map = lambda b, s: (b, 0, 0)
  const2 = lambda b, s: (0, 0)
  once = pl.Buffered(1)

  smem = pl.BlockSpec(memory_space=pltpu.SMEM)
  in_specs = [
      smem, smem,
      pl.BlockSpec((CHUNK, BAND), const2, pipeline_mode=once),
      pl.BlockSpec((None, MIX_ROWS, D_MODEL), x_map, pipeline_mode=once if has_state else None),
      pl.BlockSpec((1, D_MODEL), const2, pipeline_mode=once),
      pl.BlockSpec((D_MODEL, IN_COLS), const2, pipeline_mode=once),
      pl.BlockSpec((CONV_K, CONV_WIDTH), const2, pipeline_mode=once),
  ]
  args = [rel_table, sinks, bkt, x, g_pre, w_in, conv_w]
  if has_state:
    in_specs += [
        pl.BlockSpec((n_seg, CONV_K - 1, CONV_WIDTH), seq_map),
        pl.BlockSpec((n_seg, WINDOW, KV_WIDTH), seq_map, pipeline_mode=once),
        pl.BlockSpec((n_seg, WINDOW, KV_WIDTH), seq_map, pipeline_mode=once),
    ]
    args += [state, cache_k, cache_v]
  cast_specs, cast_shapes = _cast_specs(cast_weights, grid[0] * grid[1], lambda b, s: b * grid[1] + s)
  in_specs += cast_specs
  args += list(cast_weights)
  out_shape = (
      jax.ShapeDtypeStruct(x.shape, BF16),
      jax.ShapeDtypeStruct((n_seq, CONV_K - 1, CONV_WIDTH), F32),
      jax.ShapeDtypeStruct((n_seq, WINDOW, KV_WIDTH), F32),
      jax.ShapeDtypeStruct((n_seq, WINDOW, KV_WIDTH), F32),
      *cast_shapes,
  )
  out_specs = (
      pl.BlockSpec((None, MIX_ROWS, D_MODEL), x_map),
      pl.BlockSpec((n_seg, CONV_K - 1, CONV_WIDTH), seq_map),
      pl.BlockSpec((n_seg, WINDOW, KV_WIDTH), seq_map),
      pl.BlockSpec((n_seg, WINDOW, KV_WIDTH), seq_map),
      *cast_specs,
  )
  band_rows = n_seg * (WINDOW + seg_len)
  scratch = [
      pltpu.VMEM((1 if has_state else N_BIAS_VARIANTS, N_KV_HEADS, GROUP * CHUNK, BAND), F32),
      pltpu.VMEM((MIX_ROWS, D_MODEL), BF16),
      pltpu.VMEM((MIX_ROWS, 3 * CONV_WIDTH), F32),
      pltpu.VMEM((N_HEADS, MIX_ROWS, HEAD_DIM), BF16),
      pltpu.VMEM((MIX_ROWS, 2 * KV_WIDTH), F32),
      pltpu.VMEM((N_KV_HEADS, band_rows, HEAD_DIM), BF16),
      pltpu.VMEM((N_KV_HEADS, band_rows, HEAD_DIM), BF16),
      pltpu.VMEM((n_seg * (G_PAD + seg_len), CONV_WIDTH), F32),
  ]
  kernel = functools.partial(_mixer_kernel, n_seg=n_seg, seg_len=seg_len, has_state=has_state,
                             n_cast=len(cast_weights))
  return pl.pallas_call(
      kernel,
      grid=grid,
      in_specs=in_specs,
      out_specs=out_specs,
      out_shape=out_shape,
      scratch_shapes=scratch,
      compiler_params=pltpu.CompilerParams(
          dimension_semantics=("arbitrary", "arbitrary"),
          vmem_limit_bytes=V7X_VMEM_LIMIT_BYTES),
      name="mixer_state" if has_state else "mixer_stream",
  )(*args)


def _outproj_kernel(mixed_ref, x_ref, wout_ref, gpm_ref, gpf_ref, *rest, n_cast):
  rest = list(rest)
  cast_src, rest = rest[:n_cast], rest[n_cast:]
  (x1_ref, hn_ref), rest = rest[:2], rest[2:]
  cast_dst, (acc_ref,) = rest[:n_cast], rest[n_cast:]
  for src, dst in zip(cast_src, cast_dst):
    dst[...] = src[...].astype(BF16)
  gpm = gpm_ref[...]
  gpf = gpf_ref[...]
  part = mixed_ref.shape[0] // OUT_PARTS
  for r0 in range(0, mixed_ref.shape[0], part):
    acc_ref[r0:r0 + part, :] = jnp.dot(
        mixed_ref[r0:r0 + part, :], wout_ref[...], preferred_element_type=F32)
    for r in range(r0, r0 + part, NORM_ROWS):
      x1 = x_ref[r:r + NORM_ROWS, :] + _rms_scale(acc_ref[r:r + NORM_ROWS, :], gpm)
      x1_ref[r:r + NORM_ROWS, :] = x1
      hn_ref[r:r + NORM_ROWS, :] = _rms_scale(x1, gpf).astype(BF16)


def _outproj_call(mixed, x, w_out, g_post_mix, g_pre_ffn, cast_weights=()):
  rows = x.shape[0]
  row_map = lambda i: (i, 0)
  const2 = lambda i: (0, 0)
  once = pl.Buffered(1)
  cast_specs, cast_shapes = _cast_specs(cast_weights, rows // OUT_ROWS, lambda i: i)
  return pl.pallas_call(
      functools.partial(_outproj_kernel, n_cast=len(cast_weights)),
      grid=(rows // OUT_ROWS,),
      in_specs=[
          pl.BlockSpec((OUT_ROWS, D_MODEL), row_map),
          pl.BlockSpec((OUT_ROWS, D_MODEL), row_map),
          pl.BlockSpec((D_MODEL, D_MODEL), const2, pipeline_mode=once),
          pl.BlockSpec((1, D_MODEL), const2, pipeline_mode=once),
          pl.BlockSpec((1, D_MODEL), const2, pipeline_mode=once),
          *cast_specs,
      ],
      out_specs=(pl.BlockSpec((OUT_ROWS, D_MODEL), row_map), pl.BlockSpec((OUT_ROWS, D_MODEL), row_map),
                 *cast_specs),
      out_shape=(jax.ShapeDtypeStruct((rows, D_MODEL), F32), jax.ShapeDtypeStruct((rows, D_MODEL), BF16),
                 *cast_shapes),
      scratch_shapes=[pltpu.VMEM((OUT_ROWS, D_MODEL), F32)],
      compiler_params=pltpu.CompilerParams(
          dimension_semantics=("arbitrary",), vmem_limit_bytes=V7X_VMEM_LIMIT_BYTES),
      name="outproj",
  )(mixed, x, w_out, g_post_mix, g_pre_ffn, *cast_weights)


def _ffn_tile(i, j):
  n_tiles = pl.cdiv(D_FF, FFN_TILE)
  step = jnp.where(i % 2 == 0, j, n_tiles - 1 - j)
  mid = n_tiles // 2 - 1
  return jnp.where(step < mid, step, jnp.where(step == mid, n_tiles - 1, step - 1))


def _ffn_kernel(x1_ref, hn_ref, gqf_ref, wg_ref, wu_ref, wd_ref, y_ref):
  j = pl.program_id(1)
  tile = _ffn_tile(pl.program_id(0), j)
  n_tiles = pl.cdiv(D_FF, FFN_TILE)
  rows = y_ref.shape[0]

  def swiglu_down(cols):
    hn = hn_ref[...]
    gate = jnp.dot(hn, wg_ref[:, 0:cols], preferred_element_type=F32)
    up = jnp.dot(hn, wu_ref[:, 0:cols], preferred_element_type=F32)
    mid = (gate * (1.0 / (1.0 + jnp.exp(-gate))) * up).astype(BF16)
    return jnp.dot(mid, wd_ref[0:cols, :], preferred_element_type=F32)

  assert n_tiles >= 4
  full = tile < n_tiles - 1

  @pl.when(full & (j == 0))
  def _():
    y_ref[...] = swiglu_down(FFN_TILE)

  @pl.when(full & (j > 0))
  def _():
    y_ref[...] += swiglu_down(FFN_TILE)

  @pl.when(jnp.logical_not(full))
  def _():
    y_ref[...] += swiglu_down(D_FF - (n_tiles - 1) * FFN_TILE)

  @pl.when(j == n_tiles - 1)
  def _():
    gqf = gqf_ref[...]
    for r in range(0, rows, NORM_ROWS):
      y_ref[r:r + NORM_ROWS, :] = x1_ref[r:r + NORM_ROWS, :] + _rms_scale(y_ref[r:r + NORM_ROWS, :], gqf)


def _ffn_call(x1, hn, g_post_ffn, w_gate, w_up, w_down):
  rows = x1.shape[0]
  row_map = lambda i, j: (i, 0)
  return pl.pallas_call(
      _ffn_kernel,
      grid=(rows // FFN_ROWS, pl.cdiv(D_FF, FFN_TILE)),
      in_specs=[
          pl.BlockSpec((FFN_ROWS, D_MODEL), row_map),
          pl.BlockSpec((FFN_ROWS, D_MODEL), row_map),
          pl.BlockSpec((1, D_MODEL), lambda i, j: (0, 0), pipeline_mode=pl.Buffered(1)),
          pl.BlockSpec((D_MODEL, FFN_TILE), lambda i, j: (0, _ffn_tile(i, j))),
          pl.BlockSpec((D_MODEL, FFN_TILE), lambda i, j: (0, _ffn_tile(i, j))),
          pl.BlockSpec((FFN_TILE, D_MODEL), lambda i, j: (_ffn_tile(i, j), 0)),
      ],
      out_specs=pl.BlockSpec((FFN_ROWS, D_MODEL), row_map),
      out_shape=jax.ShapeDtypeStruct((rows, D_MODEL), F32),
      compiler_params=pltpu.CompilerParams(
          dimension_semantics=("arbitrary", "arbitrary"),
          vmem_limit_bytes=V7X_VMEM_LIMIT_BYTES),
      name="ffn",
  )(x1, hn, g_post_ffn, w_gate, w_up, w_down)


def kernel(x_prompt, x_sample, state_conv, cache_k, cache_v, rel_table, g_pre_mix, w_in, conv_w,
           attn_sinks, w_out, g_post_mix, g_pre_ffn, w_gate, w_up, w_down, g_post_ffn):
  depth, n_dec, cache_len = cache_k.shape[:3]
  batch, seq_len, _ = x_prompt.shape
  assert depth == 1 and cache_len == WINDOW and x_sample.shape[1] == CHUNK
  assert seq_len % MIX_ROWS == 0 and MIX_ROWS % CHUNK == 0

  rel = (jnp.arange(BAND) - WINDOW)[None, :] - jnp.arange(CHUNK)[:, None]
  bkt = _t5_bucket(rel).astype(jnp.int32)

  row = lambda g: g[0].reshape(1, D_MODEL)
  win = w_in[0].astype(BF16)
  mix_args = (bkt, rel_table, attn_sinks[0], row(g_pre_mix), win, conv_w[0])
  mixed_p, conv_p, k_p, v_p, wg = _mixer_call(
      x_prompt, None, None, None, *mix_args, has_state=False, cast_weights=(w_gate[0],))
  mixed_s, conv_s, k_s, v_s = _mixer_call(
      x_sample, state_conv[0], cache_k[0].reshape(n_dec, WINDOW, KV_WIDTH),
      cache_v[0].reshape(n_dec, WINDOW, KV_WIDTH), *mix_args, has_state=True)

  out_args = (w_out[0].astype(BF16), row(g_post_mix), row(g_pre_ffn))
  x1_p, hn_p, wd = _outproj_call(
      mixed_p.reshape(-1, D_MODEL), x_prompt.reshape(-1, D_MODEL), *out_args, cast_weights=(w_down[0],))
  x1_s, hn_s = _outproj_call(mixed_s.reshape(-1, D_MODEL), x_sample.reshape(-1, D_MODEL), *out_args)
  ffn_args = (row(g_post_ffn), wg, w_up[0].astype(BF16), wd)
  y_p = _ffn_call(x1_p, hn_p, *ffn_args)
  y_s = _ffn_call(x1_s, hn_s, *ffn_args)

  heads = lambda a: a.reshape(1, a.shape[0], WINDOW, N_KV_HEADS, HEAD_DIM)
  return (y_p.reshape(x_prompt.shape), y_s.reshape(x_sample.shape),
          conv_p[None], heads(k_p), heads(v_p), conv_s[None], heads(k_s), heads(v_s))
```

```python
import functools
import math

import jax
import jax.numpy as jnp
from jax import lax
from jax.experimental import pallas as pl
from jax.experimental.pallas import tpu as pltpu

D_MODEL = 2048
CHUNK = 64
HEAD_DIM = 64
CONV_WIDTH = D_MODEL // 2
CONV_K = 3
N_HEADS = (D_MODEL - CONV_WIDTH) // HEAD_DIM
N_KV_HEADS = 4
GROUP = N_HEADS // N_KV_HEADS
ATTN_WIDTH = N_HEADS * HEAD_DIM
KV_WIDTH = N_KV_HEADS * HEAD_DIM
IN_COLS = 3 * CONV_WIDTH + ATTN_WIDTH + 2 * KV_WIDTH
WINDOW = 128
BAND = WINDOW + CHUNK
NUM_BUCKETS = 32
MAX_DISTANCE = 128
D_FF = -(-8 * D_MODEL // (3 * 256)) * 256
EPS = 1e-6
NEG = -1e30

OFF_CB, OFF_CC, OFF_CU = 0, CONV_WIDTH, 2 * CONV_WIDTH
OFF_Q = 3 * CONV_WIDTH
OFF_KV = OFF_Q + ATTN_WIDTH

SUBLANES = 8
BF16_ROWS = 16
V7X_VMEM_LIMIT_BYTES = 61 << 20
MIX_ROWS = 512
OUT_ROWS = 512
OUT_PARTS = 4
FFN_ROWS = 1024
FFN_TILE = 512
CONV_COLS = 512
NORM_ROWS = 16
NORM_UNROLL = 8
CONV_ROWS = 32
G_PAD = SUBLANES

F32 = jnp.float32
BF16 = jnp.bfloat16


def _t5_bucket(rel):
  half = NUM_BUCKETS // 2
  ret = jnp.where(rel > 0, half, 0)
  n = jnp.abs(rel)
  max_exact = half // 2
  nf = jnp.maximum(n, 1).astype(jnp.float32)
  large = max_exact + (jnp.log(nf / max_exact) / math.log(MAX_DISTANCE / max_exact)
                       * (half - max_exact)).astype(jnp.int32)
  large = jnp.minimum(large, half - 1)
  return ret + jnp.where(n < max_exact, n, large)


def _rms_scale(y, gain):
  return y * lax.rsqrt(jnp.mean(y * y, axis=-1, keepdims=True) + EPS) * gain


def _mixer_kernel(tbl_ref, sink_ref, bkt_ref, x_ref, gpre_ref, win_ref, convw_ref, *rest,
                  n_seg, seg_len, has_state, n_cast):
  rest = list(rest)
  if has_state:
    state_ref, ck_ref, cv_ref = rest[:3]
    rest = rest[3:]
  cast_src, rest = rest[:n_cast], rest[n_cast:]
  (mixed_ref, nconv_ref, nk_ref, nv_ref), rest = rest[:4], rest[4:]
  cast_dst, rest = rest[:n_cast], rest[n_cast:]
  bias_ref, h_ref, zc_ref, q_ref, kvf_ref, kb_ref, vb_ref, gs_ref = rest
  rows = n_seg * seg_len
  g_stride = G_PAD + seg_len
  b_stride = WINDOW + seg_len
  first_call = (pl.program_id(0) == 0) & (pl.program_id(1) == 0)
  seq_start = pl.program_id(1) == 0

  @pl.when(first_call)
  def _():
    bkt = bkt_ref[...]
    for h in range(N_HEADS):
      acc = jnp.zeros((CHUNK, BAND), F32)
      for j in range(NUM_BUCKETS):
        acc = jnp.where(bkt == j, tbl_ref[j, h], acc)
      g = h % GROUP
      bias_ref[h // GROUP, g * CHUNK:(g + 1) * CHUNK, :] = acc

  if has_state:
    for s in range(n_seg):
      gs_ref[s * g_stride + G_PAD - (CONV_K - 1):s * g_stride + G_PAD, :] = state_ref[s]
      for kh in range(N_KV_HEADS):
        kb_ref[kh, s * b_stride:s * b_stride + WINDOW, :] = (
            ck_ref[s, :, kh * HEAD_DIM:(kh + 1) * HEAD_DIM].astype(BF16))
        vb_ref[kh, s * b_stride:s * b_stride + WINDOW, :] = (
            cv_ref[s, :, kh * HEAD_DIM:(kh + 1) * HEAD_DIM].astype(BF16))
  else:
    @pl.when(seq_start)
    def _():
      gs_ref[0:G_PAD, :] = jnp.zeros((G_PAD, CONV_WIDTH), F32)
      kb_ref[:, 0:WINDOW, :] = jnp.zeros((N_KV_HEADS, WINDOW, HEAD_DIM), BF16)
      vb_ref[:, 0:WINDOW, :] = jnp.zeros((N_KV_HEADS, WINDOW, HEAD_DIM), BF16)

    @pl.when(jnp.logical_not(seq_start))
    def _():
      gs_ref[0:G_PAD, :] = gs_ref[seg_len:seg_len + G_PAD, :]
      kb_ref[:, 0:WINDOW, :] = kb_ref[:, seg_len:seg_len + WINDOW, :]
      vb_ref[:, 0:WINDOW, :] = vb_ref[:, seg_len:seg_len + WINDOW, :]

  gpre = gpre_ref[...]

  def norm_body(i, carry):
    r = pl.multiple_of(i * NORM_ROWS, NORM_ROWS)
    h_ref[pl.ds(r, NORM_ROWS), :] = _rms_scale(x_ref[pl.ds(r, NORM_ROWS), :], gpre).astype(BF16)
    return carry

  lax.fori_loop(0, rows // NORM_ROWS, norm_body, 0, unroll=NORM_UNROLL)

  for src, dst in zip(cast_src, cast_dst):
    dst[...] = src[...].astype(BF16)

  for c0 in range(0, ATTN_WIDTH, CONV_COLS):
    zc_ref[:, 0:CONV_COLS] = jnp.dot(
        h_ref[...], win_ref[:, OFF_Q + c0:OFF_Q + c0 + CONV_COLS], preferred_element_type=F32)
    for hh in range(CONV_COLS // HEAD_DIM):
      q_ref[c0 // HEAD_DIM + hh] = (
          zc_ref[:, hh * HEAD_DIM:(hh + 1) * HEAD_DIM] * (HEAD_DIM ** -0.5)).astype(BF16)

  kvf_ref[...] = jnp.dot(h_ref[...], win_ref[:, OFF_KV:OFF_KV + 2 * KV_WIDTH], preferred_element_type=F32)
  for s in range(n_seg):
    for kh in range(N_KV_HEADS):
      kb_ref[kh, s * b_stride + WINDOW:(s + 1) * b_stride, :] = (
          kvf_ref[s * seg_len:(s + 1) * seg_len, kh * HEAD_DIM:(kh + 1) * HEAD_DIM].astype(BF16))
      vb_ref[kh, s * b_stride + WINDOW:(s + 1) * b_stride, :] = (
          kvf_ref[s * seg_len:(s + 1) * seg_len,
                  KV_WIDTH + kh * HEAD_DIM:KV_WIDTH + (kh + 1) * HEAD_DIM].astype(BF16))

  for s in range(n_seg):
    if seg_len >= WINDOW:
      nk_ref[s] = kvf_ref[(s + 1) * seg_len - WINDOW:(s + 1) * seg_len, 0:KV_WIDTH]
      nv_ref[s] = kvf_ref[(s + 1) * seg_len - WINDOW:(s + 1) * seg_len, KV_WIDTH:2 * KV_WIDTH]
    else:
      nk_ref[s, 0:WINDOW - seg_len, :] = ck_ref[s, seg_len:WINDOW, :]
      nv_ref[s, 0:WINDOW - seg_len, :] = cv_ref[s, seg_len:WINDOW, :]
      nk_ref[s, WINDOW - seg_len:WINDOW, :] = kvf_ref[s * seg_len:(s + 1) * seg_len, 0:KV_WIDTH]
      nv_ref[s, WINDOW - seg_len:WINDOW, :] = kvf_ref[s * seg_len:(s + 1) * seg_len, KV_WIDTH:2 * KV_WIDTH]

  row_id = lax.broadcasted_iota(jnp.int32, (GROUP * CHUNK, 1), 0)
  sink_cols = []
  for kh in range(N_KV_HEADS):
    col = jnp.full((GROUP * CHUNK, 1), sink_ref[kh * GROUP + GROUP - 1], F32)
    for g in range(GROUP - 2, -1, -1):
      col = jnp.where(row_id < (g + 1) * CHUNK, sink_ref[kh * GROUP + g], col)
    sink_cols.append(col)
  n_items = rows // CHUNK
  assert n_seg == 1 or seg_len == CHUNK
  band_step = CHUNK if n_seg == 1 else b_stride

  key_row = lax.broadcasted_iota(jnp.int32, (1, BAND), 1)

  def attend(item):
    q0 = item * CHUNK
    b0 = item * band_step
    masked_keys = 0 if has_state else max(WINDOW - item * CHUNK, 0)
    if masked_keys:
      start_mask = jnp.where(seq_start & (key_row < masked_keys), NEG, 0.0)
    for kh in range(N_KV_HEADS):
      qs = jnp.concatenate(
          [q_ref[kh * GROUP + g, q0:q0 + CHUNK, :] for g in range(GROUP)], axis=0)
      kband = kb_ref[kh, b0:b0 + BAND, :]
      vband = vb_ref[kh, b0:b0 + BAND, :]
      s = lax.dot_general(qs, kband, (((1,), (1,)), ((), ())), preferred_element_type=F32)
      s = s + bias_ref[kh]
      if masked_keys:
        s = s + start_mask
      sink = sink_cols[kh]
      m = jnp.maximum(jnp.max(s, axis=-1, keepdims=True), sink)
      p = jnp.exp(s - m)
      denom = jnp.sum(p, axis=-1, keepdims=True) + jnp.exp(sink - m)
      o = jnp.dot(p.astype(BF16), vband, preferred_element_type=F32) * (1.0 / denom)
      o = jnp.concatenate([o[g * CHUNK:(g + 1) * CHUNK, :] for g in range(GROUP)], axis=1)
      mixed_ref[q0:q0 + CHUNK, CONV_WIDTH + kh * GROUP * HEAD_DIM:
                CONV_WIDTH + (kh + 1) * GROUP * HEAD_DIM] = o.astype(BF16)

  n_pieces = 3 * CONV_WIDTH // CONV_COLS
  assert n_pieces <= n_items
  for item in range(n_items):
    if item < n_pieces:
      cols = slice(item * CONV_COLS, (item + 1) * CONV_COLS)
      zc_ref[:, cols] = jnp.dot(h_ref[...], win_ref[:, cols], preferred_element_type=F32)
    attend(item)

  for c0 in range(0, CONV_WIDTH, CONV_COLS):
    cols = slice(c0, c0 + CONV_COLS)
    cb_cols, cc_cols, cu_cols = (slice(off + c0, off + c0 + CONV_COLS) for off in (OFF_CB, OFF_CC, OFF_CU))
    for s in range(n_seg):
      for r in range(0, seg_len, CONV_ROWS):
        zr = s * seg_len + r
        gr = s * g_stride + G_PAD + r
        gs_ref[gr:gr + CONV_ROWS, cols] = (
            zc_ref[zr:zr + CONV_ROWS, cc_cols] * zc_ref[zr:zr + CONV_ROWS, cu_cols])
    for s in range(n_seg):
      for r in range(0, seg_len, CONV_ROWS):
        zr = s * seg_len + r
        gr = s * g_stride + G_PAD + r
        conv = convw_ref[0:1, cols] * gs_ref[gr - 2:gr - 2 + CONV_ROWS, cols]
        conv = conv + convw_ref[1:2, cols] * gs_ref[gr - 1:gr - 1 + CONV_ROWS, cols]
        conv = conv + convw_ref[2:3, cols] * gs_ref[gr:gr + CONV_ROWS, cols]
        mixed_ref[zr:zr + CONV_ROWS, cols] = (zc_ref[zr:zr + CONV_ROWS, cb_cols] * conv).astype(BF16)

  for s in range(n_seg):
    g_end = s * g_stride + G_PAD + seg_len
    nconv_ref[s] = gs_ref[g_end - (CONV_K - 1):g_end, :]


def _cast_specs(weights, n_steps, step_of):
  specs, shapes = [], []
  for w in weights:
    assert w.shape[0] % (n_steps * BF16_ROWS) == 0
    specs.append(pl.BlockSpec((w.shape[0] // n_steps, w.shape[1]), lambda *ids: (step_of(*ids), 0)))
    shapes.append(jax.ShapeDtypeStruct(w.shape, BF16))
  return specs, shapes


def _mixer_call(x, state, cache_k, cache_v, bkt, rel_table, sinks, g_pre, w_in, conv_w, *, has_state,
                cast_weights=()):
  n_seq, seq_len, _ = x.shape
  if has_state:
    n_seg, seg_len = MIX_ROWS // seq_len, seq_len
    grid = (n_seq // n_seg, 1)
    x = x.reshape(n_seq // n_seg, MIX_ROWS, D_MODEL)
    x_map = lambda b, s: (b, 0, 0)
    seq_map = lambda b, s: (b, 0, 0)
  else:
    n_seg, seg_len = 1, MIX_ROWS
    grid = (n_seq, seq_len // MIX_ROWS)
    x_map = lambda b, s: (b, s, 0)
    seq_map = lambda b, s: (b, 0, 0)
  const2 = lambda b, s: (0, 0)
  once = pl.Buffered(1)

  smem = pl.BlockSpec(memory_space=pltpu.SMEM)
  in_specs = [
      smem, smem,
      pl.BlockSpec((CHUNK, BAND), const2, pipeline_mode=once),
      pl.BlockSpec((None, MIX_ROWS, D_MODEL), x_map, pipeline_mode=once if has_state else None),
      pl.BlockSpec((1, D_MODEL), const2, pipeline_mode=once),
      pl.BlockSpec((D_MODEL, IN_COLS), const2, pipeline_mode=once),
      pl.BlockSpec((CONV_K, CONV_WIDTH), const2, pipeline_mode=once),
  ]
  args = [rel_table, sinks, bkt, x, g_pre, w_in, conv_w]
  if has_state:
    in_specs += [
        pl.BlockSpec((n_seg, CONV_K - 1, CONV_WIDTH), seq_map),
        pl.BlockSpec((n_seg, WINDOW, KV_WIDTH), seq_map, pipeline_mode=once),
        pl.BlockSpec((n_seg, WINDOW, KV_WIDTH), seq_map, pipeline_mode=once),
    ]
    args += [state, cache_k, cache_v]
  cast_specs, cast_shapes = _cast_specs(cast_weights, grid[0] * grid[1], lambda b, s: b * grid[1] + s)
  in_specs += cast_specs
  args += list(cast_weights)
  out_shape = (
      jax.ShapeDtypeStruct(x.shape, BF16),
      jax.ShapeDtypeStruct((n_seq, CONV_K - 1, CONV_WIDTH), F32),
      jax.ShapeDtypeStruct((n_seq, WINDOW, KV_WIDTH), F32),
      jax.ShapeDtypeStruct((n_seq, WINDOW, KV_WIDTH), F32),
      *cast_shapes,
  )
  out_specs = (
      pl.BlockSpec((None, MIX_ROWS, D_MODEL), x_map),
      pl.BlockSpec((n_seg, CONV_K - 1, CONV_WIDTH), seq_map),
      pl.BlockSpec((n_seg, WINDOW, KV_WIDTH), seq_map),
      pl.BlockSpec((n_seg, WINDOW, KV_WIDTH), seq_map),
      *cast_specs,
  )
  band_rows = n_seg * (WINDOW + seg_len)
  scratch = [
      pltpu.VMEM((N_KV_HEADS, GROUP * CHUNK, BAND), F32),
      pltpu.VMEM((MIX_ROWS, D_MODEL), BF16),
      pltpu.VMEM((MIX_ROWS, 3 * CONV_WIDTH), F32),
      pltpu.VMEM((N_HEADS, MIX_ROWS, HEAD_DIM), BF16),
      pltpu.VMEM((MIX_ROWS, 2 * KV_WIDTH), F32),
      pltpu.VMEM((N_KV_HEADS, band_rows, HEAD_DIM), BF16),
      pltpu.VMEM((N_KV_HEADS, band_rows, HEAD_DIM), BF16),
      pltpu.VMEM((n_seg * (G_PAD + seg_len), CONV_WIDTH), F32),
  ]
  kernel = functools.partial(_mixer_kernel, n_seg=n_seg, seg_len=seg_len, has_state=has_state,
                             n_cast=len(cast_weights))
  return pl.pallas_call(
      kernel,
      grid=grid,
      in_specs=in_specs,
      out_specs=out_specs,
      out_shape=out_shape,
      scratch_shapes=scratch,
      compiler_params=pltpu.CompilerParams(
          dimension_semantics=("arbitrary", "arbitrary"),
          vmem_limit_bytes=V7X_VMEM_LIMIT_BYTES),
      name="mixer_state" if has_state else "mixer_stream",
  )(*args)


def _outproj_kernel(mixed_ref, x_ref, wout_ref, gpm_ref, gpf_ref, *rest, n_cast):
  rest = list(rest)
  cast_src, rest = rest[:n_cast], rest[n_cast:]
  (x1_ref, hn_ref), rest = rest[:2], rest[2:]
  cast_dst, (acc_ref,) = rest[:n_cast], rest[n_cast:]
  for src, dst in zip(cast_src, cast_dst):
    dst[...] = src[...].astype(BF16)
  gpm = gpm_ref[...]
  gpf = gpf_ref[...]
  part = mixed_ref.shape[0] // OUT_PARTS
  for r0 in range(0, mixed_ref.shape[0], part):
    acc_ref[r0:r0 + part, :] = jnp.dot(
        mixed_ref[r0:r0 + part, :], wout_ref[...], preferred_element_type=F32)
    for r in range(r0, r0 + part, NORM_ROWS):
      x1 = x_ref[r:r + NORM_ROWS, :] + _rms_scale(acc_ref[r:r + NORM_ROWS, :], gpm)
      x1_ref[r:r + NORM_ROWS, :] = x1
      hn_ref[r:r + NORM_ROWS, :] = _rms_scale(x1, gpf).astype(BF16)


def _outproj_call(mixed, x, w_out, g_post_mix, g_pre_ffn, cast_weights=()):
  rows = x.shape[0]
  row_map = lambda i: (i, 0)
  const2 = lambda i: (0, 0)
  once = pl.Buffered(1)
  cast_specs, cast_shapes = _cast_specs(cast_weights, rows // OUT_ROWS, lambda i: i)
  return pl.pallas_call(
      functools.partial(_outproj_kernel, n_cast=len(cast_weights)),
      grid=(rows // OUT_ROWS,),
      in_specs=[
          pl.BlockSpec((OUT_ROWS, D_MODEL), row_map),
          pl.BlockSpec((OUT_ROWS, D_MODEL), row_map),
          pl.BlockSpec((D_MODEL, D_MODEL), const2, pipeline_mode=once),
          pl.BlockSpec((1, D_MODEL), const2, pipeline_mode=once),
          pl.BlockSpec((1, D_MODEL), const2, pipeline_mode=once),
          *cast_specs,
      ],
      out_specs=(pl.BlockSpec((OUT_ROWS, D_MODEL), row_map), pl.BlockSpec((OUT_ROWS, D_MODEL), row_map),
                 *cast_specs),
      out_shape=(jax.ShapeDtypeStruct((rows, D_MODEL), F32), jax.ShapeDtypeStruct((rows, D_MODEL), BF16),
                 *cast_shapes),
      scratch_shapes=[pltpu.VMEM((OUT_ROWS, D_MODEL), F32)],
      compiler_params=pltpu.CompilerParams(
          dimension_semantics=("arbitrary",), vmem_limit_bytes=V7X_VMEM_LIMIT_BYTES),
      name="outproj",
  )(mixed, x, w_out, g_post_mix, g_pre_ffn, *cast_weights)


def _ffn_tile(i, j):
  n_tiles = pl.cdiv(D_FF, FFN_TILE)
  step = jnp.where(i % 2 == 0, j, n_tiles - 1 - j)
  mid = n_tiles // 2 - 1
  return jnp.where(step < mid, step, jnp.where(step == mid, n_tiles - 1, step - 1))


def _ffn_kernel(x1_ref, hn_ref, gqf_ref, wg_ref, wu_ref, wd_ref, y_ref):
  j = pl.program_id(1)
  tile = _ffn_tile(pl.program_id(0), j)
  n_tiles = pl.cdiv(D_FF, FFN_TILE)
  rows = y_ref.shape[0]

  def swiglu_down(cols):
    hn = hn_ref[...]
    gate = jnp.dot(hn, wg_ref[:, 0:cols], preferred_element_type=F32)
    up = jnp.dot(hn, wu_ref[:, 0:cols], preferred_element_type=F32)
    mid = (gate * (1.0 / (1.0 + jnp.exp(-gate))) * up).astype(BF16)
    return jnp.dot(mid, wd_ref[0:cols, :], preferred_element_type=F32)

  assert n_tiles >= 4
  full = tile < n_tiles - 1

  @pl.when(full & (j == 0))
  def _():
    y_ref[...] = swiglu_down(FFN_TILE)

  @pl.when(full & (j > 0))
  def _():
    y_ref[...] += swiglu_down(FFN_TILE)

  @pl.when(jnp.logical_not(full))
  def _():
    y_ref[...] += swiglu_down(D_FF - (n_tiles - 1) * FFN_TILE)

  @pl.when(j == n_tiles - 1)
  def _():
    gqf = gqf_ref[...]
    for r in range(0, rows, NORM_ROWS):
      y_ref[r:r + NORM_ROWS, :] = x1_ref[r:r + NORM_ROWS, :] + _rms_scale(y_ref[r:r + NORM_ROWS, :], gqf)


def _ffn_call(x1, hn, g_post_ffn, w_gate, w_up, w_down):
  rows = x1.shape[0]
  row_map = lambda i, j: (i, 0)
  return pl.pallas_call(
      _ffn_kernel,
      grid=(rows // FFN_ROWS, pl.cdiv(D_FF, FFN_TILE)),
      in_specs=[
          pl.BlockSpec((FFN_ROWS, D_MODEL), row_map),
          pl.BlockSpec((FFN_ROWS, D_MODEL), row_map),
          pl.BlockSpec((1, D_MODEL), lambda i, j: (0, 0), pipeline_mode=pl.Buffered(1)),
          pl.BlockSpec((D_MODEL, FFN_TILE), lambda i, j: (0, _ffn_tile(i, j))),
          pl.BlockSpec((D_MODEL, FFN_TILE), lambda i, j: (0, _ffn_tile(i, j))),
          pl.BlockSpec((FFN_TILE, D_MODEL), lambda i, j: (_ffn_tile(i, j), 0)),
      ],
      out_specs=pl.BlockSpec((FFN_ROWS, D_MODEL), row_map),
      out_shape=jax.ShapeDtypeStruct((rows, D_MODEL), F32),
      compiler_params=pltpu.CompilerParams(
          dimension_semantics=("arbitrary", "arbitrary"),
          vmem_limit_bytes=V7X_VMEM_LIMIT_BYTES),
      name="ffn",
  )(x1, hn, g_post_ffn, w_gate, w_up, w_down)


def kernel(x_prompt, x_sample, state_conv, cache_k, cache_v, rel_table, g_pre_mix, w_in, conv_w,
           attn_sinks, w_out, g_post_mix, g_pre_ffn, w_gate, w_up, w_down, g_post_ffn):
  depth, n_dec, cache_len = cache_k.shape[:3]
  batch, seq_len, _ = x_prompt.shape
  assert depth == 1 and cache_len == WINDOW and x_sample.shape[1] == CHUNK
  assert seq_len % MIX_ROWS == 0 and MIX_ROWS % CHUNK == 0

  rel = (jnp.arange(BAND) - WINDOW)[None, :] - jnp.arange(CHUNK)[:, None]
  bkt = _t5_bucket(rel).astype(jnp.int32)

  row = lambda g: g[0].reshape(1, D_MODEL)
  win = w_in[0].astype(BF16)
  mix_args = (bkt, rel_table, attn_sinks[0], row(g_pre_mix), win, conv_w[0])
  mixed_p, conv_p, k_p, v_p, wg, wu = _mixer_call(
      x_prompt, None, None, None, *mix_args, has_state=False, cast_weights=(w_gate[0], w_up[0]))
  mixed_s, conv_s, k_s, v_s = _mixer_call(
      x_sample, state_conv[0], cache_k[0].reshape(n_dec, WINDOW, KV_WIDTH),
      cache_v[0].reshape(n_dec, WINDOW, KV_WIDTH), *mix_args, has_state=True)

  out_args = (w_out[0].astype(BF16), row(g_post_mix), row(g_pre_ffn))
  x1_p, hn_p, wd = _outproj_call(
      mixed_p.reshape(-1, D_MODEL), x_prompt.reshape(-1, D_MODEL), *out_args, cast_weights=(w_down[0],))
  x1_s, hn_s = _outproj_call(mixed_s.reshape(-1, D_MODEL), x_sample.reshape(-1, D_MODEL), *out_args)
  ffn_args = (row(g_post_ffn), wg, wu, wd)
  y_p = _ffn_call(x1_p, hn_p, *ffn_args)
  y_s = _ffn_call(x1_s, hn_s, *ffn_args)

  heads = lambda a: a.reshape(1, a.shape[0], WINDOW, N_KV_HEADS, HEAD_DIM)
  return (y_p.reshape(x_prompt.shape), y_s.reshape(x_sample.shape),
          conv_p[None], heads(k_p), heads(v_p), conv_s[None], heads(k_s), heads(v_s))
```

```python
import functools
import math

import jax
import jax.numpy as jnp
from jax import lax
from jax.experimental import pallas as pl
from jax.experimental.pallas import tpu as pltpu

D_MODEL = 2048
CHUNK = 64
HEAD_DIM = 64
CONV_WIDTH = D_MODEL // 2
CONV_K = 3
N_HEADS = (D_MODEL - CONV_WIDTH) // HEAD_DIM
N_KV_HEADS = 4
GROUP = N_HEADS // N_KV_HEADS
ATTN_WIDTH = N_HEADS * HEAD_DIM
KV_WIDTH = N_KV_HEADS * HEAD_DIM
IN_COLS = 3 * CONV_WIDTH + ATTN_WIDTH + 2 * KV_WIDTH
WINDOW = 128
BAND = WINDOW + CHUNK
NUM_BUCKETS = 32
MAX_DISTANCE = 128
D_FF = -(-8 * D_MODEL // (3 * 256)) * 256
EPS = 1e-6
NEG = -1e30

OFF_CB, OFF_CC, OFF_CU = 0, CONV_WIDTH, 2 * CONV_WIDTH
OFF_Q = 3 * CONV_WIDTH
OFF_KV = OFF_Q + ATTN_WIDTH

SUBLANES = 8
BF16_ROWS = 16
V7X_VMEM_LIMIT_BYTES = 61 << 20
MIX_ROWS = 512
OUT_ROWS = 512
OUT_PARTS = 4
FFN_ROWS = 1024
FFN_TILE = 512
CONV_COLS = 512
NORM_ROWS = 16
NORM_UNROLL = 8
CONV_ROWS = 32
G_PAD = SUBLANES

F32 = jnp.float32
BF16 = jnp.bfloat16


def _t5_bucket(rel):
  half = NUM_BUCKETS // 2
  ret = jnp.where(rel > 0, half, 0)
  n = jnp.abs(rel)
  max_exact = half // 2
  nf = jnp.maximum(n, 1).astype(jnp.float32)
  large = max_exact + (jnp.log(nf / max_exact) / math.log(MAX_DISTANCE / max_exact)
                       * (half - max_exact)).astype(jnp.int32)
  large = jnp.minimum(large, half - 1)
  return ret + jnp.where(n < max_exact, n, large)


def _rms_scale(y, gain):
  return y * lax.rsqrt(jnp.mean(y * y, axis=-1, keepdims=True) + EPS) * gain


def _mixer_kernel(tbl_ref, sink_ref, bkt_ref, x_ref, gpre_ref, win_ref, convw_ref, *rest,
                  n_seg, seg_len, has_state, n_cast):
  rest = list(rest)
  if has_state:
    state_ref, ck_ref, cv_ref = rest[:3]
    rest = rest[3:]
  cast_src, rest = rest[:n_cast], rest[n_cast:]
  (mixed_ref, nconv_ref, nk_ref, nv_ref), rest = rest[:4], rest[4:]
  cast_dst, rest = rest[:n_cast], rest[n_cast:]
  bias_ref, h_ref, zc_ref, q_ref, kvf_ref, kb_ref, vb_ref, gs_ref = rest
  rows = n_seg * seg_len
  g_stride = G_PAD + seg_len
  b_stride = WINDOW + seg_len
  first_call = (pl.program_id(0) == 0) & (pl.program_id(1) == 0)
  seq_start = pl.program_id(1) == 0

  @pl.when(first_call)
  def _():
    bkt = bkt_ref[...]
    for h in range(N_HEADS):
      acc = jnp.zeros((CHUNK, BAND), F32)
      for j in range(NUM_BUCKETS):
        acc = jnp.where(bkt == j, tbl_ref[j, h], acc)
      g = h % GROUP
      bias_ref[h // GROUP, g * CHUNK:(g + 1) * CHUNK, :] = acc

  if has_state:
    for s in range(n_seg):
      gs_ref[s * g_stride + G_PAD - (CONV_K - 1):s * g_stride + G_PAD, :] = state_ref[s]
      for kh in range(N_KV_HEADS):
        kb_ref[kh, s * b_stride:s * b_stride + WINDOW, :] = (
            ck_ref[s, :, kh * HEAD_DIM:(kh + 1) * HEAD_DIM].astype(BF16))
        vb_ref[kh, s * b_stride:s * b_stride + WINDOW, :] = (
            cv_ref[s, :, kh * HEAD_DIM:(kh + 1) * HEAD_DIM].astype(BF16))
  else:
    @pl.when(seq_start)
    def _():
      gs_ref[0:G_PAD, :] = jnp.zeros((G_PAD, CONV_WIDTH), F32)
      kb_ref[:, 0:WINDOW, :] = jnp.zeros((N_KV_HEADS, WINDOW, HEAD_DIM), BF16)
      vb_ref[:, 0:WINDOW, :] = jnp.zeros((N_KV_HEADS, WINDOW, HEAD_DIM), BF16)

    @pl.when(jnp.logical_not(seq_start))
    def _():
      gs_ref[0:G_PAD, :] = gs_ref[seg_len:seg_len + G_PAD, :]
      kb_ref[:, 0:WINDOW, :] = kb_ref[:, seg_len:seg_len + WINDOW, :]
      vb_ref[:, 0:WINDOW, :] = vb_ref[:, seg_len:seg_len + WINDOW, :]

  gpre = gpre_ref[...]

  def norm_body(i, carry):
    r = pl.multiple_of(i * NORM_ROWS, NORM_ROWS)
    h_ref[pl.ds(r, NORM_ROWS), :] = _rms_scale(x_ref[pl.ds(r, NORM_ROWS), :], gpre).astype(BF16)
    return carry

  lax.fori_loop(0, rows // NORM_ROWS, norm_body, 0, unroll=NORM_UNROLL)

  for src, dst in zip(cast_src, cast_dst):
    dst[...] = src[...].astype(BF16)

  for c0 in range(0, ATTN_WIDTH, CONV_COLS):
    zc_ref[:, 0:CONV_COLS] = jnp.dot(
        h_ref[...], win_ref[:, OFF_Q + c0:OFF_Q + c0 + CONV_COLS], preferred_element_type=F32)
    for hh in range(CONV_COLS // HEAD_DIM):
      q_ref[c0 // HEAD_DIM + hh] = (
          zc_ref[:, hh * HEAD_DIM:(hh + 1) * HEAD_DIM] * (HEAD_DIM ** -0.5)).astype(BF16)

  kvf_ref[...] = jnp.dot(h_ref[...], win_ref[:, OFF_KV:OFF_KV + 2 * KV_WIDTH], preferred_element_type=F32)
  for s in range(n_seg):
    for kh in range(N_KV_HEADS):
      kb_ref[kh, s * b_stride + WINDOW:(s + 1) * b_stride, :] = (
          kvf_ref[s * seg_len:(s + 1) * seg_len, kh * HEAD_DIM:(kh + 1) * HEAD_DIM].astype(BF16))
      vb_ref[kh, s * b_stride + WINDOW:(s + 1) * b_stride, :] = (
          kvf_ref[s * seg_len:(s + 1) * seg_len,
                  KV_WIDTH + kh * HEAD_DIM:KV_WIDTH + (kh + 1) * HEAD_DIM].astype(BF16))

  for s in range(n_seg):
    if seg_len >= WINDOW:
      nk_ref[s] = kvf_ref[(s + 1) * seg_len - WINDOW:(s + 1) * seg_len, 0:KV_WIDTH]
      nv_ref[s] = kvf_ref[(s + 1) * seg_len - WINDOW:(s + 1) * seg_len, KV_WIDTH:2 * KV_WIDTH]
    else:
      nk_ref[s, 0:WINDOW - seg_len, :] = ck_ref[s, seg_len:WINDOW, :]
      nv_ref[s, 0:WINDOW - seg_len, :] = cv_ref[s, seg_len:WINDOW, :]
      nk_ref[s, WINDOW - seg_len:WINDOW, :] = kvf_ref[s * seg_len:(s + 1) * seg_len, 0:KV_WIDTH]
      nv_ref[s, WINDOW - seg_len:WINDOW, :] = kvf_ref[s * seg_len:(s + 1) * seg_len, KV_WIDTH:2 * KV_WIDTH]

  row_id = lax.broadcasted_iota(jnp.int32, (GROUP * CHUNK, 1), 0)
  sink_cols = []
  for kh in range(N_KV_HEADS):
    col = jnp.full((GROUP * CHUNK, 1), sink_ref[kh * GROUP + GROUP - 1], F32)
    for g in range(GROUP - 2, -1, -1):
      col = jnp.where(row_id < (g + 1) * CHUNK, sink_ref[kh * GROUP + g], col)
    sink_cols.append(col)
  n_items = rows // CHUNK
  assert n_seg == 1 or seg_len == CHUNK
  band_step = CHUNK if n_seg == 1 else b_stride

  key_row = lax.broadcasted_iota(jnp.int32, (1, BAND), 1)

  def attend(item):
    q0 = item * CHUNK
    b0 = item * band_step
    masked_keys = 0 if has_state else max(WINDOW - item * CHUNK, 0)
    if masked_keys:
      start_mask = jnp.where(seq_start & (key_row < masked_keys), NEG, 0.0)
    for kh in range(N_KV_HEADS):
      qs = jnp.concatenate(
          [q_ref[kh * GROUP + g, q0:q0 + CHUNK, :] for g in range(GROUP)], axis=0)
      kband = kb_ref[kh, b0:b0 + BAND, :]
      vband = vb_ref[kh, b0:b0 + BAND, :]
      s = lax.dot_general(qs, kband, (((1,), (1,)), ((), ())), preferred_element_type=F32)
      s = s + bias_ref[kh]
      if masked_keys:
        s = s + start_mask
      sink = sink_cols[kh]
      m = jnp.maximum(jnp.max(s, axis=-1, keepdims=True), sink)
      p = jnp.exp(s - m)
      denom = jnp.sum(p, axis=-1, keepdims=True) + jnp.exp(sink - m)
      o = jnp.dot(p.astype(BF16), vband, preferred_element_type=F32) * (1.0 / denom)
      o = jnp.concatenate([o[g * CHUNK:(g + 1) * CHUNK, :] for g in range(GROUP)], axis=1)
      mixed_ref[q0:q0 + CHUNK, CONV_WIDTH + kh * GROUP * HEAD_DIM:
                CONV_WIDTH + (kh + 1) * GROUP * HEAD_DIM] = o.astype(BF16)

  def conv_group(c0):
    cols = slice(c0, c0 + CONV_COLS)
    cb_cols, cc_cols, cu_cols = (slice(off + c0, off + c0 + CONV_COLS) for off in (OFF_CB, OFF_CC, OFF_CU))
    for s in range(n_seg):
      for r in range(0, seg_len, CONV_ROWS):
        zr = s * seg_len + r
        gr = s * g_stride + G_PAD + r
        gs_ref[gr:gr + CONV_ROWS, cols] = (
            zc_ref[zr:zr + CONV_ROWS, cc_cols] * zc_ref[zr:zr + CONV_ROWS, cu_cols])
    for s in range(n_seg):
      for r in range(0, seg_len, CONV_ROWS):
        zr = s * seg_len + r
        gr = s * g_stride + G_PAD + r
        conv = convw_ref[0:1, cols] * gs_ref[gr - 2:gr - 2 + CONV_ROWS, cols]
        conv = conv + convw_ref[1:2, cols] * gs_ref[gr - 1:gr - 1 + CONV_ROWS, cols]
        conv = conv + convw_ref[2:3, cols] * gs_ref[gr:gr + CONV_ROWS, cols]
        mixed_ref[zr:zr + CONV_ROWS, cols] = (zc_ref[zr:zr + CONV_ROWS, cb_cols] * conv).astype(BF16)

  pieces = [off + c0 for c0 in range(0, CONV_WIDTH, CONV_COLS) for off in (OFF_CB, OFF_CC, OFF_CU)]
  assert len(pieces) <= n_items
  for item in range(n_items):
    if item < len(pieces):
      cols = slice(pieces[item], pieces[item] + CONV_COLS)
      zc_ref[:, cols] = jnp.dot(h_ref[...], win_ref[:, cols], preferred_element_type=F32)
    attend(item)
    if item < len(pieces) and item % 3 == 2:
      conv_group(pieces[item] - OFF_CU)

  for s in range(n_seg):
    g_end = s * g_stride + G_PAD + seg_len
    nconv_ref[s] = gs_ref[g_end - (CONV_K - 1):g_end, :]


def _cast_specs(weights, n_steps, step_of):
  specs, shapes = [], []
  for w in weights:
    assert w.shape[0] % (n_steps * BF16_ROWS) == 0
    specs.append(pl.BlockSpec((w.shape[0] // n_steps, w.shape[1]), lambda *ids: (step_of(*ids), 0)))
    shapes.append(jax.ShapeDtypeStruct(w.shape, BF16))
  return specs, shapes


def _mixer_call(x, state, cache_k, cache_v, bkt, rel_table, sinks, g_pre, w_in, conv_w, *, has_state,
                cast_weights=()):
  n_seq, seq_len, _ = x.shape
  if has_state:
    n_seg, seg_len = MIX_ROWS // seq_len, seq_len
    grid = (n_seq // n_seg, 1)
    x = x.reshape(n_seq // n_seg, MIX_ROWS, D_MODEL)
    x_map = lambda b, s: (b, 0, 0)
    seq_map = lambda b, s: (b, 0, 0)
  else:
    n_seg, seg_len = 1, MIX_ROWS
    grid = (n_seq, seq_len // MIX_ROWS)
    x_map = lambda b, s: (b, s, 0)
    seq_map = lambda b, s: (b, 0, 0)
  const2 = lambda b, s: (0, 0)
  once = pl.Buffered(1)

  smem = pl.BlockSpec(memory_space=pltpu.SMEM)
  in_specs = [
      smem, smem,
      pl.BlockSpec((CHUNK, BAND), const2, pipeline_mode=once),
      pl.BlockSpec((None, MIX_ROWS, D_MODEL), x_map, pipeline_mode=once if has_state else None),
      pl.BlockSpec((1, D_MODEL), const2, pipeline_mode=once),
      pl.BlockSpec((D_MODEL, IN_COLS), const2, pipeline_mode=once),
      pl.BlockSpec((CONV_K, CONV_WIDTH), const2, pipeline_mode=once),
  ]
  args = [rel_table, sinks, bkt, x, g_pre, w_in, conv_w]
  if has_state:
    in_specs += [
        pl.BlockSpec((n_seg, CONV_K - 1, CONV_WIDTH), seq_map),
        pl.BlockSpec((n_seg, WINDOW, KV_WIDTH), seq_map, pipeline_mode=once),
        pl.BlockSpec((n_seg, WINDOW, KV_WIDTH), seq_map, pipeline_mode=once),
    ]
    args += [state, cache_k, cache_v]
  cast_specs, cast_shapes = _cast_specs(cast_weights, grid[0] * grid[1], lambda b, s: b * grid[1] + s)
  in_specs += cast_specs
  args += list(cast_weights)
  out_shape = (
      jax.ShapeDtypeStruct(x.shape, BF16),
      jax.ShapeDtypeStruct((n_seq, CONV_K - 1, CONV_WIDTH), F32),
      jax.ShapeDtypeStruct((n_seq, WINDOW, KV_WIDTH), F32),
      jax.ShapeDtypeStruct((n_seq, WINDOW, KV_WIDTH), F32),
      *cast_shapes,
  )
  out_specs = (
      pl.BlockSpec((None, MIX_ROWS, D_MODEL), x_map),
      pl.BlockSpec((n_seg, CONV_K - 1, CONV_WIDTH), seq_map),
      pl.BlockSpec((n_seg, WINDOW, KV_WIDTH), seq_map),
      pl.BlockSpec((n_seg, WINDOW, KV_WIDTH), seq_map),
      *cast_specs,
  )
  band_rows = n_seg * (WINDOW + seg_len)
  scratch = [
      pltpu.VMEM((N_KV_HEADS, GROUP * CHUNK, BAND), F32),
      pltpu.VMEM((MIX_ROWS, D_MODEL), BF16),
      pltpu.VMEM((MIX_ROWS, 3 * CONV_WIDTH), F32),
      pltpu.VMEM((N_HEADS, MIX_ROWS, HEAD_DIM), BF16),
      pltpu.VMEM((MIX_ROWS, 2 * KV_WIDTH), F32),
      pltpu.VMEM((N_KV_HEADS, band_rows, HEAD_DIM), BF16),
      pltpu.VMEM((N_KV_HEADS, band_rows, HEAD_DIM), BF16),
      pltpu.VMEM((n_seg * (G_PAD + seg_len), CONV_WIDTH), F32),
  ]
  kernel = functools.partial(_mixer_kernel, n_seg=n_seg, seg_len=seg_len, has_state=has_state,
                             n_cast=len(cast_weights))
  return pl.pallas_call(
      kernel,
      grid=grid,
      in_specs=in_specs,
      out_specs=out_specs,
      out_shape=out_shape,
      scratch_shapes=scratch,
      compiler_params=pltpu.CompilerParams(
          dimension_semantics=("arbitrary", "arbitrary"),
          vmem_limit_bytes=V7X_VMEM_LIMIT_BYTES),
      name="mixer_state" if has_state else "mixer_stream",
  )(*args)


def _outproj_kernel(mixed_ref, x_ref, wout_ref, gpm_ref, gpf_ref, *rest, n_cast):
  rest = list(rest)
  cast_src, rest = rest[:n_cast], rest[n_cast:]
  (x1_ref, hn_ref), rest = rest[:2], rest[2:]
  cast_dst, (acc_ref,) = rest[:n_cast], rest[n_cast:]
  for src, dst in zip(cast_src, cast_dst):
    dst[...] = src[...].astype(BF16)
  gpm = gpm_ref[...]
  gpf = gpf_ref[...]
  part = mixed_ref.shape[0] // OUT_PARTS
  for r0 in range(0, mixed_ref.shape[0], part):
    acc_ref[r0:r0 + part, :] = jnp.dot(
        mixed_ref[r0:r0 + part, :], wout_ref[...], preferred_element_type=F32)
    for r in range(r0, r0 + part, NORM_ROWS):
      x1 = x_ref[r:r + NORM_ROWS, :] + _rms_scale(acc_ref[r:r + NORM_ROWS, :], gpm)
      x1_ref[r:r + NORM_ROWS, :] = x1
      hn_ref[r:r + NORM_ROWS, :] = _rms_scale(x1, gpf).astype(BF16)


def _outproj_call(mixed, x, w_out, g_post_mix, g_pre_ffn, cast_weights=()):
  rows = x.shape[0]
  row_map = lambda i: (i, 0)
  const2 = lambda i: (0, 0)
  once = pl.Buffered(1)
  cast_specs, cast_shapes = _cast_specs(cast_weights, rows // OUT_ROWS, lambda i: i)
  return pl.pallas_call(
      functools.partial(_outproj_kernel, n_cast=len(cast_weights)),
      grid=(rows // OUT_ROWS,),
      in_specs=[
          pl.BlockSpec((OUT_ROWS, D_MODEL), row_map),
          pl.BlockSpec((OUT_ROWS, D_MODEL), row_map),
          pl.BlockSpec((D_MODEL, D_MODEL), const2, pipeline_mode=once),
          pl.BlockSpec((1, D_MODEL), const2, pipeline_mode=once),
          pl.BlockSpec((1, D_MODEL), const2, pipeline_mode=once),
          *cast_specs,
      ],
      out_specs=(pl.BlockSpec((OUT_ROWS, D_MODEL), row_map), pl.BlockSpec((OUT_ROWS, D_MODEL), row_map),
                 *cast_specs),
      out_shape=(jax.ShapeDtypeStruct((rows, D_MODEL), F32), jax.ShapeDtypeStruct((rows, D_MODEL), BF16),
                 *cast_shapes),
      scratch_shapes=[pltpu.VMEM((OUT_ROWS, D_MODEL), F32)],
      compiler_params=pltpu.CompilerParams(
          dimension_semantics=("arbitrary",), vmem_limit_bytes=V7X_VMEM_LIMIT_BYTES),
      name="outproj",
  )(mixed, x, w_out, g_post_mix, g_pre_ffn, *cast_weights)


def _ffn_tile(i, j):
  n_tiles = pl.cdiv(D_FF, FFN_TILE)
  step = jnp.where(i % 2 == 0, j, n_tiles - 1 - j)
  mid = n_tiles // 2 - 1
  return jnp.where(step < mid, step, jnp.where(step == mid, n_tiles - 1, step - 1))


def _ffn_kernel(x1_ref, hn_ref, gqf_ref, wg_ref, wu_ref, wd_ref, y_ref):
  j = pl.program_id(1)
  tile = _ffn_tile(pl.program_id(0), j)
  n_tiles = pl.cdiv(D_FF, FFN_TILE)
  rows = y_ref.shape[0]

  def swiglu_down(cols):
    hn = hn_ref[...]
    gate = jnp.dot(hn, wg_ref[:, 0:cols], preferred_element_type=F32)
    up = jnp.dot(hn, wu_ref[:, 0:cols], preferred_element_type=F32)
    mid = (gate * (1.0 / (1.0 + jnp.exp(-gate))) * up).astype(BF16)
    return jnp.dot(mid, wd_ref[0:cols, :], preferred_element_type=F32)

  assert n_tiles >= 4
  full = tile < n_tiles - 1

  @pl.when(full & (j == 0))
  def _():
    y_ref[...] = swiglu_down(FFN_TILE)

  @pl.when(full & (j > 0))
  def _():
    y_ref[...] += swiglu_down(FFN_TILE)

  @pl.when(jnp.logical_not(full))
  def _():
    y_ref[...] += swiglu_down(D_FF - (n_tiles - 1) * FFN_TILE)

  @pl.when(j == n_tiles - 1)
  def _():
    gqf = gqf_ref[...]
    for r in range(0, rows, NORM_ROWS):
      y_ref[r:r + NORM_ROWS, :] = x1_ref[r:r + NORM_ROWS, :] + _rms_scale(y_ref[r:r + NORM_ROWS, :], gqf)


def _ffn_call(x1, hn, g_post_ffn, w_gate, w_up, w_down):
  rows = x1.shape[0]
  row_map = lambda i, j: (i, 0)
  return pl.pallas_call(
      _ffn_kernel,
      grid=(rows // FFN_ROWS, pl.cdiv(D_FF, FFN_TILE)),
      in_specs=[
          pl.BlockSpec((FFN_ROWS, D_MODEL), row_map),
          pl.BlockSpec((FFN_ROWS, D_MODEL), row_map),
          pl.BlockSpec((1, D_MODEL), lambda i, j: (0, 0), pipeline_mode=pl.Buffered(1)),
          pl.BlockSpec((D_MODEL, FFN_TILE), lambda i, j: (0, _ffn_tile(i, j))),
          pl.BlockSpec((D_MODEL, FFN_TILE), lambda i, j: (0, _ffn_tile(i, j))),
          pl.BlockSpec((FFN_TILE, D_MODEL), lambda i, j: (_ffn_tile(i, j), 0)),
      ],
      out_specs=pl.BlockSpec((FFN_ROWS, D_MODEL), row_map),
      out_shape=jax.ShapeDtypeStruct((rows, D_MODEL), F32),
      compiler_params=pltpu.CompilerParams(
          dimension_semantics=("arbitrary", "arbitrary"),
          vmem_limit_bytes=V7X_VMEM_LIMIT_BYTES),
      name="ffn",
  )(x1, hn, g_post_ffn, w_gate, w_up, w_down)


def kernel(x_prompt, x_sample, state_conv, cache_k, cache_v, rel_table, g_pre_mix, w_in, conv_w,
           attn_sinks, w_out, g_post_mix, g_pre_ffn, w_gate, w_up, w_down, g_post_ffn):
  depth, n_dec, cache_len = cache_k.shape[:3]
  batch, seq_len, _ = x_prompt.shape
  assert depth == 1 and cache_len == WINDOW and x_sample.shape[1] == CHUNK
  assert seq_len % MIX_ROWS == 0 and MIX_ROWS % CHUNK == 0

  rel = (jnp.arange(BAND) - WINDOW)[None, :] - jnp.arange(CHUNK)[:, None]
  bkt = _t5_bucket(rel).astype(jnp.int32)

  row = lambda g: g[0].reshape(1, D_MODEL)
  win = w_in[0].astype(BF16)
  mix_args = (bkt, rel_table, attn_sinks[0], row(g_pre_mix), win, conv_w[0])
  mixed_p, conv_p, k_p, v_p, wg, wu, wo = _mixer_call(
      x_prompt, None, None, None, *mix_args, has_state=False, cast_weights=(w_gate[0], w_up[0], w_out[0]))
  mixed_s, conv_s, k_s, v_s = _mixer_call(
      x_sample, state_conv[0], cache_k[0].reshape(n_dec, WINDOW, KV_WIDTH),
      cache_v[0].reshape(n_dec, WINDOW, KV_WIDTH), *mix_args, has_state=True)

  out_args = (wo, row(g_post_mix), row(g_pre_ffn))
  x1_p, hn_p, wd = _outproj_call(
      mixed_p.reshape(-1, D_MODEL), x_prompt.reshape(-1, D_MODEL), *out_args, cast_weights=(w_down[0],))
  x1_s, hn_s = _outproj_call(mixed_s.reshape(-1, D_MODEL), x_sample.reshape(-1, D_MODEL), *out_args)
  ffn_args = (row(g_post_ffn), wg, wu, wd)
  y_p = _ffn_call(x1_p, hn_p, *ffn_args)
  y_s = _ffn_call(x1_s, hn_s, *ffn_args)

  heads = lambda a: a.reshape(1, a.shape[0], WINDOW, N_KV_HEADS, HEAD_DIM)
  return (y_p.reshape(x_prompt.shape), y_s.reshape(x_sample.shape),
          conv_p[None], heads(k_p), heads(v_p), conv_s[None], heads(k_s), heads(v_s))
```

```python
import functools
import math

import jax
import jax.numpy as jnp
from jax import lax
from jax.experimental import pallas as pl
from jax.experimental.pallas import tpu as pltpu

D_MODEL = 2048
CHUNK = 64
HEAD_DIM = 64
CONV_WIDTH = D_MODEL // 2
CONV_K = 3
N_HEADS = (D_MODEL - CONV_WIDTH) // HEAD_DIM
N_KV_HEADS = 4
GROUP = N_HEADS // N_KV_HEADS
ATTN_WIDTH = N_HEADS * HEAD_DIM
KV_WIDTH = N_KV_HEADS * HEAD_DIM
IN_COLS = 3 * CONV_WIDTH + ATTN_WIDTH + 2 * KV_WIDTH
WINDOW = 128
BAND = WINDOW + CHUNK
NUM_BUCKETS = 32
MAX_DISTANCE = 128
D_FF = -(-8 * D_MODEL // (3 * 256)) * 256
EPS = 1e-6
NEG = -1e30

OFF_CB, OFF_CC, OFF_CU = 0, CONV_WIDTH, 2 * CONV_WIDTH
OFF_Q = 3 * CONV_WIDTH
OFF_KV = OFF_Q + ATTN_WIDTH

SUBLANES = 8
BF16_ROWS = 16
V7X_VMEM_LIMIT_BYTES = 61 << 20
MIX_ROWS = 512
OUT_ROWS = 512
OUT_PARTS = 4
FFN_ROWS = 1024
FFN_TILE = 512
CONV_COLS = 512
NORM_ROWS = 16
NORM_UNROLL = 8
CONV_ROWS = 32
G_PAD = SUBLANES

F32 = jnp.float32
BF16 = jnp.bfloat16


def _t5_bucket(rel):
  half = NUM_BUCKETS // 2
  ret = jnp.where(rel > 0, half, 0)
  n = jnp.abs(rel)
  max_exact = half // 2
  nf = jnp.maximum(n, 1).astype(jnp.float32)
  large = max_exact + (jnp.log(nf / max_exact) / math.log(MAX_DISTANCE / max_exact)
                       * (half - max_exact)).astype(jnp.int32)
  large = jnp.minimum(large, half - 1)
  return ret + jnp.where(n < max_exact, n, large)


def _rms_scale(y, gain):
  return y * lax.rsqrt(jnp.mean(y * y, axis=-1, keepdims=True) + EPS) * gain


def _mixer_kernel(tbl_ref, sink_ref, bkt_ref, x_ref, gpre_ref, win_ref, convw_ref, *rest,
                  n_seg, seg_len, has_state, cast_groups):
  rest = list(rest)
  if has_state:
    state_ref, ck_ref, cv_ref = rest[:3]
    rest = rest[3:]
  cast_src, rest = rest[:sum(cast_groups)], rest[sum(cast_groups):]
  (mixed_ref, nconv_ref, nk_ref, nv_ref), rest = rest[:4], rest[4:]
  cast_dst, rest = rest[:len(cast_groups)], rest[len(cast_groups):]
  bias_ref, h_ref, zc_ref, q_ref, kvf_ref, kb_ref, vb_ref, gs_ref = rest
  rows = n_seg * seg_len
  g_stride = G_PAD + seg_len
  b_stride = WINDOW + seg_len
  first_call = (pl.program_id(0) == 0) & (pl.program_id(1) == 0)
  seq_start = pl.program_id(1) == 0

  @pl.when(first_call)
  def _():
    bkt = bkt_ref[...]
    for h in range(N_HEADS):
      acc = jnp.zeros((CHUNK, BAND), F32)
      for j in range(NUM_BUCKETS):
        acc = jnp.where(bkt == j, tbl_ref[j, h], acc)
      g = h % GROUP
      bias_ref[h // GROUP, g * CHUNK:(g + 1) * CHUNK, :] = acc

  if has_state:
    for s in range(n_seg):
      gs_ref[s * g_stride + G_PAD - (CONV_K - 1):s * g_stride + G_PAD, :] = state_ref[s]
      for kh in range(N_KV_HEADS):
        kb_ref[kh, s * b_stride:s * b_stride + WINDOW, :] = (
            ck_ref[s, :, kh * HEAD_DIM:(kh + 1) * HEAD_DIM].astype(BF16))
        vb_ref[kh, s * b_stride:s * b_stride + WINDOW, :] = (
            cv_ref[s, :, kh * HEAD_DIM:(kh + 1) * HEAD_DIM].astype(BF16))
  else:
    @pl.when(seq_start)
    def _():
      gs_ref[0:G_PAD, :] = jnp.zeros((G_PAD, CONV_WIDTH), F32)
      kb_ref[:, 0:WINDOW, :] = jnp.zeros((N_KV_HEADS, WINDOW, HEAD_DIM), BF16)
      vb_ref[:, 0:WINDOW, :] = jnp.zeros((N_KV_HEADS, WINDOW, HEAD_DIM), BF16)

    @pl.when(jnp.logical_not(seq_start))
    def _():
      gs_ref[0:G_PAD, :] = gs_ref[seg_len:seg_len + G_PAD, :]
      kb_ref[:, 0:WINDOW, :] = kb_ref[:, seg_len:seg_len + WINDOW, :]
      vb_ref[:, 0:WINDOW, :] = vb_ref[:, seg_len:seg_len + WINDOW, :]

  gpre = gpre_ref[...]

  def norm_body(i, carry):
    r = pl.multiple_of(i * NORM_ROWS, NORM_ROWS)
    h_ref[pl.ds(r, NORM_ROWS), :] = _rms_scale(x_ref[pl.ds(r, NORM_ROWS), :], gpre).astype(BF16)
    return carry

  lax.fori_loop(0, rows // NORM_ROWS, norm_body, 0, unroll=NORM_UNROLL)

  _cast_all(cast_src, cast_dst, cast_groups)

  for c0 in range(0, ATTN_WIDTH, CONV_COLS):
    zc_ref[:, 0:CONV_COLS] = jnp.dot(
        h_ref[...], win_ref[:, OFF_Q + c0:OFF_Q + c0 + CONV_COLS], preferred_element_type=F32)
    for hh in range(CONV_COLS // HEAD_DIM):
      q_ref[c0 // HEAD_DIM + hh] = (
          zc_ref[:, hh * HEAD_DIM:(hh + 1) * HEAD_DIM] * (HEAD_DIM ** -0.5)).astype(BF16)

  kvf_ref[...] = jnp.dot(h_ref[...], win_ref[:, OFF_KV:OFF_KV + 2 * KV_WIDTH], preferred_element_type=F32)
  for s in range(n_seg):
    for kh in range(N_KV_HEADS):
      kb_ref[kh, s * b_stride + WINDOW:(s + 1) * b_stride, :] = (
          kvf_ref[s * seg_len:(s + 1) * seg_len, kh * HEAD_DIM:(kh + 1) * HEAD_DIM].astype(BF16))
      vb_ref[kh, s * b_stride + WINDOW:(s + 1) * b_stride, :] = (
          kvf_ref[s * seg_len:(s + 1) * seg_len,
                  KV_WIDTH + kh * HEAD_DIM:KV_WIDTH + (kh + 1) * HEAD_DIM].astype(BF16))

  for s in range(n_seg):
    if seg_len >= WINDOW:
      nk_ref[s] = kvf_ref[(s + 1) * seg_len - WINDOW:(s + 1) * seg_len, 0:KV_WIDTH]
      nv_ref[s] = kvf_ref[(s + 1) * seg_len - WINDOW:(s + 1) * seg_len, KV_WIDTH:2 * KV_WIDTH]
    else:
      nk_ref[s, 0:WINDOW - seg_len, :] = ck_ref[s, seg_len:WINDOW, :]
      nv_ref[s, 0:WINDOW - seg_len, :] = cv_ref[s, seg_len:WINDOW, :]
      nk_ref[s, WINDOW - seg_len:WINDOW, :] = kvf_ref[s * seg_len:(s + 1) * seg_len, 0:KV_WIDTH]
      nv_ref[s, WINDOW - seg_len:WINDOW, :] = kvf_ref[s * seg_len:(s + 1) * seg_len, KV_WIDTH:2 * KV_WIDTH]

  row_id = lax.broadcasted_iota(jnp.int32, (GROUP * CHUNK, 1), 0)
  sink_cols = []
  for kh in range(N_KV_HEADS):
    col = jnp.full((GROUP * CHUNK, 1), sink_ref[kh * GROUP + GROUP - 1], F32)
    for g in range(GROUP - 2, -1, -1):
      col = jnp.where(row_id < (g + 1) * CHUNK, sink_ref[kh * GROUP + g], col)
    sink_cols.append(col)
  n_items = rows // CHUNK
  assert n_seg == 1 or seg_len == CHUNK
  band_step = CHUNK if n_seg == 1 else b_stride

  key_row = lax.broadcasted_iota(jnp.int32, (1, BAND), 1)

  def attend(item):
    q0 = item * CHUNK
    b0 = item * band_step
    masked_keys = 0 if has_state else max(WINDOW - item * CHUNK, 0)
    if masked_keys:
      start_mask = jnp.where(seq_start & (key_row < masked_keys), NEG, 0.0)
    for kh in range(N_KV_HEADS):
      qs = jnp.concatenate(
          [q_ref[kh * GROUP + g, q0:q0 + CHUNK, :] for g in range(GROUP)], axis=0)
      kband = kb_ref[kh, b0:b0 + BAND, :]
      vband = vb_ref[kh, b0:b0 + BAND, :]
      s = lax.dot_general(qs, kband, (((1,), (1,)), ((), ())), preferred_element_type=F32)
      s = s + bias_ref[kh]
      if masked_keys:
        s = s + start_mask
      sink = sink_cols[kh]
      m = jnp.maximum(jnp.max(s, axis=-1, keepdims=True), sink)
      p = jnp.exp(s - m)
      denom = jnp.sum(p, axis=-1, keepdims=True) + jnp.exp(sink - m)
      o = jnp.dot(p.astype(BF16), vband, preferred_element_type=F32) * (1.0 / denom)
      o = jnp.concatenate([o[g * CHUNK:(g + 1) * CHUNK, :] for g in range(GROUP)], axis=1)
      mixed_ref[q0:q0 + CHUNK, CONV_WIDTH + kh * GROUP * HEAD_DIM:
                CONV_WIDTH + (kh + 1) * GROUP * HEAD_DIM] = o.astype(BF16)

  def conv_group(c0):
    cols = slice(c0, c0 + CONV_COLS)
    cb_cols, cc_cols, cu_cols = (slice(off + c0, off + c0 + CONV_COLS) for off in (OFF_CB, OFF_CC, OFF_CU))
    for s in range(n_seg):
      for r in range(0, seg_len, CONV_ROWS):
        zr = s * seg_len + r
        gr = s * g_stride + G_PAD + r
        gs_ref[gr:gr + CONV_ROWS, cols] = (
            zc_ref[zr:zr + CONV_ROWS, cc_cols] * zc_ref[zr:zr + CONV_ROWS, cu_cols])
    for s in range(n_seg):
      for r in range(0, seg_len, CONV_ROWS):
        zr = s * seg_len + r
        gr = s * g_stride + G_PAD + r
        conv = convw_ref[0:1, cols] * gs_ref[gr - 2:gr - 2 + CONV_ROWS, cols]
        conv = conv + convw_ref[1:2, cols] * gs_ref[gr - 1:gr - 1 + CONV_ROWS, cols]
        conv = conv + convw_ref[2:3, cols] * gs_ref[gr:gr + CONV_ROWS, cols]
        mixed_ref[zr:zr + CONV_ROWS, cols] = (zc_ref[zr:zr + CONV_ROWS, cb_cols] * conv).astype(BF16)

  pieces = [off + c0 for c0 in range(0, CONV_WIDTH, CONV_COLS) for off in (OFF_CB, OFF_CC, OFF_CU)]
  assert len(pieces) <= n_items
  for item in range(n_items):
    if item < len(pieces):
      cols = slice(pieces[item], pieces[item] + CONV_COLS)
      zc_ref[:, cols] = jnp.dot(h_ref[...], win_ref[:, cols], preferred_element_type=F32)
    attend(item)
    if item < len(pieces) and item % 3 == 2:
      conv_group(pieces[item] - OFF_CU)

  for s in range(n_seg):
    g_end = s * g_stride + G_PAD + seg_len
    nconv_ref[s] = gs_ref[g_end - (CONV_K - 1):g_end, :]


def _cast_specs(groups, n_steps, step_of):
  in_specs, out_specs, out_shapes = [], [], []
  for group in groups:
    rows, cols = group[0].shape
    assert rows % (n_steps * BF16_ROWS) == 0 and all(w.shape == (rows, cols) for w in group)
    index_map = lambda *ids: (step_of(*ids), 0)
    in_specs += [pl.BlockSpec((rows // n_steps, cols), index_map) for _ in group]
    out_specs.append(pl.BlockSpec((rows // n_steps, cols * len(group)), index_map))
    out_shapes.append(jax.ShapeDtypeStruct((rows, cols * len(group)), BF16))
  return in_specs, out_specs, out_shapes


def _cast_group(src_refs, dst_ref):
  n = len(src_refs)
  for j in range(src_refs[0].shape[1] // FFN_TILE):
    for k, src in enumerate(src_refs):
      dst_ref[:, (j * n + k) * FFN_TILE:(j * n + k + 1) * FFN_TILE] = (
          src[:, j * FFN_TILE:(j + 1) * FFN_TILE].astype(BF16))


def _cast_all(src_refs, dst_refs, group_sizes):
  for size, dst in zip(group_sizes, dst_refs):
    _cast_group(src_refs[:size], dst)
    src_refs = src_refs[size:]


def _mixer_call(x, state, cache_k, cache_v, bkt, rel_table, sinks, g_pre, w_in, conv_w, *, has_state,
                cast_weights=()):
  n_seq, seq_len, _ = x.shape
  if has_state:
    n_seg, seg_len = MIX_ROWS // seq_len, seq_len
    grid = (n_seq // n_seg, 1)
    x = x.reshape(n_seq // n_seg, MIX_ROWS, D_MODEL)
    x_map = lambda b, s: (b, 0, 0)
    seq_map = lambda b, s: (b, 0, 0)
  else:
    n_seg, seg_len = 1, MIX_ROWS
    grid = (n_seq, seq_len // MIX_ROWS)
    x_map = lambda b, s: (b, s, 0)
    seq_map = lambda b, s: (b, 0, 0)
  const2 = lambda b, s: (0, 0)
  once = pl.Buffered(1)

  smem = pl.BlockSpec(memory_space=pltpu.SMEM)
  in_specs = [
      smem, smem,
      pl.BlockSpec((CHUNK, BAND), const2, pipeline_mode=once),
      pl.BlockSpec((None, MIX_ROWS, D_MODEL), x_map, pipeline_mode=once if has_state else None),
      pl.BlockSpec((1, D_MODEL), const2, pipeline_mode=once),
      pl.BlockSpec((D_MODEL, IN_COLS), const2, pipeline_mode=once),
      pl.BlockSpec((CONV_K, CONV_WIDTH), const2, pipeline_mode=once),
  ]
  args = [rel_table, sinks, bkt, x, g_pre, w_in, conv_w]
  if has_state:
    in_specs += [
        pl.BlockSpec((n_seg, CONV_K - 1, CONV_WIDTH), seq_map),
        pl.BlockSpec((n_seg, WINDOW, KV_WIDTH), seq_map, pipeline_mode=once),
        pl.BlockSpec((n_seg, WINDOW, KV_WIDTH), seq_map, pipeline_mode=once),
    ]
    args += [state, cache_k, cache_v]
  cast_in, cast_out, cast_shapes = _cast_specs(cast_weights, grid[0] * grid[1], lambda b, s: b * grid[1] + s)
  in_specs += cast_in
  args += [w for group in cast_weights for w in group]
  out_shape = (
      jax.ShapeDtypeStruct(x.shape, BF16),
      jax.ShapeDtypeStruct((n_seq, CONV_K - 1, CONV_WIDTH), F32),
      jax.ShapeDtypeStruct((n_seq, WINDOW, KV_WIDTH), F32),
      jax.ShapeDtypeStruct((n_seq, WINDOW, KV_WIDTH), F32),
      *cast_shapes,
  )
  out_specs = (
      pl.BlockSpec((None, MIX_ROWS, D_MODEL), x_map),
      pl.BlockSpec((n_seg, CONV_K - 1, CONV_WIDTH), seq_map),
      pl.BlockSpec((n_seg, WINDOW, KV_WIDTH), seq_map),
      pl.BlockSpec((n_seg, WINDOW, KV_WIDTH), seq_map),
      *cast_out,
  )
  band_rows = n_seg * (WINDOW + seg_len)
  scratch = [
      pltpu.VMEM((N_KV_HEADS, GROUP * CHUNK, BAND), F32),
      pltpu.VMEM((MIX_ROWS, D_MODEL), BF16),
      pltpu.VMEM((MIX_ROWS, 3 * CONV_WIDTH), F32),
      pltpu.VMEM((N_HEADS, MIX_ROWS, HEAD_DIM), BF16),
      pltpu.VMEM((MIX_ROWS, 2 * KV_WIDTH), F32),
      pltpu.VMEM((N_KV_HEADS, band_rows, HEAD_DIM), BF16),
      pltpu.VMEM((N_KV_HEADS, band_rows, HEAD_DIM), BF16),
      pltpu.VMEM((n_seg * (G_PAD + seg_len), CONV_WIDTH), F32),
  ]
  kernel = functools.partial(_mixer_kernel, n_seg=n_seg, seg_len=seg_len, has_state=has_state,
                             cast_groups=tuple(len(group) for group in cast_weights))
  return pl.pallas_call(
      kernel,
      grid=grid,
      in_specs=in_specs,
      out_specs=out_specs,
      out_shape=out_shape,
      scratch_shapes=scratch,
      compiler_params=pltpu.CompilerParams(
          dimension_semantics=("arbitrary", "arbitrary"),
          vmem_limit_bytes=V7X_VMEM_LIMIT_BYTES),
      name="mixer_state" if has_state else "mixer_stream",
  )(*args)


def _outproj_kernel(mixed_ref, x_ref, wout_ref, gpm_ref, gpf_ref, *rest, cast_groups):
  rest = list(rest)
  cast_src, rest = rest[:sum(cast_groups)], rest[sum(cast_groups):]
  (x1_ref, hn_ref), rest = rest[:2], rest[2:]
  cast_dst, (acc_ref,) = rest[:len(cast_groups)], rest[len(cast_groups):]
  _cast_all(cast_src, cast_dst, cast_groups)
  gpm = gpm_ref[...]
  gpf = gpf_ref[...]
  part = mixed_ref.shape[0] // OUT_PARTS
  for r0 in range(0, mixed_ref.shape[0], part):
    acc_ref[r0:r0 + part, :] = jnp.dot(
        mixed_ref[r0:r0 + part, :], wout_ref[...], preferred_element_type=F32)
    for r in range(r0, r0 + part, NORM_ROWS):
      x1 = x_ref[r:r + NORM_ROWS, :] + _rms_scale(acc_ref[r:r + NORM_ROWS, :], gpm)
      x1_ref[r:r + NORM_ROWS, :] = x1
      hn_ref[r:r + NORM_ROWS, :] = _rms_scale(x1, gpf).astype(BF16)


def _outproj_call(mixed, x, w_out, g_post_mix, g_pre_ffn, cast_weights=()):
  rows = x.shape[0]
  row_map = lambda i: (i, 0)
  const2 = lambda i: (0, 0)
  once = pl.Buffered(1)
  cast_in, cast_out, cast_shapes = _cast_specs(cast_weights, rows // OUT_ROWS, lambda i: i)
  return pl.pallas_call(
      functools.partial(_outproj_kernel, cast_groups=tuple(len(group) for group in cast_weights)),
      grid=(rows // OUT_ROWS,),
      in_specs=[
          pl.BlockSpec((OUT_ROWS, D_MODEL), row_map),
          pl.BlockSpec((OUT_ROWS, D_MODEL), row_map),
          pl.BlockSpec((D_MODEL, D_MODEL), const2, pipeline_mode=once),
          pl.BlockSpec((1, D_MODEL), const2, pipeline_mode=once),
          pl.BlockSpec((1, D_MODEL), const2, pipeline_mode=once),
          *cast_in,
      ],
      out_specs=(pl.BlockSpec((OUT_ROWS, D_MODEL), row_map), pl.BlockSpec((OUT_ROWS, D_MODEL), row_map),
                 *cast_out),
      out_shape=(jax.ShapeDtypeStruct((rows, D_MODEL), F32), jax.ShapeDtypeStruct((rows, D_MODEL), BF16),
                 *cast_shapes),
      scratch_shapes=[pltpu.VMEM((OUT_ROWS, D_MODEL), F32)],
      compiler_params=pltpu.CompilerParams(
          dimension_semantics=("arbitrary",), vmem_limit_bytes=V7X_VMEM_LIMIT_BYTES),
      name="outproj",
  )(mixed, x, w_out, g_post_mix, g_pre_ffn, *[w for group in cast_weights for w in group])


def _ffn_tile(i, j):
  return jnp.where(i % 2 == 0, j, D_FF // FFN_TILE - 1 - j)


def _ffn_kernel(x1_ref, hn_ref, gqf_ref, wgu_ref, wd_ref, y_ref):
  j = pl.program_id(1)
  rows = y_ref.shape[0]

  def swiglu_down():
    gate_up = jnp.dot(hn_ref[...], wgu_ref[...], preferred_element_type=F32)
    gate, up = gate_up[:, :FFN_TILE], gate_up[:, FFN_TILE:]
    mid = (gate * (1.0 / (1.0 + jnp.exp(-gate))) * up).astype(BF16)
    return jnp.dot(mid, wd_ref[...], preferred_element_type=F32)

  @pl.when(j == 0)
  def _():
    y_ref[...] = swiglu_down()

  @pl.when(j > 0)
  def _():
    y_ref[...] += swiglu_down()

  @pl.when(j == pl.num_programs(1) - 1)
  def _():
    gqf = gqf_ref[...]
    for r in range(0, rows, NORM_ROWS):
      y_ref[r:r + NORM_ROWS, :] = x1_ref[r:r + NORM_ROWS, :] + _rms_scale(y_ref[r:r + NORM_ROWS, :], gqf)


def _ffn_call(x1, hn, g_post_ffn, w_gate_up, w_down):
  rows = x1.shape[0]
  assert D_FF % FFN_TILE == 0
  row_map = lambda i, j: (i, 0)
  return pl.pallas_call(
      _ffn_kernel,
      grid=(rows // FFN_ROWS, D_FF // FFN_TILE),
      in_specs=[
          pl.BlockSpec((FFN_ROWS, D_MODEL), row_map),
          pl.BlockSpec((FFN_ROWS, D_MODEL), row_map),
          pl.BlockSpec((1, D_MODEL), lambda i, j: (0, 0), pipeline_mode=pl.Buffered(1)),
          pl.BlockSpec((D_MODEL, 2 * FFN_TILE), lambda i, j: (0, _ffn_tile(i, j))),
          pl.BlockSpec((FFN_TILE, D_MODEL), lambda i, j: (_ffn_tile(i, j), 0)),
      ],
      out_specs=pl.BlockSpec((FFN_ROWS, D_MODEL), row_map),
      out_shape=jax.ShapeDtypeStruct((rows, D_MODEL), F32),
      compiler_params=pltpu.CompilerParams(
          dimension_semantics=("arbitrary", "arbitrary"),
          vmem_limit_bytes=V7X_VMEM_LIMIT_BYTES),
      name="ffn",
  )(x1, hn, g_post_ffn, w_gate_up, w_down)


def kernel(x_prompt, x_sample, state_conv, cache_k, cache_v, rel_table, g_pre_mix, w_in, conv_w,
           attn_sinks, w_out, g_post_mix, g_pre_ffn, w_gate, w_up, w_down, g_post_ffn):
  depth, n_dec, cache_len = cache_k.shape[:3]
  batch, seq_len, _ = x_prompt.shape
  assert depth == 1 and cache_len == WINDOW and x_sample.shape[1] == CHUNK
  assert seq_len % MIX_ROWS == 0 and MIX_ROWS % CHUNK == 0

  rel = (jnp.arange(BAND) - WINDOW)[None, :] - jnp.arange(CHUNK)[:, None]
  bkt = _t5_bucket(rel).astype(jnp.int32)

  row = lambda g: g[0].reshape(1, D_MODEL)
  win = w_in[0].astype(BF16)
  mix_args = (bkt, rel_table, attn_sinks[0], row(g_pre_mix), win, conv_w[0])
  mixed_p, conv_p, k_p, v_p, wgu, wo = _mixer_call(
      x_prompt, None, None, None, *mix_args, has_state=False,
      cast_weights=((w_gate[0], w_up[0]), (w_out[0],)))
  mixed_s, conv_s, k_s, v_s = _mixer_call(
      x_sample, state_conv[0], cache_k[0].reshape(n_dec, WINDOW, KV_WIDTH),
      cache_v[0].reshape(n_dec, WINDOW, KV_WIDTH), *mix_args, has_state=True)

  out_args = (wo, row(g_post_mix), row(g_pre_ffn))
  x1_p, hn_p, wd = _outproj_call(
      mixed_p.reshape(-1, D_MODEL), x_prompt.reshape(-1, D_MODEL), *out_args, cast_weights=((w_down[0],),))
  x1_s, hn_s = _outproj_call(mixed_s.reshape(-1, D_MODEL), x_sample.reshape(-1, D_MODEL), *out_args)
  ffn_args = (row(g_post_ffn), wgu, wd)
  y_p = _ffn_call(x1_p, hn_p, *ffn_args)
  y_s = _ffn_call(x1_s, hn_s, *ffn_args)

  heads = lambda a: a.reshape(1, a.shape[0], WINDOW, N_KV_HEADS, HEAD_DIM)
  return (y_p.reshape(x_prompt.shape), y_s.reshape(x_sample.shape),
          conv_p[None], heads(k_p), heads(v_p), conv_s[None], heads(k_s), heads(v_s))
```

```python
import functools
import math

import jax
import jax.numpy as jnp
from jax import lax
from jax.experimental import pallas as pl
from jax.experimental.pallas import tpu as pltpu

D_MODEL = 2048
CHUNK = 64
HEAD_DIM = 64
CONV_WIDTH = D_MODEL // 2
CONV_K = 3
N_HEADS = (D_MODEL - CONV_WIDTH) // HEAD_DIM
N_KV_HEADS = 4
GROUP = N_HEADS // N_KV_HEADS
ATTN_WIDTH = N_HEADS * HEAD_DIM
KV_WIDTH = N_KV_HEADS * HEAD_DIM
IN_COLS = 3 * CONV_WIDTH + ATTN_WIDTH + 2 * KV_WIDTH
WINDOW = 128
BAND = WINDOW + CHUNK
NUM_BUCKETS = 32
MAX_DISTANCE = 128
D_FF = -(-8 * D_MODEL // (3 * 256)) * 256
EPS = 1e-6
NEG = -1e30

OFF_CB, OFF_CC, OFF_CU = 0, CONV_WIDTH, 2 * CONV_WIDTH
OFF_Q = 3 * CONV_WIDTH
OFF_KV = OFF_Q + ATTN_WIDTH

SUBLANES = 8
BF16_ROWS = 16
V7X_VMEM_LIMIT_BYTES = 61 << 20
MIX_ROWS = 512
OUT_ROWS = 512
OUT_PARTS = 4
FFN_ROWS = 1024
FFN_TILE = 512
FFN_LAST_PARTS = 4
CONV_COLS = 512
NORM_ROWS = 16
NORM_UNROLL = 8
CONV_ROWS = 32
G_PAD = SUBLANES

F32 = jnp.float32
BF16 = jnp.bfloat16


def _t5_bucket(rel):
  half = NUM_BUCKETS // 2
  ret = jnp.where(rel > 0, half, 0)
  n = jnp.abs(rel)
  max_exact = half // 2
  nf = jnp.maximum(n, 1).astype(jnp.float32)
  large = max_exact + (jnp.log(nf / max_exact) / math.log(MAX_DISTANCE / max_exact)
                       * (half - max_exact)).astype(jnp.int32)
  large = jnp.minimum(large, half - 1)
  return ret + jnp.where(n < max_exact, n, large)


def _rms_scale(y, gain):
  return y * lax.rsqrt(jnp.mean(y * y, axis=-1, keepdims=True) + EPS) * gain


def _mixer_kernel(tbl_ref, sink_ref, bkt_ref, x_ref, gpre_ref, win_ref, convw_ref, *rest,
                  n_seg, seg_len, has_state, cast_groups):
  rest = list(rest)
  if has_state:
    state_ref, ck_ref, cv_ref = rest[:3]
    rest = rest[3:]
  cast_src, rest = rest[:sum(cast_groups)], rest[sum(cast_groups):]
  (mixed_ref, nconv_ref, nk_ref, nv_ref), rest = rest[:4], rest[4:]
  cast_dst, rest = rest[:len(cast_groups)], rest[len(cast_groups):]
  bias_ref, h_ref, zc_ref, q_ref, kvf_ref, kb_ref, vb_ref, gs_ref = rest
  rows = n_seg * seg_len
  g_stride = G_PAD + seg_len
  b_stride = WINDOW + seg_len
  first_call = (pl.program_id(0) == 0) & (pl.program_id(1) == 0)
  seq_start = pl.program_id(1) == 0

  @pl.when(first_call)
  def _():
    bkt = bkt_ref[...]
    for h in range(N_HEADS):
      acc = jnp.zeros((CHUNK, BAND), F32)
      for j in range(NUM_BUCKETS):
        acc = jnp.where(bkt == j, tbl_ref[j, h], acc)
      g = h % GROUP
      bias_ref[h // GROUP, g * CHUNK:(g + 1) * CHUNK, :] = acc

  if has_state:
    for s in range(n_seg):
      gs_ref[s * g_stride + G_PAD - (CONV_K - 1):s * g_stride + G_PAD, :] = state_ref[s]
      for kh in range(N_KV_HEADS):
        kb_ref[kh, s * b_stride:s * b_stride + WINDOW, :] = (
            ck_ref[s, :, kh * HEAD_DIM:(kh + 1) * HEAD_DIM].astype(BF16))
        vb_ref[kh, s * b_stride:s * b_stride + WINDOW, :] = (
            cv_ref[s, :, kh * HEAD_DIM:(kh + 1) * HEAD_DIM].astype(BF16))
  else:
    @pl.when(seq_start)
    def _():
      gs_ref[0:G_PAD, :] = jnp.zeros((G_PAD, CONV_WIDTH), F32)
      kb_ref[:, 0:WINDOW, :] = jnp.zeros((N_KV_HEADS, WINDOW, HEAD_DIM), BF16)
      vb_ref[:, 0:WINDOW, :] = jnp.zeros((N_KV_HEADS, WINDOW, HEAD_DIM), BF16)

    @pl.when(jnp.logical_not(seq_start))
    def _():
      gs_ref[0:G_PAD, :] = gs_ref[seg_len:seg_len + G_PAD, :]
      kb_ref[:, 0:WINDOW, :] = kb_ref[:, seg_len:seg_len + WINDOW, :]
      vb_ref[:, 0:WINDOW, :] = vb_ref[:, seg_len:seg_len + WINDOW, :]

  gpre = gpre_ref[...]

  def norm_body(i, carry):
    r = pl.multiple_of(i * NORM_ROWS, NORM_ROWS)
    h_ref[pl.ds(r, NORM_ROWS), :] = _rms_scale(x_ref[pl.ds(r, NORM_ROWS), :], gpre).astype(BF16)
    return carry

  lax.fori_loop(0, rows // NORM_ROWS, norm_body, 0, unroll=NORM_UNROLL)

  _cast_all(cast_src, cast_dst, cast_groups)

  for c0 in range(0, ATTN_WIDTH, CONV_COLS):
    zc_ref[:, 0:CONV_COLS] = jnp.dot(
        h_ref[...], win_ref[:, OFF_Q + c0:OFF_Q + c0 + CONV_COLS], preferred_element_type=F32)
    for hh in range(CONV_COLS // HEAD_DIM):
      q_ref[c0 // HEAD_DIM + hh] = (
          zc_ref[:, hh * HEAD_DIM:(hh + 1) * HEAD_DIM] * (HEAD_DIM ** -0.5)).astype(BF16)

  kvf_ref[...] = jnp.dot(h_ref[...], win_ref[:, OFF_KV:OFF_KV + 2 * KV_WIDTH], preferred_element_type=F32)
  for s in range(n_seg):
    for kh in range(N_KV_HEADS):
      kb_ref[kh, s * b_stride + WINDOW:(s + 1) * b_stride, :] = (
          kvf_ref[s * seg_len:(s + 1) * seg_len, kh * HEAD_DIM:(kh + 1) * HEAD_DIM].astype(BF16))
      vb_ref[kh, s * b_stride + WINDOW:(s + 1) * b_stride, :] = (
          kvf_ref[s * seg_len:(s + 1) * seg_len,
                  KV_WIDTH + kh * HEAD_DIM:KV_WIDTH + (kh + 1) * HEAD_DIM].astype(BF16))

  for s in range(n_seg):
    if seg_len >= WINDOW:
      nk_ref[s] = kvf_ref[(s + 1) * seg_len - WINDOW:(s + 1) * seg_len, 0:KV_WIDTH]
      nv_ref[s] = kvf_ref[(s + 1) * seg_len - WINDOW:(s + 1) * seg_len, KV_WIDTH:2 * KV_WIDTH]
    else:
      nk_ref[s, 0:WINDOW - seg_len, :] = ck_ref[s, seg_len:WINDOW, :]
      nv_ref[s, 0:WINDOW - seg_len, :] = cv_ref[s, seg_len:WINDOW, :]
      nk_ref[s, WINDOW - seg_len:WINDOW, :] = kvf_ref[s * seg_len:(s + 1) * seg_len, 0:KV_WIDTH]
      nv_ref[s, WINDOW - seg_len:WINDOW, :] = kvf_ref[s * seg_len:(s + 1) * seg_len, KV_WIDTH:2 * KV_WIDTH]

  row_id = lax.broadcasted_iota(jnp.int32, (GROUP * CHUNK, 1), 0)
  sink_cols = []
  for kh in range(N_KV_HEADS):
    col = jnp.full((GROUP * CHUNK, 1), sink_ref[kh * GROUP + GROUP - 1], F32)
    for g in range(GROUP - 2, -1, -1):
      col = jnp.where(row_id < (g + 1) * CHUNK, sink_ref[kh * GROUP + g], col)
    sink_cols.append(col)
  n_items = rows // CHUNK
  assert n_seg == 1 or seg_len == CHUNK
  band_step = CHUNK if n_seg == 1 else b_stride

  key_row = lax.broadcasted_iota(jnp.int32, (1, BAND), 1)

  def attend(item):
    q0 = item * CHUNK
    b0 = item * band_step
    masked_keys = 0 if has_state else max(WINDOW - item * CHUNK, 0)
    if masked_keys:
      start_mask = jnp.where(seq_start & (key_row < masked_keys), NEG, 0.0)
    for kh in range(N_KV_HEADS):
      qs = jnp.concatenate(
          [q_ref[kh * GROUP + g, q0:q0 + CHUNK, :] for g in range(GROUP)], axis=0)
      kband = kb_ref[kh, b0:b0 + BAND, :]
      vband = vb_ref[kh, b0:b0 + BAND, :]
      s = lax.dot_general(qs, kband, (((1,), (1,)), ((), ())), preferred_element_type=F32)
      s = s + bias_ref[kh]
      if masked_keys:
        s = s + start_mask
      sink = sink_cols[kh]
      m = jnp.maximum(jnp.max(s, axis=-1, keepdims=True), sink)
      p = jnp.exp(s - m)
      denom = jnp.sum(p, axis=-1, keepdims=True) + jnp.exp(sink - m)
      o = jnp.dot(p.astype(BF16), vband, preferred_element_type=F32) * (1.0 / denom)
      o = jnp.concatenate([o[g * CHUNK:(g + 1) * CHUNK, :] for g in range(GROUP)], axis=1)
      mixed_ref[q0:q0 + CHUNK, CONV_WIDTH + kh * GROUP * HEAD_DIM:
                CONV_WIDTH + (kh + 1) * GROUP * HEAD_DIM] = o.astype(BF16)

  def conv_group(c0):
    cols = slice(c0, c0 + CONV_COLS)
    cb_cols, cc_cols, cu_cols = (slice(off + c0, off + c0 + CONV_COLS) for off in (OFF_CB, OFF_CC, OFF_CU))
    for s in range(n_seg):
      for r in range(0, seg_len, CONV_ROWS):
        zr = s * seg_len + r
        gr = s * g_stride + G_PAD + r
        gs_ref[gr:gr + CONV_ROWS, cols] = (
            zc_ref[zr:zr + CONV_ROWS, cc_cols] * zc_ref[zr:zr + CONV_ROWS, cu_cols])
    for s in range(n_seg):
      for r in range(0, seg_len, CONV_ROWS):
        zr = s * seg_len + r
        gr = s * g_stride + G_PAD + r
        conv = convw_ref[0:1, cols] * gs_ref[gr - 2:gr - 2 + CONV_ROWS, cols]
        conv = conv + convw_ref[1:2, cols] * gs_ref[gr - 1:gr - 1 + CONV_ROWS, cols]
        conv = conv + convw_ref[2:3, cols] * gs_ref[gr:gr + CONV_ROWS, cols]
        mixed_ref[zr:zr + CONV_ROWS, cols] = (zc_ref[zr:zr + CONV_ROWS, cb_cols] * conv).astype(BF16)

  pieces = [off + c0 for c0 in range(0, CONV_WIDTH, CONV_COLS) for off in (OFF_CB, OFF_CC, OFF_CU)]
  assert len(pieces) <= n_items
  for item in range(n_items):
    if item < len(pieces):
      cols = slice(pieces[item], pieces[item] + CONV_COLS)
      zc_ref[:, cols] = jnp.dot(h_ref[...], win_ref[:, cols], preferred_element_type=F32)
    attend(item)
    if item < len(pieces) and item % 3 == 2:
      conv_group(pieces[item] - OFF_CU)

  for s in range(n_seg):
    g_end = s * g_stride + G_PAD + seg_len
    nconv_ref[s] = gs_ref[g_end - (CONV_K - 1):g_end, :]


def _cast_specs(groups, n_steps, step_of):
  in_specs, out_specs, out_shapes = [], [], []
  for group in groups:
    rows, cols = group[0].shape
    assert rows % (n_steps * BF16_ROWS) == 0 and all(w.shape == (rows, cols) for w in group)
    index_map = lambda *ids: (step_of(*ids), 0)
    in_specs += [pl.BlockSpec((rows // n_steps, cols), index_map) for _ in group]
    out_specs.append(pl.BlockSpec((rows // n_steps, cols * len(group)), index_map))
    out_shapes.append(jax.ShapeDtypeStruct((rows, cols * len(group)), BF16))
  return in_specs, out_specs, out_shapes


def _cast_group(src_refs, dst_ref):
  n = len(src_refs)
  for j in range(src_refs[0].shape[1] // FFN_TILE):
    for k, src in enumerate(src_refs):
      dst_ref[:, (j * n + k) * FFN_TILE:(j * n + k + 1) * FFN_TILE] = (
          src[:, j * FFN_TILE:(j + 1) * FFN_TILE].astype(BF16))


def _cast_all(src_refs, dst_refs, group_sizes):
  for size, dst in zip(group_sizes, dst_refs):
    _cast_group(src_refs[:size], dst)
    src_refs = src_refs[size:]


def _mixer_call(x, state, cache_k, cache_v, bkt, rel_table, sinks, g_pre, w_in, conv_w, *, has_state,
                cast_weights=()):
  n_seq, seq_len, _ = x.shape
  if has_state:
    n_seg, seg_len = MIX_ROWS // seq_len, seq_len
    grid = (n_seq // n_seg, 1)
    x = x.reshape(n_seq // n_seg, MIX_ROWS, D_MODEL)
    x_map = lambda b, s: (b, 0, 0)
    seq_map = lambda b, s: (b, 0, 0)
  else:
    n_seg, seg_len = 1, MIX_ROWS
    grid = (n_seq, seq_len // MIX_ROWS)
    x_map = lambda b, s: (b, s, 0)
    seq_map = lambda b, s: (b, 0, 0)
  const2 = lambda b, s: (0, 0)
  once = pl.Buffered(1)

  smem = pl.BlockSpec(memory_space=pltpu.SMEM)
  in_specs = [
      smem, smem,
      pl.BlockSpec((CHUNK, BAND), const2, pipeline_mode=once),
      pl.BlockSpec((None, MIX_ROWS, D_MODEL), x_map, pipeline_mode=once if has_state else None),
      pl.BlockSpec((1, D_MODEL), const2, pipeline_mode=once),
      pl.BlockSpec((D_MODEL, IN_COLS), const2, pipeline_mode=once),
      pl.BlockSpec((CONV_K, CONV_WIDTH), const2, pipeline_mode=once),
  ]
  args = [rel_table, sinks, bkt, x, g_pre, w_in, conv_w]
  if has_state:
    in_specs += [
        pl.BlockSpec((n_seg, CONV_K - 1, CONV_WIDTH), seq_map),
        pl.BlockSpec((n_seg, WINDOW, KV_WIDTH), seq_map, pipeline_mode=once),
        pl.BlockSpec((n_seg, WINDOW, KV_WIDTH), seq_map, pipeline_mode=once),
    ]
    args += [state, cache_k, cache_v]
  cast_in, cast_out, cast_shapes = _cast_specs(cast_weights, grid[0] * grid[1], lambda b, s: b * grid[1] + s)
  in_specs += cast_in
  args += [w for group in cast_weights for w in group]
  out_shape = (
      jax.ShapeDtypeStruct(x.shape, BF16),
      jax.ShapeDtypeStruct((n_seq, CONV_K - 1, CONV_WIDTH), F32),
      jax.ShapeDtypeStruct((n_seq, WINDOW, KV_WIDTH), F32),
      jax.ShapeDtypeStruct((n_seq, WINDOW, KV_WIDTH), F32),
      *cast_shapes,
  )
  out_specs = (
      pl.BlockSpec((None, MIX_ROWS, D_MODEL), x_map),
      pl.BlockSpec((n_seg, CONV_K - 1, CONV_WIDTH), seq_map),
      pl.BlockSpec((n_seg, WINDOW, KV_WIDTH), seq_map),
      pl.BlockSpec((n_seg, WINDOW, KV_WIDTH), seq_map),
      *cast_out,
  )
  band_rows = n_seg * (WINDOW + seg_len)
  scratch = [
      pltpu.VMEM((N_KV_HEADS, GROUP * CHUNK, BAND), F32),
      pltpu.VMEM((MIX_ROWS, D_MODEL), BF16),
      pltpu.VMEM((MIX_ROWS, 3 * CONV_WIDTH), F32),
      pltpu.VMEM((N_HEADS, MIX_ROWS, HEAD_DIM), BF16),
      pltpu.VMEM((MIX_ROWS, 2 * KV_WIDTH), F32),
      pltpu.VMEM((N_KV_HEADS, band_rows, HEAD_DIM), BF16),
      pltpu.VMEM((N_KV_HEADS, band_rows, HEAD_DIM), BF16),
      pltpu.VMEM((n_seg * (G_PAD + seg_len), CONV_WIDTH), F32),
  ]
  kernel = functools.partial(_mixer_kernel, n_seg=n_seg, seg_len=seg_len, has_state=has_state,
                             cast_groups=tuple(len(group) for group in cast_weights))
  return pl.pallas_call(
      kernel,
      grid=grid,
      in_specs=in_specs,
      out_specs=out_specs,
      out_shape=out_shape,
      scratch_shapes=scratch,
      compiler_params=pltpu.CompilerParams(
          dimension_semantics=("arbitrary", "arbitrary"),
          vmem_limit_bytes=V7X_VMEM_LIMIT_BYTES),
      name="mixer_state" if has_state else "mixer_stream",
  )(*args)


def _outproj_kernel(mixed_ref, x_ref, wout_ref, gpm_ref, gpf_ref, *rest, cast_groups):
  rest = list(rest)
  cast_src, rest = rest[:sum(cast_groups)], rest[sum(cast_groups):]
  (x1_ref, hn_ref), rest = rest[:2], rest[2:]
  cast_dst, (acc_ref,) = rest[:len(cast_groups)], rest[len(cast_groups):]
  _cast_all(cast_src, cast_dst, cast_groups)
  gpm = gpm_ref[...]
  gpf = gpf_ref[...]
  part = mixed_ref.shape[0] // OUT_PARTS
  for r0 in range(0, mixed_ref.shape[0], part):
    acc_ref[r0:r0 + part, :] = jnp.dot(
        mixed_ref[r0:r0 + part, :], wout_ref[...], preferred_element_type=F32)
    for r in range(r0, r0 + part, NORM_ROWS):
      x1 = x_ref[r:r + NORM_ROWS, :] + _rms_scale(acc_ref[r:r + NORM_ROWS, :], gpm)
      x1_ref[r:r + NORM_ROWS, :] = x1
      hn_ref[r:r + NORM_ROWS, :] = _rms_scale(x1, gpf).astype(BF16)


def _outproj_call(mixed, x, w_out, g_post_mix, g_pre_ffn, cast_weights=()):
  rows = x.shape[0]
  row_map = lambda i: (i, 0)
  const2 = lambda i: (0, 0)
  once = pl.Buffered(1)
  cast_in, cast_out, cast_shapes = _cast_specs(cast_weights, rows // OUT_ROWS, lambda i: i)
  return pl.pallas_call(
      functools.partial(_outproj_kernel, cast_groups=tuple(len(group) for group in cast_weights)),
      grid=(rows // OUT_ROWS,),
      in_specs=[
          pl.BlockSpec((OUT_ROWS, D_MODEL), row_map),
          pl.BlockSpec((OUT_ROWS, D_MODEL), row_map),
          pl.BlockSpec((D_MODEL, D_MODEL), const2, pipeline_mode=once),
          pl.BlockSpec((1, D_MODEL), const2, pipeline_mode=once),
          pl.BlockSpec((1, D_MODEL), const2, pipeline_mode=once),
          *cast_in,
      ],
      out_specs=(pl.BlockSpec((OUT_ROWS, D_MODEL), row_map), pl.BlockSpec((OUT_ROWS, D_MODEL), row_map),
                 *cast_out),
      out_shape=(jax.ShapeDtypeStruct((rows, D_MODEL), F32), jax.ShapeDtypeStruct((rows, D_MODEL), BF16),
                 *cast_shapes),
      scratch_shapes=[pltpu.VMEM((OUT_ROWS, D_MODEL), F32)],
      compiler_params=pltpu.CompilerParams(
          dimension_semantics=("arbitrary",), vmem_limit_bytes=V7X_VMEM_LIMIT_BYTES),
      name="outproj",
  )(mixed, x, w_out, g_post_mix, g_pre_ffn, *[w for group in cast_weights for w in group])


def _ffn_tile(i, j):
  return jnp.where(i % 2 == 0, j, D_FF // FFN_TILE - 1 - j)


def _ffn_kernel(x1_ref, hn_ref, gqf_ref, wgu_ref, wd_ref, y_ref):
  j = pl.program_id(1)
  rows = y_ref.shape[0]

  last = pl.num_programs(1) - 1

  def swiglu():
    gate_up = jnp.dot(hn_ref[...], wgu_ref[...], preferred_element_type=F32)
    gate, up = gate_up[:, :FFN_TILE], gate_up[:, FFN_TILE:]
    return (gate * (1.0 / (1.0 + jnp.exp(-gate))) * up).astype(BF16)

  @pl.when(j == 0)
  def _():
    y_ref[...] = jnp.dot(swiglu(), wd_ref[...], preferred_element_type=F32)

  @pl.when((j > 0) & (j < last))
  def _():
    y_ref[...] += jnp.dot(swiglu(), wd_ref[...], preferred_element_type=F32)

  @pl.when(j == last)
  def _():
    mid = swiglu()
    gqf = gqf_ref[...]
    part = rows // FFN_LAST_PARTS
    for r0 in range(0, rows, part):
      y_ref[r0:r0 + part, :] += jnp.dot(mid[r0:r0 + part, :], wd_ref[...], preferred_element_type=F32)
      for r in range(r0, r0 + part, NORM_ROWS):
        y_ref[r:r + NORM_ROWS, :] = x1_ref[r:r + NORM_ROWS, :] + _rms_scale(y_ref[r:r + NORM_ROWS, :], gqf)


def _ffn_call(x1, hn, g_post_ffn, w_gate_up, w_down):
  rows = x1.shape[0]
  assert D_FF % FFN_TILE == 0
  row_map = lambda i, j: (i, 0)
  return pl.pallas_call(
      _ffn_kernel,
      grid=(rows // FFN_ROWS, D_FF // FFN_TILE),
      in_specs=[
          pl.BlockSpec((FFN_ROWS, D_MODEL), row_map),
          pl.BlockSpec((FFN_ROWS, D_MODEL), row_map),
          pl.BlockSpec((1, D_MODEL), lambda i, j: (0, 0), pipeline_mode=pl.Buffered(1)),
          pl.BlockSpec((D_MODEL, 2 * FFN_TILE), lambda i, j: (0, _ffn_tile(i, j))),
          pl.BlockSpec((FFN_TILE, D_MODEL), lambda i, j: (_ffn_tile(i, j), 0)),
      ],
      out_specs=pl.BlockSpec((FFN_ROWS, D_MODEL), row_map),
      out_shape=jax.ShapeDtypeStruct((rows, D_MODEL), F32),
      compiler_params=pltpu.CompilerParams(
          dimension_semantics=("arbitrary", "arbitrary"),
          vmem_limit_bytes=V7X_VMEM_LIMIT_BYTES),
      name="ffn",
  )(x1, hn, g_post_ffn, w_gate_up, w_down)


def kernel(x_prompt, x_sample, state_conv, cache_k, cache_v, rel_table, g_pre_mix, w_in, conv_w,
           attn_sinks, w_out, g_post_mix, g_pre_ffn, w_gate, w_up, w_down, g_post_ffn):
  depth, n_dec, cache_len = cache_k.shape[:3]
  batch, seq_len, _ = x_prompt.shape
  assert depth == 1 and cache_len == WINDOW and x_sample.shape[1] == CHUNK
  assert seq_len % MIX_ROWS == 0 and MIX_ROWS % CHUNK == 0

  rel = (jnp.arange(BAND) - WINDOW)[None, :] - jnp.arange(CHUNK)[:, None]
  bkt = _t5_bucket(rel).astype(jnp.int32)

  row = lambda g: g[0].reshape(1, D_MODEL)
  win = w_in[0].astype(BF16)
  mix_args = (bkt, rel_table, attn_sinks[0], row(g_pre_mix), win, conv_w[0])
  mixed_p, conv_p, k_p, v_p, wgu, wo = _mixer_call(
      x_prompt, None, None, None, *mix_args, has_state=False,
      cast_weights=((w_gate[0], w_up[0]), (w_out[0],)))
  mixed_s, conv_s, k_s, v_s = _mixer_call(
      x_sample, state_conv[0], cache_k[0].reshape(n_dec, WINDOW, KV_WIDTH),
      cache_v[0].reshape(n_dec, WINDOW, KV_WIDTH), *mix_args, has_state=True)

  out_args = (wo, row(g_post_mix), row(g_pre_ffn))
  x1_p, hn_p, wd = _outproj_call(
      mixed_p.reshape(-1, D_MODEL), x_prompt.reshape(-1, D_MODEL), *out_args, cast_weights=((w_down[0],),))
  x1_s, hn_s = _outproj_call(mixed_s.reshape(-1, D_MODEL), x_sample.reshape(-1, D_MODEL), *out_args)
  ffn_args = (row(g_post_ffn), wgu, wd)
  y_p = _ffn_call(x1_p, hn_p, *ffn_args)
  y_s = _ffn_call(x1_s, hn_s, *ffn_args)

  heads = lambda a: a.reshape(1, a.shape[0], WINDOW, N_KV_HEADS, HEAD_DIM)
  return (y_p.reshape(x_prompt.shape), y_s.reshape(x_sample.shape),
          conv_p[None], heads(k_p), heads(v_p), conv_s[None], heads(k_s), heads(v_s))
```

```python
import functools
import math

import jax
import jax.numpy as jnp
import numpy as np
from jax import lax
from jax.experimental import pallas as pl
from jax.experimental.pallas import tpu as pltpu

D_MODEL = 2048
CHUNK = 64
HEAD_DIM = 64
CONV_WIDTH = D_MODEL // 2
CONV_K = 3
N_HEADS = (D_MODEL - CONV_WIDTH) // HEAD_DIM
N_KV_HEADS = 4
GROUP = N_HEADS // N_KV_HEADS
ATTN_WIDTH = N_HEADS * HEAD_DIM
KV_WIDTH = N_KV_HEADS * HEAD_DIM
IN_COLS = 3 * CONV_WIDTH + ATTN_WIDTH + 2 * KV_WIDTH
WINDOW = 128
BAND = WINDOW + CHUNK
NUM_BUCKETS = 32
MAX_DISTANCE = 128
D_FF = -(-8 * D_MODEL // (3 * 256)) * 256
EPS = 1e-6
NEG = -1e30

OFF_CB, OFF_CC, OFF_CU = 0, CONV_WIDTH, 2 * CONV_WIDTH
OFF_Q = 3 * CONV_WIDTH
OFF_KV = OFF_Q + ATTN_WIDTH

SUBLANES = 8
BF16_ROWS = 16
V7X_VMEM_LIMIT_BYTES = 61 << 20
MIX_ROWS = 512
OUT_ROWS = 512
OUT_PARTS = 4
FFN_ROWS = 1024
FFN_TILE = 512
FFN_LAST_PARTS = 4
CONV_COLS = 512
NORM_ROWS = 16
NORM_UNROLL = 8
CONV_ROWS = 32
G_PAD = SUBLANES

F32 = jnp.float32
BF16 = jnp.bfloat16


def _t5_bucket(rel):
  half = NUM_BUCKETS // 2
  ret = np.where(rel > 0, half, 0)
  n = np.abs(rel)
  max_exact = half // 2
  nf = np.maximum(n, 1).astype(np.float32)
  large = max_exact + (np.log(nf / np.float32(max_exact)) / np.float32(math.log(MAX_DISTANCE / max_exact))
                       * np.float32(half - max_exact)).astype(np.int32)
  large = np.minimum(large, half - 1)
  return ret + np.where(n < max_exact, n, large)


def _rms_scale(y, gain):
  return y * lax.rsqrt(jnp.mean(y * y, axis=-1, keepdims=True) + EPS) * gain


def _mixer_kernel(tbl_ref, sink_ref, bkt_ref, x_ref, gpre_ref, win_ref, convw_ref, *rest,
                  n_seg, seg_len, has_state, cast_groups):
  rest = list(rest)
  if has_state:
    state_ref, ck_ref, cv_ref = rest[:3]
    rest = rest[3:]
  cast_src, rest = rest[:sum(cast_groups)], rest[sum(cast_groups):]
  (mixed_ref, nconv_ref, nk_ref, nv_ref), rest = rest[:4], rest[4:]
  cast_dst, rest = rest[:len(cast_groups)], rest[len(cast_groups):]
  bias_ref, h_ref, zc_ref, q_ref, kvf_ref, kb_ref, vb_ref, gs_ref = rest
  rows = n_seg * seg_len
  g_stride = G_PAD + seg_len
  b_stride = WINDOW + seg_len
  first_call = (pl.program_id(0) == 0) & (pl.program_id(1) == 0)
  seq_start = pl.program_id(1) == 0

  @pl.when(first_call)
  def _():
    bkt = bkt_ref[...]
    for h in range(N_HEADS):
      acc = jnp.zeros((CHUNK, BAND), F32)
      for j in range(NUM_BUCKETS):
        acc = jnp.where(bkt == j, tbl_ref[j, h], acc)
      g = h % GROUP
      bias_ref[h // GROUP, g * CHUNK:(g + 1) * CHUNK, :] = acc

  if has_state:
    for s in range(n_seg):
      gs_ref[s * g_stride + G_PAD - (CONV_K - 1):s * g_stride + G_PAD, :] = state_ref[s]
      for kh in range(N_KV_HEADS):
        kb_ref[kh, s * b_stride:s * b_stride + WINDOW, :] = (
            ck_ref[s, :, kh * HEAD_DIM:(kh + 1) * HEAD_DIM].astype(BF16))
        vb_ref[kh, s * b_stride:s * b_stride + WINDOW, :] = (
            cv_ref[s, :, kh * HEAD_DIM:(kh + 1) * HEAD_DIM].astype(BF16))
  else:
    @pl.when(seq_start)
    def _():
      gs_ref[0:G_PAD, :] = jnp.zeros((G_PAD, CONV_WIDTH), F32)
      kb_ref[:, 0:WINDOW, :] = jnp.zeros((N_KV_HEADS, WINDOW, HEAD_DIM), BF16)
      vb_ref[:, 0:WINDOW, :] = jnp.zeros((N_KV_HEADS, WINDOW, HEAD_DIM), BF16)

    @pl.when(jnp.logical_not(seq_start))
    def _():
      gs_ref[0:G_PAD, :] = gs_ref[seg_len:seg_len + G_PAD, :]
      kb_ref[:, 0:WINDOW, :] = kb_ref[:, seg_len:seg_len + WINDOW, :]
      vb_ref[:, 0:WINDOW, :] = vb_ref[:, seg_len:seg_len + WINDOW, :]

  gpre = gpre_ref[...]

  def norm_body(i, carry):
    r = pl.multiple_of(i * NORM_ROWS, NORM_ROWS)
    h_ref[pl.ds(r, NORM_ROWS), :] = _rms_scale(x_ref[pl.ds(r, NORM_ROWS), :], gpre).astype(BF16)
    return carry

  lax.fori_loop(0, rows // NORM_ROWS, norm_body, 0, unroll=NORM_UNROLL)

  _cast_all(cast_src, cast_dst, cast_groups)

  for c0 in range(0, ATTN_WIDTH, CONV_COLS):
    zc_ref[:, 0:CONV_COLS] = jnp.dot(
        h_ref[...], win_ref[:, OFF_Q + c0:OFF_Q + c0 + CONV_COLS], preferred_element_type=F32)
    for hh in range(CONV_COLS // HEAD_DIM):
      q_ref[c0 // HEAD_DIM + hh] = (
          zc_ref[:, hh * HEAD_DIM:(hh + 1) * HEAD_DIM] * (HEAD_DIM ** -0.5)).astype(BF16)

  kvf_ref[...] = jnp.dot(h_ref[...], win_ref[:, OFF_KV:OFF_KV + 2 * KV_WIDTH], preferred_element_type=F32)
  for s in range(n_seg):
    for kh in range(N_KV_HEADS):
      kb_ref[kh, s * b_stride + WINDOW:(s + 1) * b_stride, :] = (
          kvf_ref[s * seg_len:(s + 1) * seg_len, kh * HEAD_DIM:(kh + 1) * HEAD_DIM].astype(BF16))
      vb_ref[kh, s * b_stride + WINDOW:(s + 1) * b_stride, :] = (
          kvf_ref[s * seg_len:(s + 1) * seg_len,
                  KV_WIDTH + kh * HEAD_DIM:KV_WIDTH + (kh + 1) * HEAD_DIM].astype(BF16))

  for s in range(n_seg):
    if seg_len >= WINDOW:
      nk_ref[s] = kvf_ref[(s + 1) * seg_len - WINDOW:(s + 1) * seg_len, 0:KV_WIDTH]
      nv_ref[s] = kvf_ref[(s + 1) * seg_len - WINDOW:(s + 1) * seg_len, KV_WIDTH:2 * KV_WIDTH]
    else:
      nk_ref[s, 0:WINDOW - seg_len, :] = ck_ref[s, seg_len:WINDOW, :]
      nv_ref[s, 0:WINDOW - seg_len, :] = cv_ref[s, seg_len:WINDOW, :]
      nk_ref[s, WINDOW - seg_len:WINDOW, :] = kvf_ref[s * seg_len:(s + 1) * seg_len, 0:KV_WIDTH]
      nv_ref[s, WINDOW - seg_len:WINDOW, :] = kvf_ref[s * seg_len:(s + 1) * seg_len, KV_WIDTH:2 * KV_WIDTH]

  row_id = lax.broadcasted_iota(jnp.int32, (GROUP * CHUNK, 1), 0)
  sink_cols = []
  for kh in range(N_KV_HEADS):
    col = jnp.full((GROUP * CHUNK, 1), sink_ref[kh * GROUP + GROUP - 1], F32)
    for g in range(GROUP - 2, -1, -1):
      col = jnp.where(row_id < (g + 1) * CHUNK, sink_ref[kh * GROUP + g], col)
    sink_cols.append(col)
  n_items = rows // CHUNK
  assert n_seg == 1 or seg_len == CHUNK
  band_step = CHUNK if n_seg == 1 else b_stride

  key_row = lax.broadcasted_iota(jnp.int32, (1, BAND), 1)

  def attend(item):
    q0 = item * CHUNK
    b0 = item * band_step
    masked_keys = 0 if has_state else max(WINDOW - item * CHUNK, 0)
    if masked_keys:
      start_mask = jnp.where(seq_start & (key_row < masked_keys), NEG, 0.0)
    for kh in range(N_KV_HEADS):
      qs = jnp.concatenate(
          [q_ref[kh * GROUP + g, q0:q0 + CHUNK, :] for g in range(GROUP)], axis=0)
      kband = kb_ref[kh, b0:b0 + BAND, :]
      vband = vb_ref[kh, b0:b0 + BAND, :]
      s = lax.dot_general(qs, kband, (((1,), (1,)), ((), ())), preferred_element_type=F32)
      s = s + bias_ref[kh]
      if masked_keys:
        s = s + start_mask
      sink = sink_cols[kh]
      m = jnp.maximum(jnp.max(s, axis=-1, keepdims=True), sink)
      p = jnp.exp(s - m)
      denom = jnp.sum(p, axis=-1, keepdims=True) + jnp.exp(sink - m)
      o = jnp.dot(p.astype(BF16), vband, preferred_element_type=F32) * (1.0 / denom)
      o = jnp.concatenate([o[g * CHUNK:(g + 1) * CHUNK, :] for g in range(GROUP)], axis=1)
      mixed_ref[q0:q0 + CHUNK, CONV_WIDTH + kh * GROUP * HEAD_DIM:
                CONV_WIDTH + (kh + 1) * GROUP * HEAD_DIM] = o.astype(BF16)

  def conv_group(c0):
    cols = slice(c0, c0 + CONV_COLS)
    cb_cols, cc_cols, cu_cols = (slice(off + c0, off + c0 + CONV_COLS) for off in (OFF_CB, OFF_CC, OFF_CU))
    for s in range(n_seg):
      for r in range(0, seg_len, CONV_ROWS):
        zr = s * seg_len + r
        gr = s * g_stride + G_PAD + r
        gs_ref[gr:gr + CONV_ROWS, cols] = (
            zc_ref[zr:zr + CONV_ROWS, cc_cols] * zc_ref[zr:zr + CONV_ROWS, cu_cols])
    for s in range(n_seg):
      for r in range(0, seg_len, CONV_ROWS):
        zr = s * seg_len + r
        gr = s * g_stride + G_PAD + r
        conv = convw_ref[0:1, cols] * gs_ref[gr - 2:gr - 2 + CONV_ROWS, cols]
        conv = conv + convw_ref[1:2, cols] * gs_ref[gr - 1:gr - 1 + CONV_ROWS, cols]
        conv = conv + convw_ref[2:3, cols] * gs_ref[gr:gr + CONV_ROWS, cols]
        mixed_ref[zr:zr + CONV_ROWS, cols] = (zc_ref[zr:zr + CONV_ROWS, cb_cols] * conv).astype(BF16)

  pieces = [off + c0 for c0 in range(0, CONV_WIDTH, CONV_COLS) for off in (OFF_CB, OFF_CC, OFF_CU)]
  assert len(pieces) <= n_items
  for item in range(n_items):
    if item < len(pieces):
      cols = slice(pieces[item], pieces[item] + CONV_COLS)
      zc_ref[:, cols] = jnp.dot(h_ref[...], win_ref[:, cols], preferred_element_type=F32)
    attend(item)
    if item < len(pieces) and item % 3 == 2:
      conv_group(pieces[item] - OFF_CU)

  for s in range(n_seg):
    g_end = s * g_stride + G_PAD + seg_len
    nconv_ref[s] = gs_ref[g_end - (CONV_K - 1):g_end, :]


def _cast_specs(groups, n_steps, step_of):
  in_specs, out_specs, out_shapes = [], [], []
  for group in groups:
    rows, cols = group[0].shape
    assert rows % (n_steps * BF16_ROWS) == 0 and all(w.shape == (rows, cols) for w in group)
    index_map = lambda *ids: (step_of(*ids), 0)
    in_specs += [pl.BlockSpec((rows // n_steps, cols), index_map) for _ in group]
    out_specs.append(pl.BlockSpec((rows // n_steps, cols * len(group)), index_map))
    out_shapes.append(jax.ShapeDtypeStruct((rows, cols * len(group)), BF16))
  return in_specs, out_specs, out_shapes


def _cast_group(src_refs, dst_ref):
  n = len(src_refs)
  for j in range(src_refs[0].shape[1] // FFN_TILE):
    for k, src in enumerate(src_refs):
      dst_ref[:, (j * n + k) * FFN_TILE:(j * n + k + 1) * FFN_TILE] = (
          src[:, j * FFN_TILE:(j + 1) * FFN_TILE].astype(BF16))


def _cast_all(src_refs, dst_refs, group_sizes):
  for size, dst in zip(group_sizes, dst_refs):
    _cast_group(src_refs[:size], dst)
    src_refs = src_refs[size:]


def _mixer_call(x, state, cache_k, cache_v, bkt, rel_table, sinks, g_pre, w_in, conv_w, *, has_state,
                cast_weights=()):
  n_seq, seq_len, _ = x.shape
  if has_state:
    n_seg, seg_len = MIX_ROWS // seq_len, seq_len
    grid = (n_seq // n_seg, 1)
    x = x.reshape(n_seq // n_seg, MIX_ROWS, D_MODEL)
    x_map = lambda b, s: (b, 0, 0)
    seq_map = lambda b, s: (b, 0, 0)
  else:
    n_seg, seg_len = 1, MIX_ROWS
    grid = (n_seq, seq_len // MIX_ROWS)
    x_map = lambda b, s: (b, s, 0)
    seq_map = lambda b, s: (b, 0, 0)
  const2 = lambda b, s: (0, 0)
  once = pl.Buffered(1)

  smem = pl.BlockSpec(memory_space=pltpu.SMEM)
  in_specs = [
      smem, smem,
      pl.BlockSpec((CHUNK, BAND), const2, pipeline_mode=once),
      pl.BlockSpec((None, MIX_ROWS, D_MODEL), x_map),
      pl.BlockSpec((1, D_MODEL), const2, pipeline_mode=once),
      pl.BlockSpec((D_MODEL, IN_COLS), const2, pipeline_mode=once),
      pl.BlockSpec((CONV_K, CONV_WIDTH), const2, pipeline_mode=once),
  ]
  args = [rel_table, sinks, bkt, x, g_pre, w_in, conv_w]
  if has_state:
    in_specs += [
        pl.BlockSpec((n_seg, CONV_K - 1, CONV_WIDTH), seq_map),
        pl.BlockSpec((n_seg, WINDOW, KV_WIDTH), seq_map),
        pl.BlockSpec((n_seg, WINDOW, KV_WIDTH), seq_map),
    ]
    args += [state, cache_k, cache_v]
  cast_in, cast_out, cast_shapes = _cast_specs(cast_weights, grid[0] * grid[1], lambda b, s: b * grid[1] + s)
  in_specs += cast_in
  args += [w for group in cast_weights for w in group]
  out_shape = (
      jax.ShapeDtypeStruct(x.shape, BF16),
      jax.ShapeDtypeStruct((n_seq, CONV_K - 1, CONV_WIDTH), F32),
      jax.ShapeDtypeStruct((n_seq, WINDOW, KV_WIDTH), F32),
      jax.ShapeDtypeStruct((n_seq, WINDOW, KV_WIDTH), F32),
      *cast_shapes,
  )
  out_specs = (
      pl.BlockSpec((None, MIX_ROWS, D_MODEL), x_map),
      pl.BlockSpec((n_seg, CONV_K - 1, CONV_WIDTH), seq_map),
      pl.BlockSpec((n_seg, WINDOW, KV_WIDTH), seq_map),
      pl.BlockSpec((n_seg, WINDOW, KV_WIDTH), seq_map),
      *cast_out,
  )
  band_rows = n_seg * (WINDOW + seg_len)
  scratch = [
      pltpu.VMEM((N_KV_HEADS, GROUP * CHUNK, BAND), F32),
      pltpu.VMEM((MIX_ROWS, D_MODEL), BF16),
      pltpu.VMEM((MIX_ROWS, 3 * CONV_WIDTH), F32),
      pltpu.VMEM((N_HEADS, MIX_ROWS, HEAD_DIM), BF16),
      pltpu.VMEM((MIX_ROWS, 2 * KV_WIDTH), F32),
      pltpu.VMEM((N_KV_HEADS, band_rows, HEAD_DIM), BF16),
      pltpu.VMEM((N_KV_HEADS, band_rows, HEAD_DIM), BF16),
      pltpu.VMEM((n_seg * (G_PAD + seg_len), CONV_WIDTH), F32),
  ]
  kernel = functools.partial(_mixer_kernel, n_seg=n_seg, seg_len=seg_len, has_state=has_state,
                             cast_groups=tuple(len(group) for group in cast_weights))
  return pl.pallas_call(
      kernel,
      grid=grid,
      in_specs=in_specs,
      out_specs=out_specs,
      out_shape=out_shape,
      scratch_shapes=scratch,
      compiler_params=pltpu.CompilerParams(
          dimension_semantics=("arbitrary", "arbitrary"),
          vmem_limit_bytes=V7X_VMEM_LIMIT_BYTES),
      name="mixer_state" if has_state else "mixer_stream",
  )(*args)


def _outproj_kernel(mixed_ref, x_ref, wout_ref, gpm_ref, gpf_ref, *rest, cast_groups):
  rest = list(rest)
  cast_src, rest = rest[:sum(cast_groups)], rest[sum(cast_groups):]
  (x1_ref, hn_ref), rest = rest[:2], rest[2:]
  cast_dst, (acc_ref,) = rest[:len(cast_groups)], rest[len(cast_groups):]
  _cast_all(cast_src, cast_dst, cast_groups)
  gpm = gpm_ref[...]
  gpf = gpf_ref[...]
  part = mixed_ref.shape[0] // OUT_PARTS
  for r0 in range(0, mixed_ref.shape[0], part):
    acc_ref[r0:r0 + part, :] = jnp.dot(
        mixed_ref[r0:r0 + part, :], wout_ref[...], preferred_element_type=F32)
    for r in range(r0, r0 + part, NORM_ROWS):
      x1 = x_ref[r:r + NORM_ROWS, :] + _rms_scale(acc_ref[r:r + NORM_ROWS, :], gpm)
      x1_ref[r:r + NORM_ROWS, :] = x1
      hn_ref[r:r + NORM_ROWS, :] = _rms_scale(x1, gpf).astype(BF16)


def _outproj_call(mixed, x, w_out, g_post_mix, g_pre_ffn, cast_weights=()):
  rows = x.shape[0]
  row_map = lambda i: (i, 0)
  const2 = lambda i: (0, 0)
  once = pl.Buffered(1)
  cast_in, cast_out, cast_shapes = _cast_specs(cast_weights, rows // OUT_ROWS, lambda i: i)
  return pl.pallas_call(
      functools.partial(_outproj_kernel, cast_groups=tuple(len(group) for group in cast_weights)),
      grid=(rows // OUT_ROWS,),
      in_specs=[
          pl.BlockSpec((OUT_ROWS, D_MODEL), row_map),
          pl.BlockSpec((OUT_ROWS, D_MODEL), row_map),
          pl.BlockSpec((D_MODEL, D_MODEL), const2, pipeline_mode=once),
          pl.BlockSpec((1, D_MODEL), const2, pipeline_mode=once),
          pl.BlockSpec((1, D_MODEL), const2, pipeline_mode=once),
          *cast_in,
      ],
      out_specs=(pl.BlockSpec((OUT_ROWS, D_MODEL), row_map), pl.BlockSpec((OUT_ROWS, D_MODEL), row_map),
                 *cast_out),
      out_shape=(jax.ShapeDtypeStruct((rows, D_MODEL), F32), jax.ShapeDtypeStruct((rows, D_MODEL), BF16),
                 *cast_shapes),
      scratch_shapes=[pltpu.VMEM((OUT_ROWS, D_MODEL), F32)],
      compiler_params=pltpu.CompilerParams(
          dimension_semantics=("arbitrary",), vmem_limit_bytes=V7X_VMEM_LIMIT_BYTES),
      name="outproj",
  )(mixed, x, w_out, g_post_mix, g_pre_ffn, *[w for group in cast_weights for w in group])


def _ffn_tile(i, j):
  return jnp.where(i % 2 == 0, j, D_FF // FFN_TILE - 1 - j)


def _ffn_kernel(x1_ref, hn_ref, gqf_ref, wgu_ref, wd_ref, y_ref):
  j = pl.program_id(1)
  rows = y_ref.shape[0]

  last = pl.num_programs(1) - 1

  def swiglu():
    gate_up = jnp.dot(hn_ref[...], wgu_ref[...], preferred_element_type=F32)
    gate, up = gate_up[:, :FFN_TILE], gate_up[:, FFN_TILE:]
    return (gate * (1.0 / (1.0 + jnp.exp(-gate))) * up).astype(BF16)

  @pl.when(j == 0)
  def _():
    y_ref[...] = jnp.dot(swiglu(), wd_ref[...], preferred_element_type=F32)

  @pl.when((j > 0) & (j < last))
  def _():
    y_ref[...] += jnp.dot(swiglu(), wd_ref[...], preferred_element_type=F32)

  @pl.when(j == last)
  def _():
    mid = swiglu()
    gqf = gqf_ref[...]
    part = rows // FFN_LAST_PARTS
    for r0 in range(0, rows, part):
      y_ref[r0:r0 + part, :] += jnp.dot(mid[r0:r0 + part, :], wd_ref[...], preferred_element_type=F32)
      for r in range(r0, r0 + part, NORM_ROWS):
        y_ref[r:r + NORM_ROWS, :] = x1_ref[r:r + NORM_ROWS, :] + _rms_scale(y_ref[r:r + NORM_ROWS, :], gqf)


def _ffn_call(x1, hn, g_post_ffn, w_gate_up, w_down):
  rows = x1.shape[0]
  assert D_FF % FFN_TILE == 0
  row_map = lambda i, j: (i, 0)
  return pl.pallas_call(
      _ffn_kernel,
      grid=(rows // FFN_ROWS, D_FF // FFN_TILE),
      in_specs=[
          pl.BlockSpec((FFN_ROWS, D_MODEL), row_map),
          pl.BlockSpec((FFN_ROWS, D_MODEL), row_map),
          pl.BlockSpec((1, D_MODEL), lambda i, j: (0, 0), pipeline_mode=pl.Buffered(1)),
          pl.BlockSpec((D_MODEL, 2 * FFN_TILE), lambda i, j: (0, _ffn_tile(i, j))),
          pl.BlockSpec((FFN_TILE, D_MODEL), lambda i, j: (_ffn_tile(i, j), 0)),
      ],
      out_specs=pl.BlockSpec((FFN_ROWS, D_MODEL), row_map),
      out_shape=jax.ShapeDtypeStruct((rows, D_MODEL), F32),
      compiler_params=pltpu.CompilerParams(
          dimension_semantics=("arbitrary", "arbitrary"),
          vmem_limit_bytes=V7X_VMEM_LIMIT_BYTES),
      name="ffn",
  )(x1, hn, g_post_ffn, w_gate_up, w_down)


def kernel(x_prompt, x_sample, state_conv, cache_k, cache_v, rel_table, g_pre_mix, w_in, conv_w,
           attn_sinks, w_out, g_post_mix, g_pre_ffn, w_gate, w_up, w_down, g_post_ffn):
  depth, n_dec, cache_len = cache_k.shape[:3]
  batch, seq_len, _ = x_prompt.shape
  assert depth == 1 and cache_len == WINDOW and x_sample.shape[1] == CHUNK
  assert seq_len % MIX_ROWS == 0 and MIX_ROWS % CHUNK == 0

  rel = (np.arange(BAND) - WINDOW)[None, :] - np.arange(CHUNK)[:, None]
  bkt = jnp.asarray(_t5_bucket(rel), jnp.int32)

  row = lambda g: g[0].reshape(1, D_MODEL)
  win = w_in[0].astype(BF16)
  mix_args = (bkt, rel_table, attn_sinks[0], row(g_pre_mix), win, conv_w[0])
  mixed_p, conv_p, k_p, v_p, wgu, wo = _mixer_call(
      x_prompt, None, None, None, *mix_args, has_state=False,
      cast_weights=((w_gate[0], w_up[0]), (w_out[0],)))
  mixed_s, conv_s, k_s, v_s = _mixer_call(
      x_sample, state_conv[0], cache_k[0].reshape(n_dec, WINDOW, KV_WIDTH),
      cache_v[0].reshape(n_dec, WINDOW, KV_WIDTH), *mix_args, has_state=True)

  out_args = (wo, row(g_post_mix), row(g_pre_ffn))
  x1_p, hn_p, wd = _outproj_call(
      mixed_p.reshape(-1, D_MODEL), x_prompt.reshape(-1, D_MODEL), *out_args, cast_weights=((w_down[0],),))
  x1_s, hn_s = _outproj_call(mixed_s.reshape(-1, D_MODEL), x_sample.reshape(-1, D_MODEL), *out_args)
  ffn_args = (row(g_post_ffn), wgu, wd)
  y_p = _ffn_call(x1_p, hn_p, *ffn_args)
  y_s = _ffn_call(x1_s, hn_s, *ffn_args)

  heads = lambda a: a.reshape(1, a.shape[0], WINDOW, N_KV_HEADS, HEAD_DIM)
  return (y_p.reshape(x_prompt.shape), y_s.reshape(x_sample.shape),
          conv_p[None], heads(k_p), heads(v_p), conv_s[None], heads(k_s), heads(v_s))
```

```python
import functools
import math

import jax
import jax.numpy as jnp
import numpy as np
from jax import lax
from jax.experimental import pallas as pl
from jax.experimental.pallas import tpu as pltpu

D_MODEL = 2048
CHUNK = 64
HEAD_DIM = 64
CONV_WIDTH = D_MODEL // 2
CONV_K = 3
N_HEADS = (D_MODEL - CONV_WIDTH) // HEAD_DIM
N_KV_HEADS = 4
GROUP = N_HEADS // N_KV_HEADS
ATTN_WIDTH = N_HEADS * HEAD_DIM
KV_WIDTH = N_KV_HEADS * HEAD_DIM
IN_COLS = 3 * CONV_WIDTH + ATTN_WIDTH + 2 * KV_WIDTH
WINDOW = 128
BAND = WINDOW + CHUNK
NUM_BUCKETS = 32
MAX_DISTANCE = 128
D_FF = -(-8 * D_MODEL // (3 * 256)) * 256
EPS = 1e-6
NEG = -1e30

OFF_CB, OFF_CC, OFF_CU = 0, CONV_WIDTH, 2 * CONV_WIDTH
OFF_Q = 3 * CONV_WIDTH
OFF_KV = OFF_Q + ATTN_WIDTH

SUBLANES = 8
BF16_ROWS = 16
V7X_VMEM_LIMIT_BYTES = 61 << 20
MIX_ROWS = 512
OUT_ROWS = 512
OUT_PARTS = 4
FFN_ROWS = 1024
FFN_TILE = 512
FFN_LAST_PARTS = 4
CONV_COLS = 512
NORM_ROWS = 16
NORM_UNROLL = 16
CONV_ROWS = 64
G_PAD = SUBLANES

F32 = jnp.float32
BF16 = jnp.bfloat16


def _t5_bucket(rel):
  half = NUM_BUCKETS // 2
  ret = np.where(rel > 0, half, 0)
  n = np.abs(rel)
  max_exact = half // 2
  nf = np.maximum(n, 1).astype(np.float32)
  large = max_exact + (np.log(nf / np.float32(max_exact)) / np.float32(math.log(MAX_DISTANCE / max_exact))
                       * np.float32(half - max_exact)).astype(np.int32)
  large = np.minimum(large, half - 1)
  return ret + np.where(n < max_exact, n, large)


def _rms_scale(y, gain):
  return y * lax.rsqrt(jnp.mean(y * y, axis=-1, keepdims=True) + EPS) * gain


def _mixer_kernel(tbl_ref, sink_ref, bkt_ref, x_ref, gpre_ref, win_ref, convw_ref, *rest,
                  n_seg, seg_len, has_state, cast_groups):
  rest = list(rest)
  if has_state:
    state_ref, ck_ref, cv_ref = rest[:3]
    rest = rest[3:]
  cast_src, rest = rest[:sum(cast_groups)], rest[sum(cast_groups):]
  (mixed_ref, nconv_ref, nk_ref, nv_ref), rest = rest[:4], rest[4:]
  cast_dst, rest = rest[:len(cast_groups)], rest[len(cast_groups):]
  bias_ref, h_ref, zc_ref, q_ref, kvf_ref, kb_ref, vb_ref, gs_ref = rest
  rows = n_seg * seg_len
  g_stride = G_PAD + seg_len
  b_stride = WINDOW + seg_len
  first_call = (pl.program_id(0) == 0) & (pl.program_id(1) == 0)
  seq_start = pl.program_id(1) == 0

  @pl.when(first_call)
  def _():
    bkt = bkt_ref[...]
    for h in range(N_HEADS):
      acc = jnp.zeros((CHUNK, BAND), F32)
      for j in range(NUM_BUCKETS):
        acc = jnp.where(bkt == j, tbl_ref[j, h], acc)
      g = h % GROUP
      bias_ref[h // GROUP, g * CHUNK:(g + 1) * CHUNK, :] = acc

  if has_state:
    for s in range(n_seg):
      gs_ref[s * g_stride + G_PAD - (CONV_K - 1):s * g_stride + G_PAD, :] = state_ref[s]
      for kh in range(N_KV_HEADS):
        kb_ref[kh, s * b_stride:s * b_stride + WINDOW, :] = (
            ck_ref[s, :, kh * HEAD_DIM:(kh + 1) * HEAD_DIM].astype(BF16))
        vb_ref[kh, s * b_stride:s * b_stride + WINDOW, :] = (
            cv_ref[s, :, kh * HEAD_DIM:(kh + 1) * HEAD_DIM].astype(BF16))
  else:
    @pl.when(seq_start)
    def _():
      gs_ref[0:G_PAD, :] = jnp.zeros((G_PAD, CONV_WIDTH), F32)
      kb_ref[:, 0:WINDOW, :] = jnp.zeros((N_KV_HEADS, WINDOW, HEAD_DIM), BF16)
      vb_ref[:, 0:WINDOW, :] = jnp.zeros((N_KV_HEADS, WINDOW, HEAD_DIM), BF16)

    @pl.when(jnp.logical_not(seq_start))
    def _():
      gs_ref[0:G_PAD, :] = gs_ref[seg_len:seg_len + G_PAD, :]
      kb_ref[:, 0:WINDOW, :] = kb_ref[:, seg_len:seg_len + WINDOW, :]
      vb_ref[:, 0:WINDOW, :] = vb_ref[:, seg_len:seg_len + WINDOW, :]

  gpre = gpre_ref[...]

  def norm_body(i, carry):
    r = pl.multiple_of(i * NORM_ROWS, NORM_ROWS)
    h_ref[pl.ds(r, NORM_ROWS), :] = _rms_scale(x_ref[pl.ds(r, NORM_ROWS), :], gpre).astype(BF16)
    return carry

  lax.fori_loop(0, rows // NORM_ROWS, norm_body, 0, unroll=NORM_UNROLL)

  _cast_all(cast_src, cast_dst, cast_groups)

  for c0 in range(0, ATTN_WIDTH, CONV_COLS):
    zc_ref[:, 0:CONV_COLS] = jnp.dot(
        h_ref[...], win_ref[:, OFF_Q + c0:OFF_Q + c0 + CONV_COLS], preferred_element_type=F32)
    for hh in range(CONV_COLS // HEAD_DIM):
      q_ref[c0 // HEAD_DIM + hh] = (
          zc_ref[:, hh * HEAD_DIM:(hh + 1) * HEAD_DIM] * (HEAD_DIM ** -0.5)).astype(BF16)

  kvf_ref[...] = jnp.dot(h_ref[...], win_ref[:, OFF_KV:OFF_KV + 2 * KV_WIDTH], preferred_element_type=F32)
  for s in range(n_seg):
    for kh in range(N_KV_HEADS):
      kb_ref[kh, s * b_stride + WINDOW:(s + 1) * b_stride, :] = (
          kvf_ref[s * seg_len:(s + 1) * seg_len, kh * HEAD_DIM:(kh + 1) * HEAD_DIM].astype(BF16))
      vb_ref[kh, s * b_stride + WINDOW:(s + 1) * b_stride, :] = (
          kvf_ref[s * seg_len:(s + 1) * seg_len,
                  KV_WIDTH + kh * HEAD_DIM:KV_WIDTH + (kh + 1) * HEAD_DIM].astype(BF16))

  for s in range(n_seg):
    if seg_len >= WINDOW:
      nk_ref[s] = kvf_ref[(s + 1) * seg_len - WINDOW:(s + 1) * seg_len, 0:KV_WIDTH]
      nv_ref[s] = kvf_ref[(s + 1) * seg_len - WINDOW:(s + 1) * seg_len, KV_WIDTH:2 * KV_WIDTH]
    else:
      nk_ref[s, 0:WINDOW - seg_len, :] = ck_ref[s, seg_len:WINDOW, :]
      nv_ref[s, 0:WINDOW - seg_len, :] = cv_ref[s, seg_len:WINDOW, :]
      nk_ref[s, WINDOW - seg_len:WINDOW, :] = kvf_ref[s * seg_len:(s + 1) * seg_len, 0:KV_WIDTH]
      nv_ref[s, WINDOW - seg_len:WINDOW, :] = kvf_ref[s * seg_len:(s + 1) * seg_len, KV_WIDTH:2 * KV_WIDTH]

  row_id = lax.broadcasted_iota(jnp.int32, (GROUP * CHUNK, 1), 0)
  sink_cols = []
  for kh in range(N_KV_HEADS):
    col = jnp.full((GROUP * CHUNK, 1), sink_ref[kh * GROUP + GROUP - 1], F32)
    for g in range(GROUP - 2, -1, -1):
      col = jnp.where(row_id < (g + 1) * CHUNK, sink_ref[kh * GROUP + g], col)
    sink_cols.append(col)
  n_items = rows // CHUNK
  assert n_seg == 1 or seg_len == CHUNK
  band_step = CHUNK if n_seg == 1 else b_stride

  key_row = lax.broadcasted_iota(jnp.int32, (1, BAND), 1)

  def attend(item):
    q0 = item * CHUNK
    b0 = item * band_step
    masked_keys = 0 if has_state else max(WINDOW - item * CHUNK, 0)
    if masked_keys:
      start_mask = jnp.where(seq_start & (key_row < masked_keys), NEG, 0.0)
    for kh in range(N_KV_HEADS):
      qs = jnp.concatenate(
          [q_ref[kh * GROUP + g, q0:q0 + CHUNK, :] for g in range(GROUP)], axis=0)
      kband = kb_ref[kh, b0:b0 + BAND, :]
      vband = vb_ref[kh, b0:b0 + BAND, :]
      s = lax.dot_general(qs, kband, (((1,), (1,)), ((), ())), preferred_element_type=F32)
      s = s + bias_ref[kh]
      if masked_keys:
        s = s + start_mask
      sink = sink_cols[kh]
      m = jnp.maximum(jnp.max(s, axis=-1, keepdims=True), sink)
      p = jnp.exp(s - m)
      denom = jnp.sum(p, axis=-1, keepdims=True) + jnp.exp(sink - m)
      o = jnp.dot(p.astype(BF16), vband, preferred_element_type=F32) * (1.0 / denom)
      o = jnp.concatenate([o[g * CHUNK:(g + 1) * CHUNK, :] for g in range(GROUP)], axis=1)
      mixed_ref[q0:q0 + CHUNK, CONV_WIDTH + kh * GROUP * HEAD_DIM:
                CONV_WIDTH + (kh + 1) * GROUP * HEAD_DIM] = o.astype(BF16)

  def conv_group(c0):
    cols = slice(c0, c0 + CONV_COLS)
    cb_cols, cc_cols, cu_cols = (slice(off + c0, off + c0 + CONV_COLS) for off in (OFF_CB, OFF_CC, OFF_CU))
    for s in range(n_seg):
      for r in range(0, seg_len, CONV_ROWS):
        zr = s * seg_len + r
        gr = s * g_stride + G_PAD + r
        gs_ref[gr:gr + CONV_ROWS, cols] = (
            zc_ref[zr:zr + CONV_ROWS, cc_cols] * zc_ref[zr:zr + CONV_ROWS, cu_cols])
    for s in range(n_seg):
      for r in range(0, seg_len, CONV_ROWS):
        zr = s * seg_len + r
        gr = s * g_stride + G_PAD + r
        conv = convw_ref[0:1, cols] * gs_ref[gr - 2:gr - 2 + CONV_ROWS, cols]
        conv = conv + convw_ref[1:2, cols] * gs_ref[gr - 1:gr - 1 + CONV_ROWS, cols]
        conv = conv + convw_ref[2:3, cols] * gs_ref[gr:gr + CONV_ROWS, cols]
        mixed_ref[zr:zr + CONV_ROWS, cols] = (zc_ref[zr:zr + CONV_ROWS, cb_cols] * conv).astype(BF16)

  pieces = [off + c0 for c0 in range(0, CONV_WIDTH, CONV_COLS) for off in (OFF_CB, OFF_CC, OFF_CU)]
  assert len(pieces) <= n_items
  for item in range(n_items):
    if item < len(pieces):
      cols = slice(pieces[item], pieces[item] + CONV_COLS)
      zc_ref[:, cols] = jnp.dot(h_ref[...], win_ref[:, cols], preferred_element_type=F32)
    attend(item)
    if item < len(pieces) and item % 3 == 2:
      conv_group(pieces[item] - OFF_CU)

  for s in range(n_seg):
    g_end = s * g_stride + G_PAD + seg_len
    nconv_ref[s] = gs_ref[g_end - (CONV_K - 1):g_end, :]


def _cast_specs(groups, n_steps, step_of):
  in_specs, out_specs, out_shapes = [], [], []
  for group in groups:
    rows, cols = group[0].shape
    assert rows % (n_steps * BF16_ROWS) == 0 and all(w.shape == (rows, cols) for w in group)
    index_map = lambda *ids: (step_of(*ids), 0)
    in_specs += [pl.BlockSpec((rows // n_steps, cols), index_map) for _ in group]
    out_specs.append(pl.BlockSpec((rows // n_steps, cols * len(group)), index_map))
    out_shapes.append(jax.ShapeDtypeStruct((rows, cols * len(group)), BF16))
  return in_specs, out_specs, out_shapes


def _cast_group(src_refs, dst_ref):
  n = len(src_refs)
  for j in range(src_refs[0].shape[1] // FFN_TILE):
    for k, src in enumerate(src_refs):
      dst_ref[:, (j * n + k) * FFN_TILE:(j * n + k + 1) * FFN_TILE] = (
          src[:, j * FFN_TILE:(j + 1) * FFN_TILE].astype(BF16))


def _cast_all(src_refs, dst_refs, group_sizes):
  for size, dst in zip(group_sizes, dst_refs):
    _cast_group(src_refs[:size], dst)
    src_refs = src_refs[size:]


def _mixer_call(x, state, cache_k, cache_v, bkt, rel_table, sinks, g_pre, w_in, conv_w, *, has_state,
                cast_weights=()):
  n_seq, seq_len, _ = x.shape
  if has_state:
    n_seg, seg_len = MIX_ROWS // seq_len, seq_len
    grid = (n_seq // n_seg, 1)
    x = x.reshape(n_seq // n_seg, MIX_ROWS, D_MODEL)
    x_map = lambda b, s: (b, 0, 0)
    seq_map = lambda b, s: (b, 0, 0)
  else:
    n_seg, seg_len = 1, MIX_ROWS
    grid = (n_seq, seq_len // MIX_ROWS)
    x_map = lambda b, s: (b, s, 0)
    seq_map = lambda b, s: (b, 0, 0)
  const2 = lambda b, s: (0, 0)
  once = pl.Buffered(1)

  smem = pl.BlockSpec(memory_space=pltpu.SMEM)
  in_specs = [
      smem, smem,
      pl.BlockSpec((CHUNK, BAND), const2, pipeline_mode=once),
      pl.BlockSpec((None, MIX_ROWS, D_MODEL), x_map),
      pl.BlockSpec((1, D_MODEL), const2, pipeline_mode=once),
      pl.BlockSpec((D_MODEL, IN_COLS), const2, pipeline_mode=once),
      pl.BlockSpec((CONV_K, CONV_WIDTH), const2, pipeline_mode=once),
  ]
  args = [rel_table, sinks, bkt, x, g_pre, w_in, conv_w]
  if has_state:
    in_specs += [
        pl.BlockSpec((n_seg, CONV_K - 1, CONV_WIDTH), seq_map),
        pl.BlockSpec((n_seg, WINDOW, KV_WIDTH), seq_map),
        pl.BlockSpec((n_seg, WINDOW, KV_WIDTH), seq_map),
    ]
    args += [state, cache_k, cache_v]
  cast_in, cast_out, cast_shapes = _cast_specs(cast_weights, grid[0] * grid[1], lambda b, s: b * grid[1] + s)
  in_specs += cast_in
  args += [w for group in cast_weights for w in group]
  out_shape = (
      jax.ShapeDtypeStruct(x.shape, BF16),
      jax.ShapeDtypeStruct((n_seq, CONV_K - 1, CONV_WIDTH), F32),
      jax.ShapeDtypeStruct((n_seq, WINDOW, KV_WIDTH), F32),
      jax.ShapeDtypeStruct((n_seq, WINDOW, KV_WIDTH), F32),
      *cast_shapes,
  )
  out_specs = (
      pl.BlockSpec((None, MIX_ROWS, D_MODEL), x_map),
      pl.BlockSpec((n_seg, CONV_K - 1, CONV_WIDTH), seq_map),
      pl.BlockSpec((n_seg, WINDOW, KV_WIDTH), seq_map),
      pl.BlockSpec((n_seg, WINDOW, KV_WIDTH), seq_map),
      *cast_out,
  )
  band_rows = n_seg * (WINDOW + seg_len)
  scratch = [
      pltpu.VMEM((N_KV_HEADS, GROUP * CHUNK, BAND), F32),
      pltpu.VMEM((MIX_ROWS, D_MODEL), BF16),
      pltpu.VMEM((MIX_ROWS, 3 * CONV_WIDTH), F32),
      pltpu.VMEM((N_HEADS, MIX_ROWS, HEAD_DIM), BF16),
      pltpu.VMEM((MIX_ROWS, 2 * KV_WIDTH), F32),
      pltpu.VMEM((N_KV_HEADS, band_rows, HEAD_DIM), BF16),
      pltpu.VMEM((N_KV_HEADS, band_rows, HEAD_DIM), BF16),
      pltpu.VMEM((n_seg * (G_PAD + seg_len), CONV_WIDTH), F32),
  ]
  kernel = functools.partial(_mixer_kernel, n_seg=n_seg, seg_len=seg_len, has_state=has_state,
                             cast_groups=tuple(len(group) for group in cast_weights))
  return pl.pallas_call(
      kernel,
      grid=grid,
      in_specs=in_specs,
      out_specs=out_specs,
      out_shape=out_shape,
      scratch_shapes=scratch,
      compiler_params=pltpu.CompilerParams(
          dimension_semantics=("arbitrary", "arbitrary"),
          vmem_limit_bytes=V7X_VMEM_LIMIT_BYTES),
      name="mixer_state" if has_state else "mixer_stream",
  )(*args)


def _outproj_kernel(mixed_ref, x_ref, wout_ref, gpm_ref, gpf_ref, *rest, cast_groups):
  rest = list(rest)
  cast_src, rest = rest[:sum(cast_groups)], rest[sum(cast_groups):]
  (x1_ref, hn_ref), rest = rest[:2], rest[2:]
  cast_dst, (acc_ref,) = rest[:len(cast_groups)], rest[len(cast_groups):]
  _cast_all(cast_src, cast_dst, cast_groups)
  gpm = gpm_ref[...]
  gpf = gpf_ref[...]
  part = mixed_ref.shape[0] // OUT_PARTS
  for r0 in range(0, mixed_ref.shape[0], part):
    acc_ref[r0:r0 + part, :] = jnp.dot(
        mixed_ref[r0:r0 + part, :], wout_ref[...], preferred_element_type=F32)
    for r in range(r0, r0 + part, NORM_ROWS):
      x1 = x_ref[r:r + NORM_ROWS, :] + _rms_scale(acc_ref[r:r + NORM_ROWS, :], gpm)
      x1_ref[r:r + NORM_ROWS, :] = x1
      hn_ref[r:r + NORM_ROWS, :] = _rms_scale(x1, gpf).astype(BF16)


def _outproj_call(mixed, x, w_out, g_post_mix, g_pre_ffn, cast_weights=()):
  rows = x.shape[0]
  row_map = lambda i: (i, 0)
  const2 = lambda i: (0, 0)
  once = pl.Buffered(1)
  cast_in, cast_out, cast_shapes = _cast_specs(cast_weights, rows // OUT_ROWS, lambda i: i)
  return pl.pallas_call(
      functools.partial(_outproj_kernel, cast_groups=tuple(len(group) for group in cast_weights)),
      grid=(rows // OUT_ROWS,),
      in_specs=[
          pl.BlockSpec((OUT_ROWS, D_MODEL), row_map),
          pl.BlockSpec((OUT_ROWS, D_MODEL), row_map),
          pl.BlockSpec((D_MODEL, D_MODEL), const2, pipeline_mode=once),
          pl.BlockSpec((1, D_MODEL), const2, pipeline_mode=once),
          pl.BlockSpec((1, D_MODEL), const2, pipeline_mode=once),
          *cast_in,
      ],
      out_specs=(pl.BlockSpec((OUT_ROWS, D_MODEL), row_map), pl.BlockSpec((OUT_ROWS, D_MODEL), row_map),
                 *cast_out),
      out_shape=(jax.ShapeDtypeStruct((rows, D_MODEL), F32), jax.ShapeDtypeStruct((rows, D_MODEL), BF16),
                 *cast_shapes),
      scratch_shapes=[pltpu.VMEM((OUT_ROWS, D_MODEL), F32)],
      compiler_params=pltpu.CompilerParams(
          dimension_semantics=("arbitrary",), vmem_limit_bytes=V7X_VMEM_LIMIT_BYTES),
      name="outproj",
  )(mixed, x, w_out, g_post_mix, g_pre_ffn, *[w for group in cast_weights for w in group])


def _ffn_tile(i, j):
  return jnp.where(i % 2 == 0, j, D_FF // FFN_TILE - 1 - j)


def _ffn_kernel(x1_ref, hn_ref, gqf_ref, wgu_ref, wd_ref, y_ref):
  j = pl.program_id(1)
  rows = y_ref.shape[0]

  last = pl.num_programs(1) - 1

  def swiglu():
    gate_up = jnp.dot(hn_ref[...], wgu_ref[...], preferred_element_type=F32)
    gate, up = gate_up[:, :FFN_TILE], gate_up[:, FFN_TILE:]
    return (gate * (1.0 / (1.0 + jnp.exp(-gate))) * up).astype(BF16)

  @pl.when(j == 0)
  def _():
    y_ref[...] = jnp.dot(swiglu(), wd_ref[...], preferred_element_type=F32)

  @pl.when((j > 0) & (j < last))
  def _():
    y_ref[...] += jnp.dot(swiglu(), wd_ref[...], preferred_element_type=F32)

  @pl.when(j == last)
  def _():
    mid = swiglu()
    gqf = gqf_ref[...]
    part = rows // FFN_LAST_PARTS
    for r0 in range(0, rows, part):
      y_ref[r0:r0 + part, :] += jnp.dot(mid[r0:r0 + part, :], wd_ref[...], preferred_element_type=F32)
      for r in range(r0, r0 + part, NORM_ROWS):
        y_ref[r:r + NORM_ROWS, :] = x1_ref[r:r + NORM_ROWS, :] + _rms_scale(y_ref[r:r + NORM_ROWS, :], gqf)


def _ffn_call(x1, hn, g_post_ffn, w_gate_up, w_down):
  rows = x1.shape[0]
  assert D_FF % FFN_TILE == 0
  row_map = lambda i, j: (i, 0)
  return pl.pallas_call(
      _ffn_kernel,
      grid=(rows // FFN_ROWS, D_FF // FFN_TILE),
      in_specs=[
          pl.BlockSpec((FFN_ROWS, D_MODEL), row_map),
          pl.BlockSpec((FFN_ROWS, D_MODEL), row_map),
          pl.BlockSpec((1, D_MODEL), lambda i, j: (0, 0), pipeline_mode=pl.Buffered(1)),
          pl.BlockSpec((D_MODEL, 2 * FFN_TILE), lambda i, j: (0, _ffn_tile(i, j))),
          pl.BlockSpec((FFN_TILE, D_MODEL), lambda i, j: (_ffn_tile(i, j), 0)),
      ],
      out_specs=pl.BlockSpec((FFN_ROWS, D_MODEL), row_map),
      out_shape=jax.ShapeDtypeStruct((rows, D_MODEL), F32),
      compiler_params=pltpu.CompilerParams(
          dimension_semantics=("arbitrary", "arbitrary"),
          vmem_limit_bytes=V7X_VMEM_LIMIT_BYTES),
      name="ffn",
  )(x1, hn, g_post_ffn, w_gate_up, w_down)


def kernel(x_prompt, x_sample, state_conv, cache_k, cache_v, rel_table, g_pre_mix, w_in, conv_w,
           attn_sinks, w_out, g_post_mix, g_pre_ffn, w_gate, w_up, w_down, g_post_ffn):
  depth, n_dec, cache_len = cache_k.shape[:3]
  batch, seq_len, _ = x_prompt.shape
  assert depth == 1 and cache_len == WINDOW and x_sample.shape[1] == CHUNK
  assert seq_len % MIX_ROWS == 0 and MIX_ROWS % CHUNK == 0

  rel = (np.arange(BAND) - WINDOW)[None, :] - np.arange(CHUNK)[:, None]
  bkt = jnp.asarray(_t5_bucket(rel), jnp.int32)

  row = lambda g: g[0].reshape(1, D_MODEL)
  win = w_in[0].astype(BF16)
  mix_args = (bkt, rel_table, attn_sinks[0], row(g_pre_mix), win, conv_w[0])
  mixed_p, conv_p, k_p, v_p, wgu, wo = _mixer_call(
      x_prompt, None, None, None, *mix_args, has_state=False,
      cast_weights=((w_gate[0], w_up[0]), (w_out[0],)))
  mixed_s, conv_s, k_s, v_s = _mixer_call(
      x_sample, state_conv[0], cache_k[0].reshape(n_dec, WINDOW, KV_WIDTH),
      cache_v[0].reshape(n_dec, WINDOW, KV_WIDTH), *mix_args, has_state=True)

  out_args = (wo, row(g_post_mix), row(g_pre_ffn))
  x1_p, hn_p, wd = _outproj_call(
      mixed_p.reshape(-1, D_MODEL), x_prompt.reshape(-1, D_MODEL), *out_args, cast_weights=((w_down[0],),))
  x1_s, hn_s = _outproj_call(mixed_s.reshape(-1, D_MODEL), x_sample.reshape(-1, D_MODEL), *out_args)
  ffn_args = (row(g_post_ffn), wgu, wd)
  y_p = _ffn_call(x1_p, hn_p, *ffn_args)
  y_s = _ffn_call(x1_s, hn_s, *ffn_args)

  heads = lambda a: a.reshape(1, a.shape[0], WINDOW, N_KV_HEADS, HEAD_DIM)
  return (y_p.reshape(x_prompt.shape), y_s.reshape(x_sample.shape),
          conv_p[None], heads(k_p), heads(v_p), conv_s[None], heads(k_s), heads(v_s))
```

```python
import functools
import math

import jax
import jax.numpy as jnp
import numpy as np
from jax import lax
from jax.experimental import pallas as pl
from jax.experimental.pallas import tpu as pltpu

D_MODEL = 2048
CHUNK = 64
HEAD_DIM = 64
CONV_WIDTH = D_MODEL // 2
CONV_K = 3
N_HEADS = (D_MODEL - CONV_WIDTH) // HEAD_DIM
N_KV_HEADS = 4
GROUP = N_HEADS // N_KV_HEADS
ATTN_WIDTH = N_HEADS * HEAD_DIM
KV_WIDTH = N_KV_HEADS * HEAD_DIM
IN_COLS = 3 * CONV_WIDTH + ATTN_WIDTH + 2 * KV_WIDTH
WINDOW = 128
BAND = WINDOW + CHUNK
NUM_BUCKETS = 32
MAX_DISTANCE = 128
D_FF = -(-8 * D_MODEL // (3 * 256)) * 256
EPS = 1e-6
NEG = -1e30

OFF_CB, OFF_CC, OFF_CU = 0, CONV_WIDTH, 2 * CONV_WIDTH
OFF_Q = 3 * CONV_WIDTH
OFF_KV = OFF_Q + ATTN_WIDTH

SUBLANES = 8
BF16_ROWS = 16
V7X_VMEM_LIMIT_BYTES = 61 << 20
MIX_ROWS = 512
OUT_ROWS = 512
OUT_PARTS = 4
FFN_ROWS = 1024
FFN_TILE = 512
FFN_LAST_PARTS = 4
CONV_COLS = 512
NORM_ROWS = 16
NORM_UNROLL = 32
CONV_ROWS = 64
G_PAD = SUBLANES

F32 = jnp.float32
BF16 = jnp.bfloat16


def _t5_bucket(rel):
  half = NUM_BUCKETS // 2
  ret = np.where(rel > 0, half, 0)
  n = np.abs(rel)
  max_exact = half // 2
  nf = np.maximum(n, 1).astype(np.float32)
  large = max_exact + (np.log(nf / np.float32(max_exact)) / np.float32(math.log(MAX_DISTANCE / max_exact))
                       * np.float32(half - max_exact)).astype(np.int32)
  large = np.minimum(large, half - 1)
  return ret + np.where(n < max_exact, n, large)


def _rms_scale(y, gain):
  return y * lax.rsqrt(jnp.mean(y * y, axis=-1, keepdims=True) + EPS) * gain


def _mixer_kernel(tbl_ref, sink_ref, bkt_ref, x_ref, gpre_ref, win_ref, convw_ref, *rest,
                  n_seg, seg_len, has_state, cast_groups):
  rest = list(rest)
  if has_state:
    state_ref, ck_ref, cv_ref = rest[:3]
    rest = rest[3:]
  cast_src, rest = rest[:sum(cast_groups)], rest[sum(cast_groups):]
  (mixed_ref, nconv_ref, nk_ref, nv_ref), rest = rest[:4], rest[4:]
  cast_dst, rest = rest[:len(cast_groups)], rest[len(cast_groups):]
  bias_ref, h_ref, zc_ref, q_ref, kvf_ref, kb_ref, vb_ref, gs_ref = rest
  rows = n_seg * seg_len
  g_stride = G_PAD + seg_len
  b_stride = WINDOW + seg_len
  first_call = (pl.program_id(0) == 0) & (pl.program_id(1) == 0)
  seq_start = pl.program_id(1) == 0

  @pl.when(first_call)
  def _():
    bkt = bkt_ref[...]
    for h in range(N_HEADS):
      acc = jnp.zeros((CHUNK, BAND), F32)
      for j in range(NUM_BUCKETS):
        acc = jnp.where(bkt == j, tbl_ref[j, h], acc)
      g = h % GROUP
      bias_ref[h // GROUP, g * CHUNK:(g + 1) * CHUNK, :] = acc

  if has_state:
    for s in range(n_seg):
      gs_ref[s * g_stride + G_PAD - (CONV_K - 1):s * g_stride + G_PAD, :] = state_ref[s]
      for kh in range(N_KV_HEADS):
        kb_ref[kh, s * b_stride:s * b_stride + WINDOW, :] = (
            ck_ref[s, :, kh * HEAD_DIM:(kh + 1) * HEAD_DIM].astype(BF16))
        vb_ref[kh, s * b_stride:s * b_stride + WINDOW, :] = (
            cv_ref[s, :, kh * HEAD_DIM:(kh + 1) * HEAD_DIM].astype(BF16))
  else:
    @pl.when(seq_start)
    def _():
      gs_ref[0:G_PAD, :] = jnp.zeros((G_PAD, CONV_WIDTH), F32)
      kb_ref[:, 0:WINDOW, :] = jnp.zeros((N_KV_HEADS, WINDOW, HEAD_DIM), BF16)
      vb_ref[:, 0:WINDOW, :] = jnp.zeros((N_KV_HEADS, WINDOW, HEAD_DIM), BF16)

    @pl.when(jnp.logical_not(seq_start))
    def _():
      gs_ref[0:G_PAD, :] = gs_ref[seg_len:seg_len + G_PAD, :]
      kb_ref[:, 0:WINDOW, :] = kb_ref[:, seg_len:seg_len + WINDOW, :]
      vb_ref[:, 0:WINDOW, :] = vb_ref[:, seg_len:seg_len + WINDOW, :]

  gpre = gpre_ref[...]

  def norm_body(i, carry):
    r = pl.multiple_of(i * NORM_ROWS, NORM_ROWS)
    h_ref[pl.ds(r, NORM_ROWS), :] = _rms_scale(x_ref[pl.ds(r, NORM_ROWS), :], gpre).astype(BF16)
    return carry

  lax.fori_loop(0, rows // NORM_ROWS, norm_body, 0, unroll=NORM_UNROLL)

  _cast_all(cast_src, cast_dst, cast_groups)

  for c0 in range(0, ATTN_WIDTH, CONV_COLS):
    zc_ref[:, 0:CONV_COLS] = jnp.dot(
        h_ref[...], win_ref[:, OFF_Q + c0:OFF_Q + c0 + CONV_COLS], preferred_element_type=F32)
    for hh in range(CONV_COLS // HEAD_DIM):
      q_ref[c0 // HEAD_DIM + hh] = (
          zc_ref[:, hh * HEAD_DIM:(hh + 1) * HEAD_DIM] * (HEAD_DIM ** -0.5)).astype(BF16)

  kvf_ref[...] = jnp.dot(h_ref[...], win_ref[:, OFF_KV:OFF_KV + 2 * KV_WIDTH], preferred_element_type=F32)
  for s in range(n_seg):
    for kh in range(N_KV_HEADS):
      kb_ref[kh, s * b_stride + WINDOW:(s + 1) * b_stride, :] = (
          kvf_ref[s * seg_len:(s + 1) * seg_len, kh * HEAD_DIM:(kh + 1) * HEAD_DIM].astype(BF16))
      vb_ref[kh, s * b_stride + WINDOW:(s + 1) * b_stride, :] = (
          kvf_ref[s * seg_len:(s + 1) * seg_len,
                  KV_WIDTH + kh * HEAD_DIM:KV_WIDTH + (kh + 1) * HEAD_DIM].astype(BF16))

  for s in range(n_seg):
    if seg_len >= WINDOW:
      nk_ref[s] = kvf_ref[(s + 1) * seg_len - WINDOW:(s + 1) * seg_len, 0:KV_WIDTH]
      nv_ref[s] = kvf_ref[(s + 1) * seg_len - WINDOW:(s + 1) * seg_len, KV_WIDTH:2 * KV_WIDTH]
    else:
      nk_ref[s, 0:WINDOW - seg_len, :] = ck_ref[s, seg_len:WINDOW, :]
      nv_ref[s, 0:WINDOW - seg_len, :] = cv_ref[s, seg_len:WINDOW, :]
      nk_ref[s, WINDOW - seg_len:WINDOW, :] = kvf_ref[s * seg_len:(s + 1) * seg_len, 0:KV_WIDTH]
      nv_ref[s, WINDOW - seg_len:WINDOW, :] = kvf_ref[s * seg_len:(s + 1) * seg_len, KV_WIDTH:2 * KV_WIDTH]

  row_id = lax.broadcasted_iota(jnp.int32, (GROUP * CHUNK, 1), 0)
  sink_cols = []
  for kh in range(N_KV_HEADS):
    col = jnp.full((GROUP * CHUNK, 1), sink_ref[kh * GROUP + GROUP - 1], F32)
    for g in range(GROUP - 2, -1, -1):
      col = jnp.where(row_id < (g + 1) * CHUNK, sink_ref[kh * GROUP + g], col)
    sink_cols.append(col)
  n_items = rows // CHUNK
  assert n_seg == 1 or seg_len == CHUNK
  band_step = CHUNK if n_seg == 1 else b_stride

  key_row = lax.broadcasted_iota(jnp.int32, (1, BAND), 1)

  def attend(item):
    q0 = item * CHUNK
    b0 = item * band_step
    masked_keys = 0 if has_state else max(WINDOW - item * CHUNK, 0)
    if masked_keys:
      start_mask = jnp.where(seq_start & (key_row < masked_keys), NEG, 0.0)
    for kh in range(N_KV_HEADS):
      qs = jnp.concatenate(
          [q_ref[kh * GROUP + g, q0:q0 + CHUNK, :] for g in range(GROUP)], axis=0)
      kband = kb_ref[kh, b0:b0 + BAND, :]
      vband = vb_ref[kh, b0:b0 + BAND, :]
      s = lax.dot_general(qs, kband, (((1,), (1,)), ((), ())), preferred_element_type=F32)
      s = s + bias_ref[kh]
      if masked_keys:
        s = s + start_mask
      sink = sink_cols[kh]
      m = jnp.maximum(jnp.max(s, axis=-1, keepdims=True), sink)
      p = jnp.exp(s - m)
      denom = jnp.sum(p, axis=-1, keepdims=True) + jnp.exp(sink - m)
      o = jnp.dot(p.astype(BF16), vband, preferred_element_type=F32) * (1.0 / denom)
      o = jnp.concatenate([o[g * CHUNK:(g + 1) * CHUNK, :] for g in range(GROUP)], axis=1)
      mixed_ref[q0:q0 + CHUNK, CONV_WIDTH + kh * GROUP * HEAD_DIM:
                CONV_WIDTH + (kh + 1) * GROUP * HEAD_DIM] = o.astype(BF16)

  def conv_group(c0):
    cols = slice(c0, c0 + CONV_COLS)
    cb_cols, cc_cols, cu_cols = (slice(off + c0, off + c0 + CONV_COLS) for off in (OFF_CB, OFF_CC, OFF_CU))
    for s in range(n_seg):
      for r in range(0, seg_len, CONV_ROWS):
        zr = s * seg_len + r
        gr = s * g_stride + G_PAD + r
        gs_ref[gr:gr + CONV_ROWS, cols] = (
            zc_ref[zr:zr + CONV_ROWS, cc_cols] * zc_ref[zr:zr + CONV_ROWS, cu_cols])
    for s in range(n_seg):
      for r in range(0, seg_len, CONV_ROWS):
        zr = s * seg_len + r
        gr = s * g_stride + G_PAD + r
        conv = convw_ref[0:1, cols] * gs_ref[gr - 2:gr - 2 + CONV_ROWS, cols]
        conv = conv + convw_ref[1:2, cols] * gs_ref[gr - 1:gr - 1 + CONV_ROWS, cols]
        conv = conv + convw_ref[2:3, cols] * gs_ref[gr:gr + CONV_ROWS, cols]
        mixed_ref[zr:zr + CONV_ROWS, cols] = (zc_ref[zr:zr + CONV_ROWS, cb_cols] * conv).astype(BF16)

  pieces = [off + c0 for c0 in range(0, CONV_WIDTH, CONV_COLS) for off in (OFF_CB, OFF_CC, OFF_CU)]
  assert len(pieces) <= n_items
  for item in range(n_items):
    if item < len(pieces):
      cols = slice(pieces[item], pieces[item] + CONV_COLS)
      zc_ref[:, cols] = jnp.dot(h_ref[...], win_ref[:, cols], preferred_element_type=F32)
    attend(item)
    if item < len(pieces) and item % 3 == 2:
      conv_group(pieces[item] - OFF_CU)

  for s in range(n_seg):
    g_end = s * g_stride + G_PAD + seg_len
    nconv_ref[s] = gs_ref[g_end - (CONV_K - 1):g_end, :]


def _cast_specs(groups, n_steps, step_of):
  in_specs, out_specs, out_shapes = [], [], []
  for group in groups:
    rows, cols = group[0].shape
    assert rows % (n_steps * BF16_ROWS) == 0 and all(w.shape == (rows, cols) for w in group)
    index_map = lambda *ids: (step_of(*ids), 0)
    in_specs += [pl.BlockSpec((rows // n_steps, cols), index_map) for _ in group]
    out_specs.append(pl.BlockSpec((rows // n_steps, cols * len(group)), index_map))
    out_shapes.append(jax.ShapeDtypeStruct((rows, cols * len(group)), BF16))
  return in_specs, out_specs, out_shapes


def _cast_group(src_refs, dst_ref):
  n = len(src_refs)
  for j in range(src_refs[0].shape[1] // FFN_TILE):
    for k, src in enumerate(src_refs):
      dst_ref[:, (j * n + k) * FFN_TILE:(j * n + k + 1) * FFN_TILE] = (
          src[:, j * FFN_TILE:(j + 1) * FFN_TILE].astype(BF16))


def _cast_all(src_refs, dst_refs, group_sizes):
  for size, dst in zip(group_sizes, dst_refs):
    _cast_group(src_refs[:size], dst)
    src_refs = src_refs[size:]


def _mixer_call(x, state, cache_k, cache_v, bkt, rel_table, sinks, g_pre, w_in, conv_w, *, has_state,
                cast_weights=()):
  n_seq, seq_len, _ = x.shape
  if has_state:
    n_seg, seg_len = MIX_ROWS // seq_len, seq_len
    grid = (n_seq // n_seg, 1)
    x = x.reshape(n_seq // n_seg, MIX_ROWS, D_MODEL)
    x_map = lambda b, s: (b, 0, 0)
    seq_map = lambda b, s: (b, 0, 0)
  else:
    n_seg, seg_len = 1, MIX_ROWS
    grid = (n_seq, seq_len // MIX_ROWS)
    x_map = lambda b, s: (b, s, 0)
    seq_map = lambda b, s: (b, 0, 0)
  const2 = lambda b, s: (0, 0)
  once = pl.Buffered(1)

  smem = pl.BlockSpec(memory_space=pltpu.SMEM)
  in_specs = [
      smem, smem,
      pl.BlockSpec((CHUNK, BAND), const2, pipeline_mode=once),
      pl.BlockSpec((None, MIX_ROWS, D_MODEL), x_map),
      pl.BlockSpec((1, D_MODEL), const2, pipeline_mode=once),
      pl.BlockSpec((D_MODEL, IN_COLS), const2, pipeline_mode=once),
      pl.BlockSpec((CONV_K, CONV_WIDTH), const2, pipeline_mode=once),
  ]
  args = [rel_table, sinks, bkt, x, g_pre, w_in, conv_w]
  if has_state:
    in_specs += [
        pl.BlockSpec((n_seg, CONV_K - 1, CONV_WIDTH), seq_map),
        pl.BlockSpec((n_seg, WINDOW, KV_WIDTH), seq_map),
        pl.BlockSpec((n_seg, WINDOW, KV_WIDTH), seq_map),
    ]
    args += [state, cache_k, cache_v]
  cast_in, cast_out, cast_shapes = _cast_specs(cast_weights, grid[0] * grid[1], lambda b, s: b * grid[1] + s)
  in_specs += cast_in
  args += [w for group in cast_weights for w in group]
  out_shape = (
      jax.ShapeDtypeStruct(x.shape, BF16),
      jax.ShapeDtypeStruct((n_seq, CONV_K - 1, CONV_WIDTH), F32),
      jax.ShapeDtypeStruct((n_seq, WINDOW, KV_WIDTH), F32),
      jax.ShapeDtypeStruct((n_seq, WINDOW, KV_WIDTH), F32),
      *cast_shapes,
  )
  out_specs = (
      pl.BlockSpec((None, MIX_ROWS, D_MODEL), x_map),
      pl.BlockSpec((n_seg, CONV_K - 1, CONV_WIDTH), seq_map),
      pl.BlockSpec((n_seg, WINDOW, KV_WIDTH), seq_map),
      pl.BlockSpec((n_seg, WINDOW, KV_WIDTH), seq_map),
      *cast_out,
  )
  band_rows = n_seg * (WINDOW + seg_len)
  scratch = [
      pltpu.VMEM((N_KV_HEADS, GROUP * CHUNK, BAND), F32),
      pltpu.VMEM((MIX_ROWS, D_MODEL), BF16),
      pltpu.VMEM((MIX_ROWS, 3 * CONV_WIDTH), F32),
      pltpu.VMEM((N_HEADS, MIX_ROWS, HEAD_DIM), BF16),
      pltpu.VMEM((MIX_ROWS, 2 * KV_WIDTH), F32),
      pltpu.VMEM((N_KV_HEADS, band_rows, HEAD_DIM), BF16),
      pltpu.VMEM((N_KV_HEADS, band_rows, HEAD_DIM), BF16),
      pltpu.VMEM((n_seg * (G_PAD + seg_len), CONV_WIDTH), F32),
  ]
  kernel = functools.partial(_mixer_kernel, n_seg=n_seg, seg_len=seg_len, has_state=has_state,
                             cast_groups=tuple(len(group) for group in cast_weights))
  return pl.pallas_call(
      kernel,
      grid=grid,
      in_specs=in_specs,
      out_specs=out_specs,
      out_shape=out_shape,
      scratch_shapes=scratch,
      compiler_params=pltpu.CompilerParams(
          dimension_semantics=("arbitrary", "arbitrary"),
          vmem_limit_bytes=V7X_VMEM_LIMIT_BYTES),
      name="mixer_state" if has_state else "mixer_stream",
  )(*args)


def _outproj_kernel(mixed_ref, x_ref, wout_ref, gpm_ref, gpf_ref, *rest, cast_groups):
  rest = list(rest)
  cast_src, rest = rest[:sum(cast_groups)], rest[sum(cast_groups):]
  (x1_ref, hn_ref), rest = rest[:2], rest[2:]
  cast_dst, (acc_ref,) = rest[:len(cast_groups)], rest[len(cast_groups):]
  _cast_all(cast_src, cast_dst, cast_groups)
  gpm = gpm_ref[...]
  gpf = gpf_ref[...]
  part = mixed_ref.shape[0] // OUT_PARTS
  for r0 in range(0, mixed_ref.shape[0], part):
    acc_ref[r0:r0 + part, :] = jnp.dot(
        mixed_ref[r0:r0 + part, :], wout_ref[...], preferred_element_type=F32)
    for r in range(r0, r0 + part, NORM_ROWS):
      x1 = x_ref[r:r + NORM_ROWS, :] + _rms_scale(acc_ref[r:r + NORM_ROWS, :], gpm)
      x1_ref[r:r + NORM_ROWS, :] = x1
      hn_ref[r:r + NORM_ROWS, :] = _rms_scale(x1, gpf).astype(BF16)


def _outproj_call(mixed, x, w_out, g_post_mix, g_pre_ffn, cast_weights=()):
  rows = x.shape[0]
  row_map = lambda i: (i, 0)
  const2 = lambda i: (0, 0)
  once = pl.Buffered(1)
  cast_in, cast_out, cast_shapes = _cast_specs(cast_weights, rows // OUT_ROWS, lambda i: i)
  return pl.pallas_call(
      functools.partial(_outproj_kernel, cast_groups=tuple(len(group) for group in cast_weights)),
      grid=(rows // OUT_ROWS,),
      in_specs=[
          pl.BlockSpec((OUT_ROWS, D_MODEL), row_map),
          pl.BlockSpec((OUT_ROWS, D_MODEL), row_map),
          pl.BlockSpec((D_MODEL, D_MODEL), const2, pipeline_mode=once),
          pl.BlockSpec((1, D_MODEL), const2, pipeline_mode=once),
          pl.BlockSpec((1, D_MODEL), const2, pipeline_mode=once),
          *cast_in,
      ],
      out_specs=(pl.BlockSpec((OUT_ROWS, D_MODEL), row_map), pl.BlockSpec((OUT_ROWS, D_MODEL), row_map),
                 *cast_out),
      out_shape=(jax.ShapeDtypeStruct((rows, D_MODEL), F32), jax.ShapeDtypeStruct((rows, D_MODEL), BF16),
                 *cast_shapes),
      scratch_shapes=[pltpu.VMEM((OUT_ROWS, D_MODEL), F32)],
      compiler_params=pltpu.CompilerParams(
          dimension_semantics=("arbitrary",), vmem_limit_bytes=V7X_VMEM_LIMIT_BYTES),
      name="outproj",
  )(mixed, x, w_out, g_post_mix, g_pre_ffn, *[w for group in cast_weights for w in group])


def _ffn_tile(i, j):
  return jnp.where(i % 2 == 0, j, D_FF // FFN_TILE - 1 - j)


def _ffn_kernel(x1_ref, hn_ref, gqf_ref, wgu_ref, wd_ref, y_ref):
  j = pl.program_id(1)
  rows = y_ref.shape[0]

  last = pl.num_programs(1) - 1

  def swiglu():
    gate_up = jnp.dot(hn_ref[...], wgu_ref[...], preferred_element_type=F32)
    gate, up = gate_up[:, :FFN_TILE], gate_up[:, FFN_TILE:]
    return (gate * (1.0 / (1.0 + jnp.exp(-gate))) * up).astype(BF16)

  @pl.when(j == 0)
  def _():
    y_ref[...] = jnp.dot(swiglu(), wd_ref[...], preferred_element_type=F32)

  @pl.when((j > 0) & (j < last))
  def _():
    y_ref[...] += jnp.dot(swiglu(), wd_ref[...], preferred_element_type=F32)

  @pl.when(j == last)
  def _():
    mid = swiglu()
    gqf = gqf_ref[...]
    part = rows // FFN_LAST_PARTS
    for r0 in range(0, rows, part):
      y_ref[r0:r0 + part, :] += jnp.dot(mid[r0:r0 + part, :], wd_ref[...], preferred_element_type=F32)
      for r in range(r0, r0 + part, NORM_ROWS):
        y_ref[r:r + NORM_ROWS, :] = x1_ref[r:r + NORM_ROWS, :] + _rms_scale(y_ref[r:r + NORM_ROWS, :], gqf)


def _ffn_call(x1, hn, g_post_ffn, w_gate_up, w_down):
  rows = x1.shape[0]
  assert D_FF % FFN_TILE == 0
  row_map = lambda i, j: (i, 0)
  return pl.pallas_call(
      _ffn_kernel,
      grid=(rows // FFN_ROWS, D_FF // FFN_TILE),
      in_specs=[
          pl.BlockSpec((FFN_ROWS, D_MODEL), row_map),
          pl.BlockSpec((FFN_ROWS, D_MODEL), row_map),
          pl.BlockSpec((1, D_MODEL), lambda i, j: (0, 0), pipeline_mode=pl.Buffered(1)),
          pl.BlockSpec((D_MODEL, 2 * FFN_TILE), lambda i, j: (0, _ffn_tile(i, j))),
          pl.BlockSpec((FFN_TILE, D_MODEL), lambda i, j: (_ffn_tile(i, j), 0)),
      ],
      out_specs=pl.BlockSpec((FFN_ROWS, D_MODEL), row_map),
      out_shape=jax.ShapeDtypeStruct((rows, D_MODEL), F32),
      compiler_params=pltpu.CompilerParams(
          dimension_semantics=("arbitrary", "arbitrary"),
          vmem_limit_bytes=V7X_VMEM_LIMIT_BYTES),
      name="ffn",
  )(x1, hn, g_post_ffn, w_gate_up, w_down)


def kernel(x_prompt, x_sample, state_conv, cache_k, cache_v, rel_table, g_pre_mix, w_in, conv_w,
           attn_sinks, w_out, g_post_mix, g_pre_ffn, w_gate, w_up, w_down, g_post_ffn):
  depth, n_dec, cache_len = cache_k.shape[:3]
  batch, seq_len, _ = x_prompt.shape
  assert depth == 1 and cache_len == WINDOW and x_sample.shape[1] == CHUNK
  assert seq_len % MIX_ROWS == 0 and MIX_ROWS % CHUNK == 0

  rel = (np.arange(BAND) - WINDOW)[None, :] - np.arange(CHUNK)[:, None]
  bkt = jnp.asarray(_t5_bucket(rel), jnp.int32)

  row = lambda g: g[0].reshape(1, D_MODEL)
  win = w_in[0].astype(BF16)
  mix_args = (bkt, rel_table, attn_sinks[0], row(g_pre_mix), win, conv_w[0])
  mixed_p, conv_p, k_p, v_p, wgu, wo = _mixer_call(
      x_prompt, None, None, None, *mix_args, has_state=False,
      cast_weights=((w_gate[0], w_up[0]), (w_out[0],)))
  mixed_s, conv_s, k_s, v_s = _mixer_call(
      x_sample, state_conv[0], cache_k[0].reshape(n_dec, WINDOW, KV_WIDTH),
      cache_v[0].reshape(n_dec, WINDOW, KV_WIDTH), *mix_args, has_state=True)

  out_args = (wo, row(g_post_mix), row(g_pre_ffn))
  x1_p, hn_p, wd = _outproj_call(
      mixed_p.reshape(-1, D_MODEL), x_prompt.reshape(-1, D_MODEL), *out_args, cast_weights=((w_down[0],),))
  x1_s, hn_s = _outproj_call(mixed_s.reshape(-1, D_MODEL), x_sample.reshape(-1, D_MODEL), *out_args)
  ffn_args = (row(g_post_ffn), wgu, wd)
  y_p = _ffn_call(x1_p, hn_p, *ffn_args)
  y_s = _ffn_call(x1_s, hn_s, *ffn_args)

  heads = lambda a: a.reshape(1, a.shape[0], WINDOW, N_KV_HEADS, HEAD_DIM)
  return (y_p.reshape(x_prompt.shape), y_s.reshape(x_sample.shape),
          conv_p[None], heads(k_p), heads(v_p), conv_s[None], heads(k_s), heads(v_s))
```

```python
import functools
import math

import jax
import jax.numpy as jnp
import numpy as np
from jax import lax
from jax.experimental import pallas as pl
from jax.experimental.pallas import tpu as pltpu

D_MODEL = 2048
CHUNK = 64
HEAD_DIM = 64
CONV_WIDTH = D_MODEL // 2
CONV_K = 3
N_HEADS = (D_MODEL - CONV_WIDTH) // HEAD_DIM
N_KV_HEADS = 4
GROUP = N_HEADS // N_KV_HEADS
ATTN_WIDTH = N_HEADS * HEAD_DIM
KV_WIDTH = N_KV_HEADS * HEAD_DIM
IN_COLS = 3 * CONV_WIDTH + ATTN_WIDTH + 2 * KV_WIDTH
WINDOW = 128
BAND = WINDOW + CHUNK
NUM_BUCKETS = 32
MAX_DISTANCE = 128
D_FF = -(-8 * D_MODEL // (3 * 256)) * 256
EPS = 1e-6
NEG = -1e30

OFF_CB, OFF_CC, OFF_CU = 0, CONV_WIDTH, 2 * CONV_WIDTH
OFF_Q = 3 * CONV_WIDTH
OFF_KV = OFF_Q + ATTN_WIDTH

SUBLANES = 8
BF16_ROWS = 16
V7X_VMEM_LIMIT_BYTES = 61 << 20
MIX_ROWS = 512
OUT_ROWS = 512
OUT_PARTS = 4
FFN_ROWS = 1024
FFN_TILE = 512
FFN_LAST_PARTS = 4
CONV_COLS = 512
NORM_ROWS = 16
CONV_ROWS = 64
G_PAD = SUBLANES

F32 = jnp.float32
BF16 = jnp.bfloat16


def _t5_bucket(rel):
  half = NUM_BUCKETS // 2
  ret = np.where(rel > 0, half, 0)
  n = np.abs(rel)
  max_exact = half // 2
  nf = np.maximum(n, 1).astype(np.float32)
  large = max_exact + (np.log(nf / np.float32(max_exact)) / np.float32(math.log(MAX_DISTANCE / max_exact))
                       * np.float32(half - max_exact)).astype(np.int32)
  large = np.minimum(large, half - 1)
  return ret + np.where(n < max_exact, n, large)


def _rms_scale(y, gain):
  return y * lax.rsqrt(jnp.mean(y * y, axis=-1, keepdims=True) + EPS) * gain


def _mixer_kernel(tbl_ref, sink_ref, bkt_ref, x_ref, gpre_ref, win_ref, convw_ref, *rest,
                  n_seg, seg_len, has_state, cast_groups):
  rest = list(rest)
  if has_state:
    state_ref, ck_ref, cv_ref = rest[:3]
    rest = rest[3:]
  cast_src, rest = rest[:sum(cast_groups)], rest[sum(cast_groups):]
  (mixed_ref, nconv_ref, nk_ref, nv_ref), rest = rest[:4], rest[4:]
  cast_dst, rest = rest[:len(cast_groups)], rest[len(cast_groups):]
  bias_ref, h_ref, zc_ref, q_ref, kvf_ref, kb_ref, vb_ref, gs_ref = rest
  rows = n_seg * seg_len
  g_stride = G_PAD + seg_len
  b_stride = WINDOW + seg_len
  first_call = (pl.program_id(0) == 0) & (pl.program_id(1) == 0)
  seq_start = pl.program_id(1) == 0

  @pl.when(first_call)
  def _():
    bkt = bkt_ref[...]
    for h in range(N_HEADS):
      acc = jnp.zeros((CHUNK, BAND), F32)
      for j in range(NUM_BUCKETS):
        acc = jnp.where(bkt == j, tbl_ref[j, h], acc)
      g = h % GROUP
      bias_ref[h // GROUP, g * CHUNK:(g + 1) * CHUNK, :] = acc

  if has_state:
    for s in range(n_seg):
      gs_ref[s * g_stride + G_PAD - (CONV_K - 1):s * g_stride + G_PAD, :] = state_ref[s]
      for kh in range(N_KV_HEADS):
        kb_ref[kh, s * b_stride:s * b_stride + WINDOW, :] = (
            ck_ref[s, :, kh * HEAD_DIM:(kh + 1) * HEAD_DIM].astype(BF16))
        vb_ref[kh, s * b_stride:s * b_stride + WINDOW, :] = (
            cv_ref[s, :, kh * HEAD_DIM:(kh + 1) * HEAD_DIM].astype(BF16))
  else:
    @pl.when(seq_start)
    def _():
      gs_ref[0:G_PAD, :] = jnp.zeros((G_PAD, CONV_WIDTH), F32)
      kb_ref[:, 0:WINDOW, :] = jnp.zeros((N_KV_HEADS, WINDOW, HEAD_DIM), BF16)
      vb_ref[:, 0:WINDOW, :] = jnp.zeros((N_KV_HEADS, WINDOW, HEAD_DIM), BF16)

    @pl.when(jnp.logical_not(seq_start))
    def _():
      gs_ref[0:G_PAD, :] = gs_ref[seg_len:seg_len + G_PAD, :]
      kb_ref[:, 0:WINDOW, :] = kb_ref[:, seg_len:seg_len + WINDOW, :]
      vb_ref[:, 0:WINDOW, :] = vb_ref[:, seg_len:seg_len + WINDOW, :]

  gpre = gpre_ref[...]
  for r in range(0, rows, NORM_ROWS):
    h_ref[r:r + NORM_ROWS, :] = _rms_scale(x_ref[r:r + NORM_ROWS, :], gpre).astype(BF16)

  _cast_all(cast_src, cast_dst, cast_groups)

  for c0 in range(0, ATTN_WIDTH, CONV_COLS):
    zc_ref[:, 0:CONV_COLS] = jnp.dot(
        h_ref[...], win_ref[:, OFF_Q + c0:OFF_Q + c0 + CONV_COLS], preferred_element_type=F32)
    for hh in range(CONV_COLS // HEAD_DIM):
      q_ref[c0 // HEAD_DIM + hh] = (
          zc_ref[:, hh * HEAD_DIM:(hh + 1) * HEAD_DIM] * (HEAD_DIM ** -0.5)).astype(BF16)

  kvf_ref[...] = jnp.dot(h_ref[...], win_ref[:, OFF_KV:OFF_KV + 2 * KV_WIDTH], preferred_element_type=F32)
  for s in range(n_seg):
    for kh in range(N_KV_HEADS):
      kb_ref[kh, s * b_stride + WINDOW:(s + 1) * b_stride, :] = (
          kvf_ref[s * seg_len:(s + 1) * seg_len, kh * HEAD_DIM:(kh + 1) * HEAD_DIM].astype(BF16))
      vb_ref[kh, s * b_stride + WINDOW:(s + 1) * b_stride, :] = (
          kvf_ref[s * seg_len:(s + 1) * seg_len,
                  KV_WIDTH + kh * HEAD_DIM:KV_WIDTH + (kh + 1) * HEAD_DIM].astype(BF16))

  for s in range(n_seg):
    if seg_len >= WINDOW:
      nk_ref[s] = kvf_ref[(s + 1) * seg_len - WINDOW:(s + 1) * seg_len, 0:KV_WIDTH]
      nv_ref[s] = kvf_ref[(s + 1) * seg_len - WINDOW:(s + 1) * seg_len, KV_WIDTH:2 * KV_WIDTH]
    else:
      nk_ref[s, 0:WINDOW - seg_len, :] = ck_ref[s, seg_len:WINDOW, :]
      nv_ref[s, 0:WINDOW - seg_len, :] = cv_ref[s, seg_len:WINDOW, :]
      nk_ref[s, WINDOW - seg_len:WINDOW, :] = kvf_ref[s * seg_len:(s + 1) * seg_len, 0:KV_WIDTH]
      nv_ref[s, WINDOW - seg_len:WINDOW, :] = kvf_ref[s * seg_len:(s + 1) * seg_len, KV_WIDTH:2 * KV_WIDTH]

  row_id = lax.broadcasted_iota(jnp.int32, (GROUP * CHUNK, 1), 0)
  sink_cols = []
  for kh in range(N_KV_HEADS):
    col = jnp.full((GROUP * CHUNK, 1), sink_ref[kh * GROUP + GROUP - 1], F32)
    for g in range(GROUP - 2, -1, -1):
      col = jnp.where(row_id < (g + 1) * CHUNK, sink_ref[kh * GROUP + g], col)
    sink_cols.append(col)
  n_items = rows // CHUNK
  assert n_seg == 1 or seg_len == CHUNK
  band_step = CHUNK if n_seg == 1 else b_stride

  key_row = lax.broadcasted_iota(jnp.int32, (1, BAND), 1)

  def attend(item):
    q0 = item * CHUNK
    b0 = item * band_step
    masked_keys = 0 if has_state else max(WINDOW - item * CHUNK, 0)
    if masked_keys:
      start_mask = jnp.where(seq_start & (key_row < masked_keys), NEG, 0.0)
    for kh in range(N_KV_HEADS):
      qs = jnp.concatenate(
          [q_ref[kh * GROUP + g, q0:q0 + CHUNK, :] for g in range(GROUP)], axis=0)
      kband = kb_ref[kh, b0:b0 + BAND, :]
      vband = vb_ref[kh, b0:b0 + BAND, :]
      s = lax.dot_general(qs, kband, (((1,), (1,)), ((), ())), preferred_element_type=F32)
      s = s + bias_ref[kh]
      if masked_keys:
        s = s + start_mask
      sink = sink_cols[kh]
      m = jnp.maximum(jnp.max(s, axis=-1, keepdims=True), sink)
      p = jnp.exp(s - m)
      denom = jnp.sum(p, axis=-1, keepdims=True) + jnp.exp(sink - m)
      o = jnp.dot(p.astype(BF16), vband, preferred_element_type=F32) * (1.0 / denom)
      o = jnp.concatenate([o[g * CHUNK:(g + 1) * CHUNK, :] for g in range(GROUP)], axis=1)
      mixed_ref[q0:q0 + CHUNK, CONV_WIDTH + kh * GROUP * HEAD_DIM:
                CONV_WIDTH + (kh + 1) * GROUP * HEAD_DIM] = o.astype(BF16)

  def conv_group(c0):
    cols = slice(c0, c0 + CONV_COLS)
    cb_cols, cc_cols, cu_cols = (slice(off + c0, off + c0 + CONV_COLS) for off in (OFF_CB, OFF_CC, OFF_CU))
    for s in range(n_seg):
      for r in range(0, seg_len, CONV_ROWS):
        zr = s * seg_len + r
        gr = s * g_stride + G_PAD + r
        gs_ref[gr:gr + CONV_ROWS, cols] = (
            zc_ref[zr:zr + CONV_ROWS, cc_cols] * zc_ref[zr:zr + CONV_ROWS, cu_cols])
    for s in range(n_seg):
      for r in range(0, seg_len, CONV_ROWS):
        zr = s * seg_len + r
        gr = s * g_stride + G_PAD + r
        conv = convw_ref[0:1, cols] * gs_ref[gr - 2:gr - 2 + CONV_ROWS, cols]
        conv = conv + convw_ref[1:2, cols] * gs_ref[gr - 1:gr - 1 + CONV_ROWS, cols]
        conv = conv + convw_ref[2:3, cols] * gs_ref[gr:gr + CONV_ROWS, cols]
        mixed_ref[zr:zr + CONV_ROWS, cols] = (zc_ref[zr:zr + CONV_ROWS, cb_cols] * conv).astype(BF16)

  pieces = [off + c0 for c0 in range(0, CONV_WIDTH, CONV_COLS) for off in (OFF_CB, OFF_CC, OFF_CU)]
  assert len(pieces) <= n_items
  for item in range(n_items):
    if item < len(pieces):
      cols = slice(pieces[item], pieces[item] + CONV_COLS)
      zc_ref[:, cols] = jnp.dot(h_ref[...], win_ref[:, cols], preferred_element_type=F32)
    attend(item)
    if item < len(pieces) and item % 3 == 2:
      conv_group(pieces[item] - OFF_CU)

  for s in range(n_seg):
    g_end = s * g_stride + G_PAD + seg_len
    nconv_ref[s] = gs_ref[g_end - (CONV_K - 1):g_end, :]


def _cast_specs(groups, n_steps, step_of):
  in_specs, out_specs, out_shapes = [], [], []
  for group in groups:
    rows, cols = group[0].shape
    assert rows % (n_steps * BF16_ROWS) == 0 and all(w.shape == (rows, cols) for w in group)
    index_map = lambda *ids: (step_of(*ids), 0)
    in_specs += [pl.BlockSpec((rows // n_steps, cols), index_map) for _ in group]
    out_specs.append(pl.BlockSpec((rows // n_steps, cols * len(group)), index_map))
    out_shapes.append(jax.ShapeDtypeStruct((rows, cols * len(group)), BF16))
  return in_specs, out_specs, out_shapes


def _cast_group(src_refs, dst_ref):
  n = len(src_refs)
  for j in range(src_refs[0].shape[1] // FFN_TILE):
    for k, src in enumerate(src_refs):
      dst_ref[:, (j * n + k) * FFN_TILE:(j * n + k + 1) * FFN_TILE] = (
          src[:, j * FFN_TILE:(j + 1) * FFN_TILE].astype(BF16))


def _cast_all(src_refs, dst_refs, group_sizes):
  for size, dst in zip(group_sizes, dst_refs):
    _cast_group(src_refs[:size], dst)
    src_refs = src_refs[size:]


def _mixer_call(x, state, cache_k, cache_v, bkt, rel_table, sinks, g_pre, w_in, conv_w, *, has_state,
                cast_weights=()):
  n_seq, seq_len, _ = x.shape
  if has_state:
    n_seg, seg_len = MIX_ROWS // seq_len, seq_len
    grid = (n_seq // n_seg, 1)
    x = x.reshape(n_seq // n_seg, MIX_ROWS, D_MODEL)
    x_map = lambda b, s: (b, 0, 0)
    seq_map = lambda b, s: (b, 0, 0)
  else:
    n_seg, seg_len = 1, MIX_ROWS
    grid = (n_seq, seq_len // MIX_ROWS)
    x_map = lambda b, s: (b, s, 0)
    seq_map = lambda b, s: (b, 0, 0)
  const2 = lambda b, s: (0, 0)
  once = pl.Buffered(1)

  smem = pl.BlockSpec(memory_space=pltpu.SMEM)
  in_specs = [
      smem, smem,
      pl.BlockSpec((CHUNK, BAND), const2, pipeline_mode=once),
      pl.BlockSpec((None, MIX_ROWS, D_MODEL), x_map),
      pl.BlockSpec((1, D_MODEL), const2, pipeline_mode=once),
      pl.BlockSpec((D_MODEL, IN_COLS), const2, pipeline_mode=once),
      pl.BlockSpec((CONV_K, CONV_WIDTH), const2, pipeline_mode=once),
  ]
  args = [rel_table, sinks, bkt, x, g_pre, w_in, conv_w]
  if has_state:
    in_specs += [
        pl.BlockSpec((n_seg, CONV_K - 1, CONV_WIDTH), seq_map),
        pl.BlockSpec((n_seg, WINDOW, KV_WIDTH), seq_map),
        pl.BlockSpec((n_seg, WINDOW, KV_WIDTH), seq_map),
    ]
    args += [state, cache_k, cache_v]
  cast_in, cast_out, cast_shapes = _cast_specs(cast_weights, grid[0] * grid[1], lambda b, s: b * grid[1] + s)
  in_specs += cast_in
  args += [w for group in cast_weights for w in group]
  out_shape = (
      jax.ShapeDtypeStruct(x.shape, BF16),
      jax.ShapeDtypeStruct((n_seq, CONV_K - 1, CONV_WIDTH), F32),
      jax.ShapeDtypeStruct((n_seq, WINDOW, KV_WIDTH), F32),
      jax.ShapeDtypeStruct((n_seq, WINDOW, KV_WIDTH), F32),
      *cast_shapes,
  )
  out_specs = (
      pl.BlockSpec((None, MIX_ROWS, D_MODEL), x_map),
      pl.BlockSpec((n_seg, CONV_K - 1, CONV_WIDTH), seq_map),
      pl.BlockSpec((n_seg, WINDOW, KV_WIDTH), seq_map),
      pl.BlockSpec((n_seg, WINDOW, KV_WIDTH), seq_map),
      *cast_out,
  )
  band_rows = n_seg * (WINDOW + seg_len)
  scratch = [
      pltpu.VMEM((N_KV_HEADS, GROUP * CHUNK, BAND), F32),
      pltpu.VMEM((MIX_ROWS, D_MODEL), BF16),
      pltpu.VMEM((MIX_ROWS, 3 * CONV_WIDTH), F32),
      pltpu.VMEM((N_HEADS, MIX_ROWS, HEAD_DIM), BF16),
      pltpu.VMEM((MIX_ROWS, 2 * KV_WIDTH), F32),
      pltpu.VMEM((N_KV_HEADS, band_rows, HEAD_DIM), BF16),
      pltpu.VMEM((N_KV_HEADS, band_rows, HEAD_DIM), BF16),
      pltpu.VMEM((n_seg * (G_PAD + seg_len), CONV_WIDTH), F32),
  ]
  kernel = functools.partial(_mixer_kernel, n_seg=n_seg, seg_len=seg_len, has_state=has_state,
                             cast_groups=tuple(len(group) for group in cast_weights))
  return pl.pallas_call(
      kernel,
      grid=grid,
      in_specs=in_specs,
      out_specs=out_specs,
      out_shape=out_shape,
      scratch_shapes=scratch,
      compiler_params=pltpu.CompilerParams(
          dimension_semantics=("arbitrary", "arbitrary"),
          vmem_limit_bytes=V7X_VMEM_LIMIT_BYTES),
      name="mixer_state" if has_state else "mixer_stream",
  )(*args)


def _outproj_kernel(mixed_ref, x_ref, wout_ref, gpm_ref, gpf_ref, *rest, cast_groups):
  rest = list(rest)
  cast_src, rest = rest[:sum(cast_groups)], rest[sum(cast_groups):]
  (x1_ref, hn_ref), rest = rest[:2], rest[2:]
  cast_dst, (acc_ref,) = rest[:len(cast_groups)], rest[len(cast_groups):]
  _cast_all(cast_src, cast_dst, cast_groups)
  gpm = gpm_ref[...]
  gpf = gpf_ref[...]
  part = mixed_ref.shape[0] // OUT_PARTS
  for r0 in range(0, mixed_ref.shape[0], part):
    acc_ref[r0:r0 + part, :] = jnp.dot(
        mixed_ref[r0:r0 + part, :], wout_ref[...], preferred_element_type=F32)
    for r in range(r0, r0 + part, NORM_ROWS):
      x1 = x_ref[r:r + NORM_ROWS, :] + _rms_scale(acc_ref[r:r + NORM_ROWS, :], gpm)
      x1_ref[r:r + NORM_ROWS, :] = x1
      hn_ref[r:r + NORM_ROWS, :] = _rms_scale(x1, gpf).astype(BF16)


def _outproj_call(mixed, x, w_out, g_post_mix, g_pre_ffn, cast_weights=()):
  rows = x.shape[0]
  row_map = lambda i: (i, 0)
  const2 = lambda i: (0, 0)
  once = pl.Buffered(1)
  cast_in, cast_out, cast_shapes = _cast_specs(cast_weights, rows // OUT_ROWS, lambda i: i)
  return pl.pallas_call(
      functools.partial(_outproj_kernel, cast_groups=tuple(len(group) for group in cast_weights)),
      grid=(rows // OUT_ROWS,),
      in_specs=[
          pl.BlockSpec((OUT_ROWS, D_MODEL), row_map),
          pl.BlockSpec((OUT_ROWS, D_MODEL), row_map),
          pl.BlockSpec((D_MODEL, D_MODEL), const2, pipeline_mode=once),
          pl.BlockSpec((1, D_MODEL), const2, pipeline_mode=once),
          pl.BlockSpec((1, D_MODEL), const2, pipeline_mode=once),
          *cast_in,
      ],
      out_specs=(pl.BlockSpec((OUT_ROWS, D_MODEL), row_map), pl.BlockSpec((OUT_ROWS, D_MODEL), row_map),
                 *cast_out),
      out_shape=(jax.ShapeDtypeStruct((rows, D_MODEL), F32), jax.ShapeDtypeStruct((rows, D_MODEL), BF16),
                 *cast_shapes),
      scratch_shapes=[pltpu.VMEM((OUT_ROWS, D_MODEL), F32)],
      compiler_params=pltpu.CompilerParams(
          dimension_semantics=("arbitrary",), vmem_limit_bytes=V7X_VMEM_LIMIT_BYTES),
      name="outproj",
  )(mixed, x, w_out, g_post_mix, g_pre_ffn, *[w for group in cast_weights for w in group])


def _ffn_tile(i, j):
  return jnp.where(i % 2 == 0, j, D_FF // FFN_TILE - 1 - j)


def _ffn_kernel(x1_ref, hn_ref, gqf_ref, wgu_ref, wd_ref, y_ref):
  j = pl.program_id(1)
  rows = y_ref.shape[0]

  last = pl.num_programs(1) - 1

  def swiglu():
    gate_up = jnp.dot(hn_ref[...], wgu_ref[...], preferred_element_type=F32)
    gate, up = gate_up[:, :FFN_TILE], gate_up[:, FFN_TILE:]
    return (gate * (1.0 / (1.0 + jnp.exp(-gate))) * up).astype(BF16)

  @pl.when(j == 0)
  def _():
    y_ref[...] = jnp.dot(swiglu(), wd_ref[...], preferred_element_type=F32)

  @pl.when((j > 0) & (j < last))
  def _():
    y_ref[...] += jnp.dot(swiglu(), wd_ref[...], preferred_element_type=F32)

  @pl.when(j == last)
  def _():
    mid = swiglu()
    gqf = gqf_ref[...]
    part = rows // FFN_LAST_PARTS
    for r0 in range(0, rows, part):
      y_ref[r0:r0 + part, :] += jnp.dot(mid[r0:r0 + part, :], wd_ref[...], preferred_element_type=F32)
      for r in range(r0, r0 + part, NORM_ROWS):
        y_ref[r:r + NORM_ROWS, :] = x1_ref[r:r + NORM_ROWS, :] + _rms_scale(y_ref[r:r + NORM_ROWS, :], gqf)


def _ffn_call(x1, hn, g_post_ffn, w_gate_up, w_down):
  rows = x1.shape[0]
  assert D_FF % FFN_TILE == 0
  row_map = lambda i, j: (i, 0)
  return pl.pallas_call(
      _ffn_kernel,
      grid=(rows // FFN_ROWS, D_FF // FFN_TILE),
      in_specs=[
          pl.BlockSpec((FFN_ROWS, D_MODEL), row_map),
          pl.BlockSpec((FFN_ROWS, D_MODEL), row_map),
          pl.BlockSpec((1, D_MODEL), lambda i, j: (0, 0), pipeline_mode=pl.Buffered(1)),
          pl.BlockSpec((D_MODEL, 2 * FFN_TILE), lambda i, j: (0, _ffn_tile(i, j))),
          pl.BlockSpec((FFN_TILE, D_MODEL), lambda i, j: (_ffn_tile(i, j), 0)),
      ],
      out_specs=pl.BlockSpec((FFN_ROWS, D_MODEL), row_map),
      out_shape=jax.ShapeDtypeStruct((rows, D_MODEL), F32),
      compiler_params=pltpu.CompilerParams(
          dimension_semantics=("arbitrary", "arbitrary"),
          vmem_limit_bytes=V7X_VMEM_LIMIT_BYTES),
      name="ffn",
  )(x1, hn, g_post_ffn, w_gate_up, w_down)


def kernel(x_prompt, x_sample, state_conv, cache_k, cache_v, rel_table, g_pre_mix, w_in, conv_w,
           attn_sinks, w_out, g_post_mix, g_pre_ffn, w_gate, w_up, w_down, g_post_ffn):
  depth, n_dec, cache_len = cache_k.shape[:3]
  batch, seq_len, _ = x_prompt.shape
  assert depth == 1 and cache_len == WINDOW and x_sample.shape[1] == CHUNK
  assert seq_len % MIX_ROWS == 0 and MIX_ROWS % CHUNK == 0

  rel = (np.arange(BAND) - WINDOW)[None, :] - np.arange(CHUNK)[:, None]
  bkt = jnp.asarray(_t5_bucket(rel), jnp.int32)

  row = lambda g: g[0].reshape(1, D_MODEL)
  win = w_in[0].astype(BF16)
  mix_args = (bkt, rel_table, attn_sinks[0], row(g_pre_mix), win, conv_w[0])
  mixed_p, conv_p, k_p, v_p, wgu, wo = _mixer_call(
      x_prompt, None, None, None, *mix_args, has_state=False,
      cast_weights=((w_gate[0], w_up[0]), (w_out[0],)))
  mixed_s, conv_s, k_s, v_s = _mixer_call(
      x_sample, state_conv[0], cache_k[0].reshape(n_dec, WINDOW, KV_WIDTH),
      cache_v[0].reshape(n_dec, WINDOW, KV_WIDTH), *mix_args, has_state=True)

  out_args = (wo, row(g_post_mix), row(g_pre_ffn))
  x1_p, hn_p, wd = _outproj_call(
      mixed_p.reshape(-1, D_MODEL), x_prompt.reshape(-1, D_MODEL), *out_args, cast_weights=((w_down[0],),))
  x1_s, hn_s = _outproj_call(mixed_s.reshape(-1, D_MODEL), x_sample.reshape(-1, D_MODEL), *out_args)
  ffn_args = (row(g_post_ffn), wgu, wd)
  y_p = _ffn_call(x1_p, hn_p, *ffn_args)
  y_s = _ffn_call(x1_s, hn_s, *ffn_args)

  heads = lambda a: a.reshape(1, a.shape[0], WINDOW, N_KV_HEADS, HEAD_DIM)
  return (y_p.reshape(x_prompt.shape), y_s.reshape(x_sample.shape),
          conv_p[None], heads(k_p), heads(v_p), conv_s[None], heads(k_s), heads(v_s))
```

```python
import functools
import math

import jax
import jax.numpy as jnp
import numpy as np
from jax import lax
from jax.experimental import pallas as pl
from jax.experimental.pallas import tpu as pltpu

D_MODEL = 2048
CHUNK = 64
HEAD_DIM = 64
CONV_WIDTH = D_MODEL // 2
CONV_K = 3
N_HEADS = (D_MODEL - CONV_WIDTH) // HEAD_DIM
N_KV_HEADS = 4
GROUP = N_HEADS // N_KV_HEADS
ATTN_WIDTH = N_HEADS * HEAD_DIM
KV_WIDTH = N_KV_HEADS * HEAD_DIM
IN_COLS = 3 * CONV_WIDTH + ATTN_WIDTH + 2 * KV_WIDTH
WINDOW = 128
BAND = WINDOW + CHUNK
NUM_BUCKETS = 32
MAX_DISTANCE = 128
D_FF = -(-8 * D_MODEL // (3 * 256)) * 256
EPS = 1e-6
NEG = -1e30

OFF_CB, OFF_CC, OFF_CU = 0, CONV_WIDTH, 2 * CONV_WIDTH
OFF_Q = 3 * CONV_WIDTH
OFF_KV = OFF_Q + ATTN_WIDTH

SUBLANES = 8
BF16_ROWS = 16
V7X_VMEM_LIMIT_BYTES = 61 << 20
MIX_ROWS = 256
OUT_ROWS = 512
OUT_PARTS = 4
FFN_ROWS = 1024
FFN_TILE = 512
FFN_LAST_PARTS = 4
CONV_COLS = 512
NORM_ROWS = 16
CONV_ROWS = 64
G_PAD = SUBLANES

F32 = jnp.float32
BF16 = jnp.bfloat16


def _t5_bucket(rel):
  half = NUM_BUCKETS // 2
  ret = np.where(rel > 0, half, 0)
  n = np.abs(rel)
  max_exact = half // 2
  nf = np.maximum(n, 1).astype(np.float32)
  large = max_exact + (np.log(nf / np.float32(max_exact)) / np.float32(math.log(MAX_DISTANCE / max_exact))
                       * np.float32(half - max_exact)).astype(np.int32)
  large = np.minimum(large, half - 1)
  return ret + np.where(n < max_exact, n, large)


def _rms_scale(y, gain):
  return y * lax.rsqrt(jnp.mean(y * y, axis=-1, keepdims=True) + EPS) * gain


def _mixer_kernel(tbl_ref, sink_ref, bkt_ref, x_ref, gpre_ref, win_ref, convw_ref, *rest,
                  n_seg, seg_len, has_state, cast_groups):
  rest = list(rest)
  if has_state:
    state_ref, ck_ref, cv_ref = rest[:3]
    rest = rest[3:]
  cast_src, rest = rest[:sum(cast_groups)], rest[sum(cast_groups):]
  (mixed_ref, nconv_ref, nk_ref, nv_ref), rest = rest[:4], rest[4:]
  cast_dst, rest = rest[:len(cast_groups)], rest[len(cast_groups):]
  bias_ref, h_ref, zc_ref, q_ref, kvf_ref, kb_ref, vb_ref, gs_ref = rest
  rows = n_seg * seg_len
  g_stride = G_PAD + seg_len
  b_stride = WINDOW + seg_len
  first_call = (pl.program_id(0) == 0) & (pl.program_id(1) == 0)
  seq_start = pl.program_id(1) == 0

  @pl.when(first_call)
  def _():
    bkt = bkt_ref[...]
    for h in range(N_HEADS):
      acc = jnp.zeros((CHUNK, BAND), F32)
      for j in range(NUM_BUCKETS):
        acc = jnp.where(bkt == j, tbl_ref[j, h], acc)
      g = h % GROUP
      bias_ref[h // GROUP, g * CHUNK:(g + 1) * CHUNK, :] = acc

  if has_state:
    for s in range(n_seg):
      gs_ref[s * g_stride + G_PAD - (CONV_K - 1):s * g_stride + G_PAD, :] = state_ref[s]
      for kh in range(N_KV_HEADS):
        kb_ref[kh, s * b_stride:s * b_stride + WINDOW, :] = (
            ck_ref[s, :, kh * HEAD_DIM:(kh + 1) * HEAD_DIM].astype(BF16))
        vb_ref[kh, s * b_stride:s * b_stride + WINDOW, :] = (
            cv_ref[s, :, kh * HEAD_DIM:(kh + 1) * HEAD_DIM].astype(BF16))
  else:
    @pl.when(seq_start)
    def _():
      gs_ref[0:G_PAD, :] = jnp.zeros((G_PAD, CONV_WIDTH), F32)
      kb_ref[:, 0:WINDOW, :] = jnp.zeros((N_KV_HEADS, WINDOW, HEAD_DIM), BF16)
      vb_ref[:, 0:WINDOW, :] = jnp.zeros((N_KV_HEADS, WINDOW, HEAD_DIM), BF16)

    @pl.when(jnp.logical_not(seq_start))
    def _():
      gs_ref[0:G_PAD, :] = gs_ref[seg_len:seg_len + G_PAD, :]
      kb_ref[:, 0:WINDOW, :] = kb_ref[:, seg_len:seg_len + WINDOW, :]
      vb_ref[:, 0:WINDOW, :] = vb_ref[:, seg_len:seg_len + WINDOW, :]

  gpre = gpre_ref[...]
  for r in range(0, rows, NORM_ROWS):
    h_ref[r:r + NORM_ROWS, :] = _rms_scale(x_ref[r:r + NORM_ROWS, :], gpre).astype(BF16)

  _cast_all(cast_src, cast_dst, cast_groups)

  for c0 in range(0, ATTN_WIDTH, CONV_COLS):
    zc_ref[:, 0:CONV_COLS] = jnp.dot(
        h_ref[...], win_ref[:, OFF_Q + c0:OFF_Q + c0 + CONV_COLS], preferred_element_type=F32)
    for hh in range(CONV_COLS // HEAD_DIM):
      q_ref[c0 // HEAD_DIM + hh] = (
          zc_ref[:, hh * HEAD_DIM:(hh + 1) * HEAD_DIM] * (HEAD_DIM ** -0.5)).astype(BF16)

  kvf_ref[...] = jnp.dot(h_ref[...], win_ref[:, OFF_KV:OFF_KV + 2 * KV_WIDTH], preferred_element_type=F32)
  for s in range(n_seg):
    for kh in range(N_KV_HEADS):
      kb_ref[kh, s * b_stride + WINDOW:(s + 1) * b_stride, :] = (
          kvf_ref[s * seg_len:(s + 1) * seg_len, kh * HEAD_DIM:(kh + 1) * HEAD_DIM].astype(BF16))
      vb_ref[kh, s * b_stride + WINDOW:(s + 1) * b_stride, :] = (
          kvf_ref[s * seg_len:(s + 1) * seg_len,
                  KV_WIDTH + kh * HEAD_DIM:KV_WIDTH + (kh + 1) * HEAD_DIM].astype(BF16))

  for s in range(n_seg):
    if seg_len >= WINDOW:
      nk_ref[s] = kvf_ref[(s + 1) * seg_len - WINDOW:(s + 1) * seg_len, 0:KV_WIDTH]
      nv_ref[s] = kvf_ref[(s + 1) * seg_len - WINDOW:(s + 1) * seg_len, KV_WIDTH:2 * KV_WIDTH]
    else:
      nk_ref[s, 0:WINDOW - seg_len, :] = ck_ref[s, seg_len:WINDOW, :]
      nv_ref[s, 0:WINDOW - seg_len, :] = cv_ref[s, seg_len:WINDOW, :]
      nk_ref[s, WINDOW - seg_len:WINDOW, :] = kvf_ref[s * seg_len:(s + 1) * seg_len, 0:KV_WIDTH]
      nv_ref[s, WINDOW - seg_len:WINDOW, :] = kvf_ref[s * seg_len:(s + 1) * seg_len, KV_WIDTH:2 * KV_WIDTH]

  row_id = lax.broadcasted_iota(jnp.int32, (GROUP * CHUNK, 1), 0)
  sink_cols = []
  for kh in range(N_KV_HEADS):
    col = jnp.full((GROUP * CHUNK, 1), sink_ref[kh * GROUP + GROUP - 1], F32)
    for g in range(GROUP - 2, -1, -1):
      col = jnp.where(row_id < (g + 1) * CHUNK, sink_ref[kh * GROUP + g], col)
    sink_cols.append(col)
  n_items = rows // CHUNK
  assert n_seg == 1 or seg_len == CHUNK
  band_step = CHUNK if n_seg == 1 else b_stride

  key_row = lax.broadcasted_iota(jnp.int32, (1, BAND), 1)

  def attend(item):
    q0 = item * CHUNK
    b0 = item * band_step
    masked_keys = 0 if has_state else max(WINDOW - item * CHUNK, 0)
    if masked_keys:
      start_mask = jnp.where(seq_start & (key_row < masked_keys), NEG, 0.0)
    for kh in range(N_KV_HEADS):
      qs = jnp.concatenate(
          [q_ref[kh * GROUP + g, q0:q0 + CHUNK, :] for g in range(GROUP)], axis=0)
      kband = kb_ref[kh, b0:b0 + BAND, :]
      vband = vb_ref[kh, b0:b0 + BAND, :]
      s = lax.dot_general(qs, kband, (((1,), (1,)), ((), ())), preferred_element_type=F32)
      s = s + bias_ref[kh]
      if masked_keys:
        s = s + start_mask
      sink = sink_cols[kh]
      m = jnp.maximum(jnp.max(s, axis=-1, keepdims=True), sink)
      p = jnp.exp(s - m)
      denom = jnp.sum(p, axis=-1, keepdims=True) + jnp.exp(sink - m)
      o = jnp.dot(p.astype(BF16), vband, preferred_element_type=F32) * (1.0 / denom)
      o = jnp.concatenate([o[g * CHUNK:(g + 1) * CHUNK, :] for g in range(GROUP)], axis=1)
      mixed_ref[q0:q0 + CHUNK, CONV_WIDTH + kh * GROUP * HEAD_DIM:
                CONV_WIDTH + (kh + 1) * GROUP * HEAD_DIM] = o.astype(BF16)

  def conv_group(c0):
    cols = slice(c0, c0 + CONV_COLS)
    cb_cols, cc_cols, cu_cols = (slice(off + c0, off + c0 + CONV_COLS) for off in (OFF_CB, OFF_CC, OFF_CU))
    for s in range(n_seg):
      for r in range(0, seg_len, CONV_ROWS):
        zr = s * seg_len + r
        gr = s * g_stride + G_PAD + r
        gs_ref[gr:gr + CONV_ROWS, cols] = (
            zc_ref[zr:zr + CONV_ROWS, cc_cols] * zc_ref[zr:zr + CONV_ROWS, cu_cols])
    for s in range(n_seg):
      for r in range(0, seg_len, CONV_ROWS):
        zr = s * seg_len + r
        gr = s * g_stride + G_PAD + r
        conv = convw_ref[0:1, cols] * gs_ref[gr - 2:gr - 2 + CONV_ROWS, cols]
        conv = conv + convw_ref[1:2, cols] * gs_ref[gr - 1:gr - 1 + CONV_ROWS, cols]
        conv = conv + convw_ref[2:3, cols] * gs_ref[gr:gr + CONV_ROWS, cols]
        mixed_ref[zr:zr + CONV_ROWS, cols] = (zc_ref[zr:zr + CONV_ROWS, cb_cols] * conv).astype(BF16)

  groups = range(0, CONV_WIDTH, CONV_COLS)
  items_per_group = n_items // len(groups)
  assert items_per_group * len(groups) == n_items
  for gi, c0 in enumerate(groups):
    for off in (OFF_CB, OFF_CC, OFF_CU):
      cols = slice(off + c0, off + c0 + CONV_COLS)
      zc_ref[:, cols] = jnp.dot(h_ref[...], win_ref[:, cols], preferred_element_type=F32)
    for item in range(gi * items_per_group, (gi + 1) * items_per_group):
      attend(item)
    conv_group(c0)

  for s in range(n_seg):
    g_end = s * g_stride + G_PAD + seg_len
    nconv_ref[s] = gs_ref[g_end - (CONV_K - 1):g_end, :]


def _cast_specs(groups, n_steps, step_of):
  in_specs, out_specs, out_shapes = [], [], []
  for group in groups:
    rows, cols = group[0].shape
    assert rows % (n_steps * BF16_ROWS) == 0 and all(w.shape == (rows, cols) for w in group)
    index_map = lambda *ids: (step_of(*ids), 0)
    in_specs += [pl.BlockSpec((rows // n_steps, cols), index_map) for _ in group]
    out_specs.append(pl.BlockSpec((rows // n_steps, cols * len(group)), index_map))
    out_shapes.append(jax.ShapeDtypeStruct((rows, cols * len(group)), BF16))
  return in_specs, out_specs, out_shapes


def _cast_group(src_refs, dst_ref):
  n = len(src_refs)
  for j in range(src_refs[0].shape[1] // FFN_TILE):
    for k, src in enumerate(src_refs):
      dst_ref[:, (j * n + k) * FFN_TILE:(j * n + k + 1) * FFN_TILE] = (
          src[:, j * FFN_TILE:(j + 1) * FFN_TILE].astype(BF16))


def _cast_all(src_refs, dst_refs, group_sizes):
  for size, dst in zip(group_sizes, dst_refs):
    _cast_group(src_refs[:size], dst)
    src_refs = src_refs[size:]


def _mixer_call(x, state, cache_k, cache_v, bkt, rel_table, sinks, g_pre, w_in, conv_w, *, has_state,
                cast_weights=()):
  n_seq, seq_len, _ = x.shape
  if has_state:
    n_seg, seg_len = MIX_ROWS // seq_len, seq_len
    grid = (n_seq // n_seg, 1)
    x = x.reshape(n_seq // n_seg, MIX_ROWS, D_MODEL)
    x_map = lambda b, s: (b, 0, 0)
    seq_map = lambda b, s: (b, 0, 0)
  else:
    n_seg, seg_len = 1, MIX_ROWS
    grid = (n_seq, seq_len // MIX_ROWS)
    x_map = lambda b, s: (b, s, 0)
    seq_map = lambda b, s: (b, 0, 0)
  const2 = lambda b, s: (0, 0)
  once = pl.Buffered(1)

  smem = pl.BlockSpec(memory_space=pltpu.SMEM)
  in_specs = [
      smem, smem,
      pl.BlockSpec((CHUNK, BAND), const2, pipeline_mode=once),
      pl.BlockSpec((None, MIX_ROWS, D_MODEL), x_map),
      pl.BlockSpec((1, D_MODEL), const2, pipeline_mode=once),
      pl.BlockSpec((D_MODEL, IN_COLS), const2, pipeline_mode=once),
      pl.BlockSpec((CONV_K, CONV_WIDTH), const2, pipeline_mode=once),
  ]
  args = [rel_table, sinks, bkt, x, g_pre, w_in, conv_w]
  if has_state:
    in_specs += [
        pl.BlockSpec((n_seg, CONV_K - 1, CONV_WIDTH), seq_map),
        pl.BlockSpec((n_seg, WINDOW, KV_WIDTH), seq_map),
        pl.BlockSpec((n_seg, WINDOW, KV_WIDTH), seq_map),
    ]
    args += [state, cache_k, cache_v]
  cast_in, cast_out, cast_shapes = _cast_specs(cast_weights, grid[0] * grid[1], lambda b, s: b * grid[1] + s)
  in_specs += cast_in
  args += [w for group in cast_weights for w in group]
  out_shape = (
      jax.ShapeDtypeStruct(x.shape, BF16),
      jax.ShapeDtypeStruct((n_seq, CONV_K - 1, CONV_WIDTH), F32),
      jax.ShapeDtypeStruct((n_seq, WINDOW, KV_WIDTH), F32),
      jax.ShapeDtypeStruct((n_seq, WINDOW, KV_WIDTH), F32),
      *cast_shapes,
  )
  out_specs = (
      pl.BlockSpec((None, MIX_ROWS, D_MODEL), x_map),
      pl.BlockSpec((n_seg, CONV_K - 1, CONV_WIDTH), seq_map),
      pl.BlockSpec((n_seg, WINDOW, KV_WIDTH), seq_map),
      pl.BlockSpec((n_seg, WINDOW, KV_WIDTH), seq_map),
      *cast_out,
  )
  band_rows = n_seg * (WINDOW + seg_len)
  scratch = [
      pltpu.VMEM((N_KV_HEADS, GROUP * CHUNK, BAND), F32),
      pltpu.VMEM((MIX_ROWS, D_MODEL), BF16),
      pltpu.VMEM((MIX_ROWS, 3 * CONV_WIDTH), F32),
      pltpu.VMEM((N_HEADS, MIX_ROWS, HEAD_DIM), BF16),
      pltpu.VMEM((MIX_ROWS, 2 * KV_WIDTH), F32),
      pltpu.VMEM((N_KV_HEADS, band_rows, HEAD_DIM), BF16),
      pltpu.VMEM((N_KV_HEADS, band_rows, HEAD_DIM), BF16),
      pltpu.VMEM((n_seg * (G_PAD + seg_len), CONV_WIDTH), F32),
  ]
  kernel = functools.partial(_mixer_kernel, n_seg=n_seg, seg_len=seg_len, has_state=has_state,
                             cast_groups=tuple(len(group) for group in cast_weights))
  return pl.pallas_call(
      kernel,
      grid=grid,
      in_specs=in_specs,
      out_specs=out_specs,
      out_shape=out_shape,
      scratch_shapes=scratch,
      compiler_params=pltpu.CompilerParams(
          dimension_semantics=("arbitrary", "arbitrary"),
          vmem_limit_bytes=V7X_VMEM_LIMIT_BYTES),
      name="mixer_state" if has_state else "mixer_stream",
  )(*args)


def _outproj_kernel(mixed_ref, x_ref, wout_ref, gpm_ref, gpf_ref, *rest, cast_groups):
  rest = list(rest)
  cast_src, rest = rest[:sum(cast_groups)], rest[sum(cast_groups):]
  (x1_ref, hn_ref), rest = rest[:2], rest[2:]
  cast_dst, (acc_ref,) = rest[:len(cast_groups)], rest[len(cast_groups):]
  _cast_all(cast_src, cast_dst, cast_groups)
  gpm = gpm_ref[...]
  gpf = gpf_ref[...]
  part = mixed_ref.shape[0] // OUT_PARTS
  for r0 in range(0, mixed_ref.shape[0], part):
    acc_ref[r0:r0 + part, :] = jnp.dot(
        mixed_ref[r0:r0 + part, :], wout_ref[...], preferred_element_type=F32)
    for r in range(r0, r0 + part, NORM_ROWS):
      x1 = x_ref[r:r + NORM_ROWS, :] + _rms_scale(acc_ref[r:r + NORM_ROWS, :], gpm)
      x1_ref[r:r + NORM_ROWS, :] = x1
      hn_ref[r:r + NORM_ROWS, :] = _rms_scale(x1, gpf).astype(BF16)


def _outproj_call(mixed, x, w_out, g_post_mix, g_pre_ffn, cast_weights=()):
  rows = x.shape[0]
  row_map = lambda i: (i, 0)
  const2 = lambda i: (0, 0)
  once = pl.Buffered(1)
  cast_in, cast_out, cast_shapes = _cast_specs(cast_weights, rows // OUT_ROWS, lambda i: i)
  return pl.pallas_call(
      functools.partial(_outproj_kernel, cast_groups=tuple(len(group) for group in cast_weights)),
      grid=(rows // OUT_ROWS,),
      in_specs=[
          pl.BlockSpec((OUT_ROWS, D_MODEL), row_map),
          pl.BlockSpec((OUT_ROWS, D_MODEL), row_map),
          pl.BlockSpec((D_MODEL, D_MODEL), const2, pipeline_mode=once),
          pl.BlockSpec((1, D_MODEL), const2, pipeline_mode=once),
          pl.BlockSpec((1, D_MODEL), const2, pipeline_mode=once),
          *cast_in,
      ],
      out_specs=(pl.BlockSpec((OUT_ROWS, D_MODEL), row_map), pl.BlockSpec((OUT_ROWS, D_MODEL), row_map),
                 *cast_out),
      out_shape=(jax.ShapeDtypeStruct((rows, D_MODEL), F32), jax.ShapeDtypeStruct((rows, D_MODEL), BF16),
                 *cast_shapes),
      scratch_shapes=[pltpu.VMEM((OUT_ROWS, D_MODEL), F32)],
      compiler_params=pltpu.CompilerParams(
          dimension_semantics=("arbitrary",), vmem_limit_bytes=V7X_VMEM_LIMIT_BYTES),
      name="outproj",
  )(mixed, x, w_out, g_post_mix, g_pre_ffn, *[w for group in cast_weights for w in group])


def _ffn_tile(i, j):
  return jnp.where(i % 2 == 0, j, D_FF // FFN_TILE - 1 - j)


def _ffn_kernel(x1_ref, hn_ref, gqf_ref, wgu_ref, wd_ref, y_ref):
  j = pl.program_id(1)
  rows = y_ref.shape[0]

  last = pl.num_programs(1) - 1

  def swiglu():
    gate_up = jnp.dot(hn_ref[...], wgu_ref[...], preferred_element_type=F32)
    gate, up = gate_up[:, :FFN_TILE], gate_up[:, FFN_TILE:]
    return (gate * (1.0 / (1.0 + jnp.exp(-gate))) * up).astype(BF16)

  @pl.when(j == 0)
  def _():
    y_ref[...] = jnp.dot(swiglu(), wd_ref[...], preferred_element_type=F32)

  @pl.when((j > 0) & (j < last))
  def _():
    y_ref[...] += jnp.dot(swiglu(), wd_ref[...], preferred_element_type=F32)

  @pl.when(j == last)
  def _():
    mid = swiglu()
    gqf = gqf_ref[...]
    part = rows // FFN_LAST_PARTS
    for r0 in range(0, rows, part):
      y_ref[r0:r0 + part, :] += jnp.dot(mid[r0:r0 + part, :], wd_ref[...], preferred_element_type=F32)
      for r in range(r0, r0 + part, NORM_ROWS):
        y_ref[r:r + NORM_ROWS, :] = x1_ref[r:r + NORM_ROWS, :] + _rms_scale(y_ref[r:r + NORM_ROWS, :], gqf)


def _ffn_call(x1, hn, g_post_ffn, w_gate_up, w_down):
  rows = x1.shape[0]
  assert D_FF % FFN_TILE == 0
  row_map = lambda i, j: (i, 0)
  return pl.pallas_call(
      _ffn_kernel,
      grid=(rows // FFN_ROWS, D_FF // FFN_TILE),
      in_specs=[
          pl.BlockSpec((FFN_ROWS, D_MODEL), row_map),
          pl.BlockSpec((FFN_ROWS, D_MODEL), row_map),
          pl.BlockSpec((1, D_MODEL), lambda i, j: (0, 0), pipeline_mode=pl.Buffered(1)),
          pl.BlockSpec((D_MODEL, 2 * FFN_TILE), lambda i, j: (0, _ffn_tile(i, j))),
          pl.BlockSpec((FFN_TILE, D_MODEL), lambda i, j: (_ffn_tile(i, j), 0)),
      ],
      out_specs=pl.BlockSpec((FFN_ROWS, D_MODEL), row_map),
      out_shape=jax.ShapeDtypeStruct((rows, D_MODEL), F32),
      compiler_params=pltpu.CompilerParams(
          dimension_semantics=("arbitrary", "arbitrary"),
          vmem_limit_bytes=V7X_VMEM_LIMIT_BYTES),
      name="ffn",
  )(x1, hn, g_post_ffn, w_gate_up, w_down)


def kernel(x_prompt, x_sample, state_conv, cache_k, cache_v, rel_table, g_pre_mix, w_in, conv_w,
           attn_sinks, w_out, g_post_mix, g_pre_ffn, w_gate, w_up, w_down, g_post_ffn):
  depth, n_dec, cache_len = cache_k.shape[:3]
  batch, seq_len, _ = x_prompt.shape
  assert depth == 1 and cache_len == WINDOW and x_sample.shape[1] == CHUNK
  assert seq_len % MIX_ROWS == 0 and MIX_ROWS % CHUNK == 0

  rel = (np.arange(BAND) - WINDOW)[None, :] - np.arange(CHUNK)[:, None]
  bkt = jnp.asarray(_t5_bucket(rel), jnp.int32)

  row = lambda g: g[0].reshape(1, D_MODEL)
  win = w_in[0].astype(BF16)
  mix_args = (bkt, rel_table, attn_sinks[0], row(g_pre_mix), win, conv_w[0])
  mixed_p, conv_p, k_p, v_p, wgu, wo = _mixer_call(
      x_prompt, None, None, None, *mix_args, has_state=False,
      cast_weights=((w_gate[0], w_up[0]), (w_out[0],)))
  mixed_s, conv_s, k_s, v_s = _mixer_call(
      x_sample, state_conv[0], cache_k[0].reshape(n_dec, WINDOW, KV_WIDTH),
      cache_v[0].reshape(n_dec, WINDOW, KV_WIDTH), *mix_args, has_state=True)

  out_args = (wo, row(g_post_mix), row(g_pre_ffn))
  x1_p, hn_p, wd = _outproj_call(
      mixed_p.reshape(-1, D_MODEL), x_prompt.reshape(-1, D_MODEL), *out_args, cast_weights=((w_down[0],),))
  x1_s, hn_s = _outproj_call(mixed_s.reshape(-1, D_MODEL), x_sample.reshape(-1, D_MODEL), *out_args)
  ffn_args = (row(g_post_ffn), wgu, wd)
  y_p = _ffn_call(x1_p, hn_p, *ffn_args)
  y_s = _ffn_call(x1_s, hn_s, *ffn_args)

  heads = lambda a: a.reshape(1, a.shape[0], WINDOW, N_KV_HEADS, HEAD_DIM)
  return (y_p.reshape(x_prompt.shape), y_s.reshape(x_sample.shape),
          conv_p[None], heads(k_p), heads(v_p), conv_s[None], heads(k_s), heads(v_s))
```

```python
import functools
import math

import jax
import jax.numpy as jnp
import numpy as np
from jax import lax
from jax.experimental import pallas as pl
from jax.experimental.pallas import tpu as pltpu

D_MODEL = 2048
CHUNK = 64
HEAD_DIM = 64
CONV_WIDTH = D_MODEL // 2
CONV_K = 3
N_HEADS = (D_MODEL - CONV_WIDTH) // HEAD_DIM
N_KV_HEADS = 4
GROUP = N_HEADS // N_KV_HEADS
ATTN_WIDTH = N_HEADS * HEAD_DIM
KV_WIDTH = N_KV_HEADS * HEAD_DIM
IN_COLS = 3 * CONV_WIDTH + ATTN_WIDTH + 2 * KV_WIDTH
WINDOW = 128
BAND = WINDOW + CHUNK
NUM_BUCKETS = 32
MAX_DISTANCE = 128
D_FF = -(-8 * D_MODEL // (3 * 256)) * 256
EPS = 1e-6
NEG = -1e30

OFF_CB, OFF_CC, OFF_CU = 0, CONV_WIDTH, 2 * CONV_WIDTH
OFF_Q = 3 * CONV_WIDTH
OFF_KV = OFF_Q + ATTN_WIDTH

SUBLANES = 8
BF16_ROWS = 16
V7X_VMEM_LIMIT_BYTES = 61 << 20
MIX_ROWS = 512
OUT_ROWS = 512
OUT_PARTS = 4
FFN_ROWS = 1024
FFN_TILE = 512
FFN_LAST_PARTS = 4
CONV_COLS = 512
NORM_ROWS = 16
CONV_ROWS = 64
SOFTMAX_ROWS = 128
G_PAD = SUBLANES

F32 = jnp.float32
BF16 = jnp.bfloat16


def _t5_bucket(rel):
  half = NUM_BUCKETS // 2
  ret = np.where(rel > 0, half, 0)
  n = np.abs(rel)
  max_exact = half // 2
  nf = np.maximum(n, 1).astype(np.float32)
  large = max_exact + (np.log(nf / np.float32(max_exact)) / np.float32(math.log(MAX_DISTANCE / max_exact))
                       * np.float32(half - max_exact)).astype(np.int32)
  large = np.minimum(large, half - 1)
  return ret + np.where(n < max_exact, n, large)


def _rms_scale(y, gain):
  return y * lax.rsqrt(jnp.mean(y * y, axis=-1, keepdims=True) + EPS) * gain


def _mixer_kernel(tbl_ref, sink_ref, bkt_ref, x_ref, gpre_ref, win_ref, convw_ref, *rest,
                  n_seg, seg_len, has_state, cast_groups):
  rest = list(rest)
  if has_state:
    state_ref, ck_ref, cv_ref = rest[:3]
    rest = rest[3:]
  cast_src, rest = rest[:sum(cast_groups)], rest[sum(cast_groups):]
  (mixed_ref, nconv_ref, nk_ref, nv_ref), rest = rest[:4], rest[4:]
  cast_dst, rest = rest[:len(cast_groups)], rest[len(cast_groups):]
  bias_ref, h_ref, zc_ref, q_ref, kvf_ref, kb_ref, vb_ref, gs_ref = rest
  rows = n_seg * seg_len
  g_stride = G_PAD + seg_len
  b_stride = WINDOW + seg_len
  first_call = (pl.program_id(0) == 0) & (pl.program_id(1) == 0)
  seq_start = pl.program_id(1) == 0

  @pl.when(first_call)
  def _():
    bkt = bkt_ref[...]
    for h in range(N_HEADS):
      acc = jnp.zeros((CHUNK, BAND), F32)
      for j in range(NUM_BUCKETS):
        acc = jnp.where(bkt == j, tbl_ref[j, h], acc)
      g = h % GROUP
      bias_ref[h // GROUP, g * CHUNK:(g + 1) * CHUNK, :] = acc

  if has_state:
    for s in range(n_seg):
      gs_ref[s * g_stride + G_PAD - (CONV_K - 1):s * g_stride + G_PAD, :] = state_ref[s]
      for kh in range(N_KV_HEADS):
        kb_ref[kh, s * b_stride:s * b_stride + WINDOW, :] = (
            ck_ref[s, :, kh * HEAD_DIM:(kh + 1) * HEAD_DIM].astype(BF16))
        vb_ref[kh, s * b_stride:s * b_stride + WINDOW, :] = (
            cv_ref[s, :, kh * HEAD_DIM:(kh + 1) * HEAD_DIM].astype(BF16))
  else:
    @pl.when(seq_start)
    def _():
      gs_ref[0:G_PAD, :] = jnp.zeros((G_PAD, CONV_WIDTH), F32)
      kb_ref[:, 0:WINDOW, :] = jnp.zeros((N_KV_HEADS, WINDOW, HEAD_DIM), BF16)
      vb_ref[:, 0:WINDOW, :] = jnp.zeros((N_KV_HEADS, WINDOW, HEAD_DIM), BF16)

    @pl.when(jnp.logical_not(seq_start))
    def _():
      gs_ref[0:G_PAD, :] = gs_ref[seg_len:seg_len + G_PAD, :]
      kb_ref[:, 0:WINDOW, :] = kb_ref[:, seg_len:seg_len + WINDOW, :]
      vb_ref[:, 0:WINDOW, :] = vb_ref[:, seg_len:seg_len + WINDOW, :]

  gpre = gpre_ref[...]
  for r in range(0, rows, NORM_ROWS):
    h_ref[r:r + NORM_ROWS, :] = _rms_scale(x_ref[r:r + NORM_ROWS, :], gpre).astype(BF16)

  _cast_all(cast_src, cast_dst, cast_groups)

  for c0 in range(0, ATTN_WIDTH, CONV_COLS):
    zc_ref[:, 0:CONV_COLS] = jnp.dot(
        h_ref[...], win_ref[:, OFF_Q + c0:OFF_Q + c0 + CONV_COLS], preferred_element_type=F32)
    for hh in range(CONV_COLS // HEAD_DIM):
      q_ref[c0 // HEAD_DIM + hh] = (
          zc_ref[:, hh * HEAD_DIM:(hh + 1) * HEAD_DIM] * (HEAD_DIM ** -0.5)).astype(BF16)

  kvf_ref[...] = jnp.dot(h_ref[...], win_ref[:, OFF_KV:OFF_KV + 2 * KV_WIDTH], preferred_element_type=F32)
  for s in range(n_seg):
    for kh in range(N_KV_HEADS):
      kb_ref[kh, s * b_stride + WINDOW:(s + 1) * b_stride, :] = (
          kvf_ref[s * seg_len:(s + 1) * seg_len, kh * HEAD_DIM:(kh + 1) * HEAD_DIM].astype(BF16))
      vb_ref[kh, s * b_stride + WINDOW:(s + 1) * b_stride, :] = (
          kvf_ref[s * seg_len:(s + 1) * seg_len,
                  KV_WIDTH + kh * HEAD_DIM:KV_WIDTH + (kh + 1) * HEAD_DIM].astype(BF16))

  for s in range(n_seg):
    if seg_len >= WINDOW:
      nk_ref[s] = kvf_ref[(s + 1) * seg_len - WINDOW:(s + 1) * seg_len, 0:KV_WIDTH]
      nv_ref[s] = kvf_ref[(s + 1) * seg_len - WINDOW:(s + 1) * seg_len, KV_WIDTH:2 * KV_WIDTH]
    else:
      nk_ref[s, 0:WINDOW - seg_len, :] = ck_ref[s, seg_len:WINDOW, :]
      nv_ref[s, 0:WINDOW - seg_len, :] = cv_ref[s, seg_len:WINDOW, :]
      nk_ref[s, WINDOW - seg_len:WINDOW, :] = kvf_ref[s * seg_len:(s + 1) * seg_len, 0:KV_WIDTH]
      nv_ref[s, WINDOW - seg_len:WINDOW, :] = kvf_ref[s * seg_len:(s + 1) * seg_len, KV_WIDTH:2 * KV_WIDTH]

  row_id = lax.broadcasted_iota(jnp.int32, (GROUP * CHUNK, 1), 0)
  sink_cols = []
  for kh in range(N_KV_HEADS):
    col = jnp.full((GROUP * CHUNK, 1), sink_ref[kh * GROUP + GROUP - 1], F32)
    for g in range(GROUP - 2, -1, -1):
      col = jnp.where(row_id < (g + 1) * CHUNK, sink_ref[kh * GROUP + g], col)
    sink_cols.append(col)
  n_items = rows // CHUNK
  assert n_seg == 1 or seg_len == CHUNK
  band_step = CHUNK if n_seg == 1 else b_stride

  key_row = lax.broadcasted_iota(jnp.int32, (1, BAND), 1)

  def attend(item):
    q0 = item * CHUNK
    b0 = item * band_step
    masked_keys = 0 if has_state else max(WINDOW - item * CHUNK, 0)
    if masked_keys:
      start_mask = jnp.where(seq_start & (key_row < masked_keys), NEG, 0.0)
    for kh in range(N_KV_HEADS):
      qs = jnp.concatenate(
          [q_ref[kh * GROUP + g, q0:q0 + CHUNK, :] for g in range(GROUP)], axis=0)
      kband = kb_ref[kh, b0:b0 + BAND, :]
      vband = vb_ref[kh, b0:b0 + BAND, :]
      s = lax.dot_general(qs, kband, (((1,), (1,)), ((), ())), preferred_element_type=F32)
      s = s + bias_ref[kh]
      if masked_keys:
        s = s + start_mask
      outs = []
      for h0 in range(0, GROUP * CHUNK, SOFTMAX_ROWS):
        sh = s[h0:h0 + SOFTMAX_ROWS, :]
        sink = sink_cols[kh][h0:h0 + SOFTMAX_ROWS, :]
        m = jnp.maximum(jnp.max(sh, axis=-1, keepdims=True), sink)
        p = jnp.exp(sh - m)
        denom = jnp.sum(p, axis=-1, keepdims=True) + jnp.exp(sink - m)
        outs.append(jnp.dot(p.astype(BF16), vband, preferred_element_type=F32) * (1.0 / denom))
      o = jnp.concatenate(outs, axis=0)
      o = jnp.concatenate([o[g * CHUNK:(g + 1) * CHUNK, :] for g in range(GROUP)], axis=1)
      mixed_ref[q0:q0 + CHUNK, CONV_WIDTH + kh * GROUP * HEAD_DIM:
                CONV_WIDTH + (kh + 1) * GROUP * HEAD_DIM] = o.astype(BF16)

  def conv_group(c0):
    cols = slice(c0, c0 + CONV_COLS)
    cb_cols, cc_cols, cu_cols = (slice(off + c0, off + c0 + CONV_COLS) for off in (OFF_CB, OFF_CC, OFF_CU))
    for s in range(n_seg):
      for r in range(0, seg_len, CONV_ROWS):
        zr = s * seg_len + r
        gr = s * g_stride + G_PAD + r
        gs_ref[gr:gr + CONV_ROWS, cols] = (
            zc_ref[zr:zr + CONV_ROWS, cc_cols] * zc_ref[zr:zr + CONV_ROWS, cu_cols])
    for s in range(n_seg):
      for r in range(0, seg_len, CONV_ROWS):
        zr = s * seg_len + r
        gr = s * g_stride + G_PAD + r
        conv = convw_ref[0:1, cols] * gs_ref[gr - 2:gr - 2 + CONV_ROWS, cols]
        conv = conv + convw_ref[1:2, cols] * gs_ref[gr - 1:gr - 1 + CONV_ROWS, cols]
        conv = conv + convw_ref[2:3, cols] * gs_ref[gr:gr + CONV_ROWS, cols]
        mixed_ref[zr:zr + CONV_ROWS, cols] = (zc_ref[zr:zr + CONV_ROWS, cb_cols] * conv).astype(BF16)

  pieces = [off + c0 for c0 in range(0, CONV_WIDTH, CONV_COLS) for off in (OFF_CB, OFF_CC, OFF_CU)]
  assert len(pieces) <= n_items
  for item in range(n_items):
    if item < len(pieces):
      cols = slice(pieces[item], pieces[item] + CONV_COLS)
      zc_ref[:, cols] = jnp.dot(h_ref[...], win_ref[:, cols], preferred_element_type=F32)
    attend(item)
    if item < len(pieces) and item % 3 == 2:
      conv_group(pieces[item] - OFF_CU)

  for s in range(n_seg):
    g_end = s * g_stride + G_PAD + seg_len
    nconv_ref[s] = gs_ref[g_end - (CONV_K - 1):g_end, :]


def _cast_specs(groups, n_steps, step_of):
  in_specs, out_specs, out_shapes = [], [], []
  for group in groups:
    rows, cols = group[0].shape
    assert rows % (n_steps * BF16_ROWS) == 0 and all(w.shape == (rows, cols) for w in group)
    index_map = lambda *ids: (step_of(*ids), 0)
    in_specs += [pl.BlockSpec((rows // n_steps, cols), index_map) for _ in group]
    out_specs.append(pl.BlockSpec((rows // n_steps, cols * len(group)), index_map))
    out_shapes.append(jax.ShapeDtypeStruct((rows, cols * len(group)), BF16))
  return in_specs, out_specs, out_shapes


def _cast_group(src_refs, dst_ref):
  n = len(src_refs)
  for j in range(src_refs[0].shape[1] // FFN_TILE):
    for k, src in enumerate(src_refs):
      dst_ref[:, (j * n + k) * FFN_TILE:(j * n + k + 1) * FFN_TILE] = (
          src[:, j * FFN_TILE:(j + 1) * FFN_TILE].astype(BF16))


def _cast_all(src_refs, dst_refs, group_sizes):
  for size, dst in zip(group_sizes, dst_refs):
    _cast_group(src_refs[:size], dst)
    src_refs = src_refs[size:]


def _mixer_call(x, state, cache_k, cache_v, bkt, rel_table, sinks, g_pre, w_in, conv_w, *, has_state,
                cast_weights=()):
  n_seq, seq_len, _ = x.shape
  if has_state:
    n_seg, seg_len = MIX_ROWS // seq_len, seq_len
    grid = (n_seq // n_seg, 1)
    x = x.reshape(n_seq // n_seg, MIX_ROWS, D_MODEL)
    x_map = lambda b, s: (b, 0, 0)
    seq_map = lambda b, s: (b, 0, 0)
  else:
    n_seg, seg_len = 1, MIX_ROWS
    grid = (n_seq, seq_len // MIX_ROWS)
    x_map = lambda b, s: (b, s, 0)
    seq_map = lambda b, s: (b, 0, 0)
  const2 = lambda b, s: (0, 0)
  once = pl.Buffered(1)

  smem = pl.BlockSpec(memory_space=pltpu.SMEM)
  in_specs = [
      smem, smem,
      pl.BlockSpec((CHUNK, BAND), const2, pipeline_mode=once),
      pl.BlockSpec((None, MIX_ROWS, D_MODEL), x_map),
      pl.BlockSpec((1, D_MODEL), const2, pipeline_mode=once),
      pl.BlockSpec((D_MODEL, IN_COLS), const2, pipeline_mode=once),
      pl.BlockSpec((CONV_K, CONV_WIDTH), const2, pipeline_mode=once),
  ]
  args = [rel_table, sinks, bkt, x, g_pre, w_in, conv_w]
  if has_state:
    in_specs += [
        pl.BlockSpec((n_seg, CONV_K - 1, CONV_WIDTH), seq_map),
        pl.BlockSpec((n_seg, WINDOW, KV_WIDTH), seq_map),
        pl.BlockSpec((n_seg, WINDOW, KV_WIDTH), seq_map),
    ]
    args += [state, cache_k, cache_v]
  cast_in, cast_out, cast_shapes = _cast_specs(cast_weights, grid[0] * grid[1], lambda b, s: b * grid[1] + s)
  in_specs += cast_in
  args += [w for group in cast_weights for w in group]
  out_shape = (
      jax.ShapeDtypeStruct(x.shape, BF16),
      jax.ShapeDtypeStruct((n_seq, CONV_K - 1, CONV_WIDTH), F32),
      jax.ShapeDtypeStruct((n_seq, WINDOW, KV_WIDTH), F32),
      jax.ShapeDtypeStruct((n_seq, WINDOW, KV_WIDTH), F32),
      *cast_shapes,
  )
  out_specs = (
      pl.BlockSpec((None, MIX_ROWS, D_MODEL), x_map),
      pl.BlockSpec((n_seg, CONV_K - 1, CONV_WIDTH), seq_map),
      pl.BlockSpec((n_seg, WINDOW, KV_WIDTH), seq_map),
      pl.BlockSpec((n_seg, WINDOW, KV_WIDTH), seq_map),
      *cast_out,
  )
  band_rows = n_seg * (WINDOW + seg_len)
  scratch = [
      pltpu.VMEM((N_KV_HEADS, GROUP * CHUNK, BAND), F32),
      pltpu.VMEM((MIX_ROWS, D_MODEL), BF16),
      pltpu.VMEM((MIX_ROWS, 3 * CONV_WIDTH), F32),
      pltpu.VMEM((N_HEADS, MIX_ROWS, HEAD_DIM), BF16),
      pltpu.VMEM((MIX_ROWS, 2 * KV_WIDTH), F32),
      pltpu.VMEM((N_KV_HEADS, band_rows, HEAD_DIM), BF16),
      pltpu.VMEM((N_KV_HEADS, band_rows, HEAD_DIM), BF16),
      pltpu.VMEM((n_seg * (G_PAD + seg_len), CONV_WIDTH), F32),
  ]
  kernel = functools.partial(_mixer_kernel, n_seg=n_seg, seg_len=seg_len, has_state=has_state,
                             cast_groups=tuple(len(group) for group in cast_weights))
  return pl.pallas_call(
      kernel,
      grid=grid,
      in_specs=in_specs,
      out_specs=out_specs,
      out_shape=out_shape,
      scratch_shapes=scratch,
      compiler_params=pltpu.CompilerParams(
          dimension_semantics=("arbitrary", "arbitrary"),
          vmem_limit_bytes=V7X_VMEM_LIMIT_BYTES),
      name="mixer_state" if has_state else "mixer_stream",
  )(*args)


def _outproj_kernel(mixed_ref, x_ref, wout_ref, gpm_ref, gpf_ref, *rest, cast_groups):
  rest = list(rest)
  cast_src, rest = rest[:sum(cast_groups)], rest[sum(cast_groups):]
  (x1_ref, hn_ref), rest = rest[:2], rest[2:]
  cast_dst, (acc_ref,) = rest[:len(cast_groups)], rest[len(cast_groups):]
  _cast_all(cast_src, cast_dst, cast_groups)
  gpm = gpm_ref[...]
  gpf = gpf_ref[...]
  part = mixed_ref.shape[0] // OUT_PARTS
  for r0 in range(0, mixed_ref.shape[0], part):
    acc_ref[r0:r0 + part, :] = jnp.dot(
        mixed_ref[r0:r0 + part, :], wout_ref[...], preferred_element_type=F32)
    for r in range(r0, r0 + part, NORM_ROWS):
      x1 = x_ref[r:r + NORM_ROWS, :] + _rms_scale(acc_ref[r:r + NORM_ROWS, :], gpm)
      x1_ref[r:r + NORM_ROWS, :] = x1
      hn_ref[r:r + NORM_ROWS, :] = _rms_scale(x1, gpf).astype(BF16)


def _outproj_call(mixed, x, w_out, g_post_mix, g_pre_ffn, cast_weights=()):
  rows = x.shape[0]
  row_map = lambda i: (i, 0)
  const2 = lambda i: (0, 0)
  once = pl.Buffered(1)
  cast_in, cast_out, cast_shapes = _cast_specs(cast_weights, rows // OUT_ROWS, lambda i: i)
  return pl.pallas_call(
      functools.partial(_outproj_kernel, cast_groups=tuple(len(group) for group in cast_weights)),
      grid=(rows // OUT_ROWS,),
      in_specs=[
          pl.BlockSpec((OUT_ROWS, D_MODEL), row_map),
          pl.BlockSpec((OUT_ROWS, D_MODEL), row_map),
          pl.BlockSpec((D_MODEL, D_MODEL), const2, pipeline_mode=once),
          pl.BlockSpec((1, D_MODEL), const2, pipeline_mode=once),
          pl.BlockSpec((1, D_MODEL), const2, pipeline_mode=once),
          *cast_in,
      ],
      out_specs=(pl.BlockSpec((OUT_ROWS, D_MODEL), row_map), pl.BlockSpec((OUT_ROWS, D_MODEL), row_map),
                 *cast_out),
      out_shape=(jax.ShapeDtypeStruct((rows, D_MODEL), F32), jax.ShapeDtypeStruct((rows, D_MODEL), BF16),
                 *cast_shapes),
      scratch_shapes=[pltpu.VMEM((OUT_ROWS, D_MODEL), F32)],
      compiler_params=pltpu.CompilerParams(
          dimension_semantics=("arbitrary",), vmem_limit_bytes=V7X_VMEM_LIMIT_BYTES),
      name="outproj",
  )(mixed, x, w_out, g_post_mix, g_pre_ffn, *[w for group in cast_weights for w in group])


def _ffn_tile(i, j):
  return jnp.where(i % 2 == 0, j, D_FF // FFN_TILE - 1 - j)


def _ffn_kernel(x1_ref, hn_ref, gqf_ref, wgu_ref, wd_ref, y_ref):
  j = pl.program_id(1)
  rows = y_ref.shape[0]

  last = pl.num_programs(1) - 1

  def swiglu():
    gate_up = jnp.dot(hn_ref[...], wgu_ref[...], preferred_element_type=F32)
    gate, up = gate_up[:, :FFN_TILE], gate_up[:, FFN_TILE:]
    return (gate * (1.0 / (1.0 + jnp.exp(-gate))) * up).astype(BF16)

  @pl.when(j == 0)
  def _():
    y_ref[...] = jnp.dot(swiglu(), wd_ref[...], preferred_element_type=F32)

  @pl.when((j > 0) & (j < last))
  def _():
    y_ref[...] += jnp.dot(swiglu(), wd_ref[...], preferred_element_type=F32)

  @pl.when(j == last)
  def _():
    mid = swiglu()
    gqf = gqf_ref[...]
    part = rows // FFN_LAST_PARTS
    for r0 in range(0, rows, part):
      y_ref[r0:r0 + part, :] += jnp.dot(mid[r0:r0 + part, :], wd_ref[...], preferred_element_type=F32)
      for r in range(r0, r0 + part, NORM_ROWS):
        y_ref[r:r + NORM_ROWS, :] = x1_ref[r:r + NORM_ROWS, :] + _rms_scale(y_ref[r:r + NORM_ROWS, :], gqf)


def _ffn_call(x1, hn, g_post_ffn, w_gate_up, w_down):
  rows = x1.shape[0]
  assert D_FF % FFN_TILE == 0
  row_map = lambda i, j: (i, 0)
  return pl.pallas_call(
      _ffn_kernel,
      grid=(rows // FFN_ROWS, D_FF // FFN_TILE),
      in_specs=[
          pl.BlockSpec((FFN_ROWS, D_MODEL), row_map),
          pl.BlockSpec((FFN_ROWS, D_MODEL), row_map),
          pl.BlockSpec((1, D_MODEL), lambda i, j: (0, 0), pipeline_mode=pl.Buffered(1)),
          pl.BlockSpec((D_MODEL, 2 * FFN_TILE), lambda i, j: (0, _ffn_tile(i, j))),
          pl.BlockSpec((FFN_TILE, D_MODEL), lambda i, j: (_ffn_tile(i, j), 0)),
      ],
      out_specs=pl.BlockSpec((FFN_ROWS, D_MODEL), row_map),
      out_shape=jax.ShapeDtypeStruct((rows, D_MODEL), F32),
      compiler_params=pltpu.CompilerParams(
          dimension_semantics=("arbitrary", "arbitrary"),
          vmem_limit_bytes=V7X_VMEM_LIMIT_BYTES),
      name="ffn",
  )(x1, hn, g_post_ffn, w_gate_up, w_down)


def kernel(x_prompt, x_sample, state_conv, cache_k, cache_v, rel_table, g_pre_mix, w_in, conv_w,
           attn_sinks, w_out, g_post_mix, g_pre_ffn, w_gate, w_up, w_down, g_post_ffn):
  depth, n_dec, cache_len = cache_k.shape[:3]
  batch, seq_len, _ = x_prompt.shape
  assert depth == 1 and cache_len == WINDOW and x_sample.shape[1] == CHUNK
  assert seq_len % MIX_ROWS == 0 and MIX_ROWS % CHUNK == 0

  rel = (np.arange(BAND) - WINDOW)[None, :] - np.arange(CHUNK)[:, None]
  bkt = jnp.asarray(_t5_bucket(rel), jnp.int32)

  row = lambda g: g[0].reshape(1, D_MODEL)
  win = w_in[0].astype(BF16)
  mix_args = (bkt, rel_table, attn_sinks[0], row(g_pre_mix), win, conv_w[0])
  mixed_p, conv_p, k_p, v_p, wgu, wo = _mixer_call(
      x_prompt, None, None, None, *mix_args, has_state=False,
      cast_weights=((w_gate[0], w_up[0]), (w_out[0],)))
  mixed_s, conv_s, k_s, v_s = _mixer_call(
      x_sample, state_conv[0], cache_k[0].reshape(n_dec, WINDOW, KV_WIDTH),
      cache_v[0].reshape(n_dec, WINDOW, KV_WIDTH), *mix_args, has_state=True)

  out_args = (wo, row(g_post_mix), row(g_pre_ffn))
  x1_p, hn_p, wd = _outproj_call(
      mixed_p.reshape(-1, D_MODEL), x_prompt.reshape(-1, D_MODEL), *out_args, cast_weights=((w_down[0],),))
  x1_s, hn_s = _outproj_call(mixed_s.reshape(-1, D_MODEL), x_sample.reshape(-1, D_MODEL), *out_args)
  ffn_args = (row(g_post_ffn), wgu, wd)
  y_p = _ffn_call(x1_p, hn_p, *ffn_args)
  y_s = _ffn_call(x1_s, hn_s, *ffn_args)

  heads = lambda a: a.reshape(1, a.shape[0], WINDOW, N_KV_HEADS, HEAD_DIM)
  return (y_p.reshape(x_prompt.shape), y_s.reshape(x_sample.shape),
          conv_p[None], heads(k_p), heads(v_p), conv_s[None], heads(k_s), heads(v_s))
```
